```python
import math
import jax, jax.numpy as jnp
from jax import lax
import numpy as np

D_MODEL = 2048
BATCH = 16
SEQ = 2048
DEPTH = 2

RW_HEADS = 8
RW_HEAD_DIM = 64
RW_WIDTH = RW_HEADS * RW_HEAD_DIM
RW_DECAY_RANK = 64
RW_ICLR_RANK = 64
RW_GATE_RANK = 128
RW_VRES_RANK = 32
RW_GN_EPS = 64e-5
RW_SPLITS = (RW_WIDTH, RW_WIDTH, RW_WIDTH, RW_DECAY_RANK, RW_ICLR_RANK, RW_GATE_RANK)
RW_COLS = sum(RW_SPLITS)
GDN_HEADS = 6
GDN_HEAD_DIM = 128
GDN_WIDTH = GDN_HEADS * GDN_HEAD_DIM
GDN_CONV = 4
GDN_CHUNK = 64
GDN_COLS = 4 * GDN_WIDTH + 2 * GDN_HEADS
SWA_Q_HEADS = 12
SWA_KV_HEADS = 4
SWA_HEAD_DIM = 64
SWA_WIDTH = SWA_Q_HEADS * SWA_HEAD_DIM
SWA_WINDOW = 128
SWA_BLOCK = SWA_WINDOW
SWA_COLS = (SWA_Q_HEADS + 2 * SWA_KV_HEADS) * SWA_HEAD_DIM
ROPE_DIM = SWA_HEAD_DIM // 4
ROPE_THETA = 500000.0
N_BRANCH = 3
MIX_W = RW_WIDTH + GDN_WIDTH + SWA_WIDTH
GATE_COLS = N_BRANCH * D_MODEL
IN_COLS = RW_COLS + GDN_COLS + SWA_COLS + GATE_COLS
D_FF = 5632
FFN_CONV = 3
NORM_EPS = 1e-6

kernel_name = "hybrid_rwkv7_gdn_swa_gated_trunk"

F32 = jnp.float32


def _split(t, sizes):
    return jnp.split(t, np.cumsum(sizes)[:-1].tolist(), axis=-1)


def rms_norm(x, gain, eps=NORM_EPS):
    xf = x.astype(F32)
    y = xf * lax.rsqrt(jnp.mean(xf * xf, axis=-1, keepdims=True) + eps)
    return (y * gain.astype(F32)).astype(x.dtype)


def _l2norm(t, eps=1e-6):
    return t * lax.rsqrt(jnp.sum(t * t, axis=-1, keepdims=True) + eps)


def _shift(t):
    return jnp.pad(t, ((0, 0), (1, 0), (0, 0)))[:, :-1]


def _lerp_shift(t, mu):
    return t + (_shift(t) - t) * mu


def causal_depthwise_conv(x, w):
    K = w.shape[0]
    return lax.conv_general_dilated(
        x, w[:, None, :].astype(x.dtype), window_strides=(1,), padding=[(K - 1, 0)],
        dimension_numbers=("NWC", "WIO", "NWC"), feature_group_count=x.shape[-1])


def _rwkv7_recurrence(r, decay, k, v, a_vec, b_vec):
    B, T, H, N = r.shape

    def step(S, inp):
        r_t, w_t, k_t, v_t, a_t, b_t = inp
        sa = jnp.einsum("bhvk,bhk->bhv", S, a_t)
        S = S * w_t[:, :, None, :] + sa[..., None] * b_t[:, :, None, :] + v_t[..., None] * k_t[:, :, None, :]
        return S, jnp.einsum("bhvk,bhk->bhv", S, r_t)

    xs = tuple(jnp.moveaxis(t, 1, 0) for t in (r, decay, k, v, a_vec, b_vec))
    _, y = lax.scan(step, jnp.zeros((B, H, N, N), F32), xs)
    return jnp.moveaxis(y, 0, 1)


def rwkv7_mixer(p, mu, w0, w_up, a0, a_up, g_up, k_k, k_a, r_k, ln_w, ln_b, v_first, vres):
    B, T, _ = p.shape
    H, N = RW_HEADS, RW_HEAD_DIM
    p = _lerp_shift(p, mu)
    r, k, v, wd, ad, gd = _split(p, RW_SPLITS)
    w = -jax.nn.softplus(-(w0 + jnp.tanh(wd) @ w_up).astype(F32)) - 0.5
    decay = jnp.exp(-jnp.exp(w))
    a = jax.nn.sigmoid(a0 + ad @ a_up)
    g = jax.nn.sigmoid(gd) @ g_up
    if vres is not None:
        vd, vres_mu, v0, v_up = vres
        vd = _lerp_shift(vd, vres_mu)
        v = v + (v_first - v) * jax.nn.sigmoid(v0 + vd @ v_up)
    heads = lambda t: t.reshape(B, T, H, N).astype(F32)
    kk = _l2norm(heads(k * k_k), 1e-12)
    k = k * (1 + (a - 1) * k_a)
    rh, kh, vh, ah = heads(r), heads(k), heads(v), heads(a)
    y = _rwkv7_recurrence(rh, heads(decay), kh, vh, -kk, kk * ah)
    mean = jnp.mean(y, axis=-1, keepdims=True)
    var = jnp.mean(jnp.square(y - mean), axis=-1, keepdims=True)
    y = ((y - mean) * lax.rsqrt(var + RW_GN_EPS)).reshape(B, T, RW_WIDTH)
    y = y * ln_w.astype(F32) + ln_b.astype(F32)
    bonus = jnp.sum(rh * kh * r_k.astype(F32), axis=-1, keepdims=True) * vh
    out = (y + bonus.reshape(B, T, RW_WIDTH)) * g.astype(F32)
    return out.astype(p.dtype), v


def chunk_gated_delta_rule(q, k, v, g, beta):
    B, T, H, dk = q.shape
    dv = v.shape[-1]
    C = GDN_CHUNK
    N = T // C
    chunks = lambda t: t.reshape(B, N, C, H, -1).transpose(0, 1, 3, 2, 4)
    q, k, v = chunks(q), chunks(k), chunks(v)
    g = jnp.cumsum(g.reshape(B, N, C, H).transpose(0, 1, 3, 2), axis=-1)
    beta = beta.reshape(B, N, C, H).transpose(0, 1, 3, 2)
    k_beta = k * beta[..., None]
    v_beta = v * beta[..., None]
    causal = jnp.tril(jnp.ones((C, C), bool))
    strict = jnp.tril(jnp.ones((C, C), bool), k=-1)
    decay = jnp.exp(jnp.where(causal, g[..., :, None] - g[..., None, :], -jnp.inf))
    L = jnp.where(strict, jnp.einsum("bnhid,bnhjd->bnhij", k_beta, k) * decay, 0.0)
    A = L + jnp.eye(C, dtype=F32)
    rhs = jnp.concatenate([v_beta, k_beta * jnp.exp(g)[..., None]], axis=-1)
    sol = lax.linalg.triangular_solve(A, rhs, left_side=True, lower=True, unit_diagonal=True)
    u, w = sol[..., :dv], sol[..., dv:]
    intra = jnp.einsum("bnhid,bnhjd->bnhij", q, k) * decay
    q_dec = q * jnp.exp(g)[..., None]
    k_tail = k * jnp.exp(g[..., -1:] - g)[..., None]
    g_last = jnp.exp(g[..., -1])

    def step(S, inp):
        u_n, w_n, a_n, qd_n, kt_n, gl_n = inp
        v_new = u_n - jnp.einsum("bhcd,bhde->bhce", w_n, S)
        o = jnp.einsum("bhcd,bhde->bhce", qd_n, S) + jnp.einsum("bhij,bhje->bhie", a_n, v_new)
        S = S * gl_n[..., None, None] + jnp.einsum("bhcd,bhce->bhde", kt_n, v_new)
        return S, o

    xs = tuple(jnp.moveaxis(t, 1, 0) for t in (u, w, intra, q_dec, k_tail, g_last))
    _, o = lax.scan(step, jnp.zeros((B, H, dk, dv), F32), xs)
    return o.transpose(1, 0, 3, 2, 4).reshape(B, T, H, dv)


def gated_deltanet_mixer(p, conv_w, a_log, dt_bias, o_norm):
    B, T, _ = p.shape
    H, Dh = GDN_HEADS, GDN_HEAD_DIM
    qkv, z, beta_logit, alpha_logit = _split(p, (3 * GDN_WIDTH, GDN_WIDTH, H, H))
    qkv = jax.nn.silu(causal_depthwise_conv(qkv, conv_w))
    q, k, v = (t.reshape(B, T, H, Dh).astype(F32) for t in jnp.split(qkv, 3, axis=-1))
    q = _l2norm(q) * (Dh ** -0.5)
    k = _l2norm(k)
    beta = jax.nn.sigmoid(beta_logit.astype(F32))
    g = -jnp.exp(a_log.astype(F32)) * jax.nn.softplus(alpha_logit.astype(F32) + dt_bias.astype(F32))
    o = chunk_gated_delta_rule(q, k, v, g, beta)
    o = o * lax.rsqrt(jnp.mean(o * o, axis=-1, keepdims=True) + NORM_EPS) * o_norm.astype(F32)
    o = o.reshape(B, T, GDN_WIDTH) * jax.nn.silu(z.astype(F32))
    return o.astype(p.dtype)


def partial_rope(x, positions):
    half = ROPE_DIM // 2
    inv_freq = ROPE_THETA ** (-jnp.arange(half, dtype=F32) * 2.0 / ROPE_DIM)
    ang = positions.astype(F32)[..., None] * inv_freq
    cos, sin = jnp.cos(ang)[:, :, None, :], jnp.sin(ang)[:, :, None, :]
    xr = x[..., :ROPE_DIM].astype(F32)
    x1, x2 = xr[..., :half], xr[..., half:]
    rot = jnp.concatenate([x1 * cos - x2 * sin, x2 * cos + x1 * sin], axis=-1)
    return jnp.concatenate([rot.astype(x.dtype), x[..., ROPE_DIM:]], axis=-1)


def _band(t, W):
    B, T, H, d = t.shape
    cur = t.reshape(B, T // W, W, H, d)
    prev = jnp.pad(cur, ((0, 0), (1, 0), (0, 0), (0, 0), (0, 0)))[:, :-1]
    return jnp.concatenate([prev, cur], axis=2)


def banded_window_attention(q, k, v, sinks):
    B, T, Hq, d = q.shape
    Hkv = k.shape[2]
    G = Hq // Hkv
    W = SWA_BLOCK
    NB = T // W
    qb = q.reshape(B, NB, W, Hkv, G, d)
    kb, vb = _band(k, W), _band(v, W)
    s = jnp.einsum("bnqhgd,bnkhd->bnhgqk", qb, kb, preferred_element_type=F32) * (d ** -0.5)
    qpos = jnp.arange(NB)[:, None] * W + jnp.arange(W)[None, :]
    kpos = jnp.arange(NB)[:, None] * W - W + jnp.arange(2 * W)[None, :]
    rel = qpos[:, :, None] - kpos[:, None, :]
    allowed = (rel >= 0) & (rel < SWA_WINDOW) & (kpos[:, None, :] >= 0)
    s = jnp.where(allowed[None, :, None, None], s, -jnp.inf)
    sink = jnp.broadcast_to(sinks.astype(F32).reshape(1, 1, Hkv, G, 1, 1), s.shape[:-1] + (1,))
    prob = jax.nn.softmax(jnp.concatenate([s, sink], axis=-1), axis=-1)[..., :-1]
    o = jnp.einsum("bnhgqk,bnkhd->bnqhgd", prob.astype(v.dtype), vb)
    return o.reshape(B, T, Hq, d)


def swa_mixer(p, positions, q_norm, k_norm, sinks):
    B, T, _ = p.shape
    hd = SWA_HEAD_DIM
    q, k, v = _split(p, (SWA_Q_HEADS * hd, SWA_KV_HEADS * hd, SWA_KV_HEADS * hd))
    q = partial_rope(rms_norm(q.reshape(B, T, SWA_Q_HEADS, hd), q_norm), positions)
    k = partial_rope(rms_norm(k.reshape(B, T, SWA_KV_HEADS, hd), k_norm), positions)
    v = v.reshape(B, T, SWA_KV_HEADS, hd)
    return banded_window_attention(q, k, v, sinks).reshape(B, T, SWA_WIDTH)


def conv_ffn(h, w_up, conv_w, w_down):
    u = causal_depthwise_conv(h @ w_up, conv_w)
    gate, up = jnp.split(u, 2, axis=-1)
    return (jax.nn.silu(gate) * up) @ w_down


def setup_inputs(seed: int = 0) -> dict:
    key = jax.random.key(seed)
    ks = iter(jax.random.split(key, 48))
    nrm = lambda shape, scale: jax.random.normal(next(ks), shape, F32) * scale
    uni = lambda shape, lo, hi: jax.random.uniform(next(ks), shape, F32, lo, hi)
    L, D = DEPTH, D_MODEL
    x = nrm((BATCH, SEQ, D), 1.0)
    positions = (jax.random.randint(next(ks), (BATCH, 1), 0, 1024, jnp.int32)
                 + jnp.arange(SEQ, dtype=jnp.int32)[None, :])
    dt = jnp.exp(uni((L, GDN_HEADS), math.log(1e-3), math.log(1e-1)))
    branch_scale = jnp.concatenate([jnp.full((RW_WIDTH,), RW_WIDTH ** -0.5, F32),
                                    jnp.full((GDN_WIDTH,), GDN_WIDTH ** -0.5, F32),
                                    jnp.full((SWA_WIDTH,), SWA_WIDTH ** -0.5, F32)])
    return {
        "x": x,
        "positions": positions,
        "norm_mix": 1.0 + nrm((L, D), 0.02),
        "w_in": nrm((L, D, IN_COLS), D ** -0.5),
        "rw_mu": uni((L, RW_COLS), 0.0, 1.0),
        "rw_w0": uni((L, RW_WIDTH), -5.0, 0.0),
        "rw_w_up": nrm((L, RW_DECAY_RANK, RW_WIDTH), RW_DECAY_RANK ** -0.5),
        "rw_a0": nrm((L, RW_WIDTH), 0.5),
        "rw_a_up": nrm((L, RW_ICLR_RANK, RW_WIDTH), RW_ICLR_RANK ** -0.5),
        "rw_g_up": nrm((L, RW_GATE_RANK, RW_WIDTH), RW_GATE_RANK ** -0.5),
        "rw_k_k": 0.85 + nrm((L, RW_WIDTH), 0.02),
        "rw_k_a": 1.0 + nrm((L, RW_WIDTH), 0.02),
        "rw_r_k": nrm((L, RW_HEADS, RW_HEAD_DIM), 0.1),
        "rw_ln_w": 1.0 + nrm((L, RW_WIDTH), 0.02),
        "rw_ln_b": nrm((L, RW_WIDTH), 0.02),
        "vres_down": nrm((L - 1, D, RW_VRES_RANK), D ** -0.5),
        "vres_mu": uni((L - 1, RW_VRES_RANK), 0.0, 1.0),
        "vres_v0": 1.0 + nrm((L - 1, RW_WIDTH), 0.1),
        "vres_v_up": nrm((L - 1, RW_VRES_RANK, RW_WIDTH), 0.5 * RW_VRES_RANK ** -0.5),
        "gdn_conv": nrm((L, GDN_CONV, 3 * GDN_WIDTH), GDN_CONV ** -0.5),
        "gdn_a_log": jnp.log(uni((L, GDN_HEADS), 1.0, 16.0)),
        "gdn_dt_bias": dt + jnp.log(-jnp.expm1(-dt)),
        "gdn_o_norm": 1.0 + nrm((L, GDN_HEAD_DIM), 0.02),
        "swa_q_norm": 1.0 + nrm((L, SWA_HEAD_DIM), 0.02),
        "swa_k_norm": 1.0 + nrm((L, SWA_HEAD_DIM), 0.02),
        "swa_sinks": nrm((L, SWA_Q_HEADS), 1.0),
        "w_branch": nrm((L, MIX_W, D), 1.0) * branch_scale[None, :, None],
        "w_out": nrm((L, D, D), D ** -0.5),
        "norm_ffn": 1.0 + nrm((L, D), 0.02),
        "ffn_up": nrm((L, D, 2 * D_FF), D ** -0.5),
        "ffn_conv": nrm((L, FFN_CONV, 2 * D_FF), FFN_CONV ** -0.5),
        "ffn_down": nrm((L, D_FF, D), D_FF ** -0.5),
    }


def reference(x, positions, norm_mix, w_in, rw_mu, rw_w0, rw_w_up, rw_a0, rw_a_up, rw_g_up,
              rw_k_k, rw_k_a, rw_r_k, rw_ln_w, rw_ln_b, vres_down, vres_mu, vres_v0, vres_v_up,
              gdn_conv, gdn_a_log, gdn_dt_bias, gdn_o_norm, swa_q_norm, swa_k_norm, swa_sinks,
              w_branch, w_out, norm_ffn, ffn_up, ffn_conv, ffn_down):
    v_first = None
    for l in range(DEPTH):
        h = rms_norm(x, norm_mix[l])
        if l == 0:
            proj = h @ w_in[l]
            p_rw, p_gdn, p_swa, p_gate = _split(proj, (RW_COLS, GDN_COLS, SWA_COLS, GATE_COLS))
            vres = None
        else:
            proj = h @ jnp.concatenate([w_in[l], vres_down[l - 1]], axis=1)
            p_rw, p_gdn, p_swa, p_gate, p_vres = _split(
                proj, (RW_COLS, GDN_COLS, SWA_COLS, GATE_COLS, RW_VRES_RANK))
            vres = (p_vres, vres_mu[l - 1], vres_v0[l - 1], vres_v_up[l - 1])
        o_rw, v_rw = rwkv7_mixer(p_rw, rw_mu[l], rw_w0[l], rw_w_up[l], rw_a0[l], rw_a_up[l],
                                 rw_g_up[l], rw_k_k[l], rw_k_a[l], rw_r_k[l], rw_ln_w[l],
                                 rw_ln_b[l], v_first, vres)
        if l == 0:
            v_first = v_rw
        o_gdn = gated_deltanet_mixer(p_gdn, gdn_conv[l], gdn_a_log[l], gdn_dt_bias[l], gdn_o_norm[l])
        o_swa = swa_mixer(p_swa, positions, swa_q_norm[l], swa_k_norm[l], swa_sinks[l])
        g_rw, g_gdn, g_swa = jnp.split(jax.nn.sigmoid(p_gate), N_BRANCH, axis=-1)
        wb = w_branch[l]
        merged = (g_rw * (o_rw @ wb[:RW_WIDTH])
                  + g_gdn * (o_gdn @ wb[RW_WIDTH:RW_WIDTH + GDN_WIDTH])
                  + g_swa * (o_swa @ wb[RW_WIDTH + GDN_WIDTH:]))
        x = x + (merged @ w_out[l]).astype(x.dtype)
        x = x + conv_ffn(rms_norm(x, norm_ffn[l]), ffn_up[l], ffn_conv[l], ffn_down[l]).astype(x.dtype)
    return x
```

```python
import functools

import jax
import jax.numpy as jnp
import numpy as np
from jax import lax
from jax.experimental import pallas as pl
from jax.experimental.pallas import tpu as pltpu

F32 = jnp.float32
BF16 = jnp.bfloat16

D_MODEL = 2048
RW_HEADS, RW_HEAD_DIM = 8, 64
RW_WIDTH = RW_HEADS * RW_HEAD_DIM
RW_DECAY_RANK, RW_ICLR_RANK, RW_GATE_RANK, RW_VRES_RANK = 64, 64, 128, 32
RW_GN_EPS = 64e-5
RW_COLS = 3 * RW_WIDTH + RW_DECAY_RANK + RW_ICLR_RANK + RW_GATE_RANK
GDN_HEADS, GDN_HEAD_DIM = 6, 128
GDN_WIDTH = GDN_HEADS * GDN_HEAD_DIM
GDN_CONV = 4
SWA_Q_HEADS, SWA_KV_HEADS, SWA_HEAD_DIM = 12, 4, 64
SWA_WIDTH = SWA_Q_HEADS * SWA_HEAD_DIM
SWA_KV_WIDTH = SWA_KV_HEADS * SWA_HEAD_DIM
SWA_WINDOW = 128
SWA_COLS = SWA_WIDTH + 2 * SWA_KV_WIDTH
ROPE_DIM = SWA_HEAD_DIM // 4
ROPE_THETA = 500000.0
D_FF = 5632
FFN_CONV = 3
NORM_EPS = 1e-6

LANES = 128
SUBLANES = 8
VMEM_LIMIT_BYTES = 56 * 1024 * 1024

CHUNK = 64
INV_BLOCK = 16
SLAB_VRES0 = 0
SLAB_BETA0 = RW_VRES_RANK
SLAB_ALPHA0 = RW_VRES_RANK + GDN_HEADS
RW_GROUP = RW_COLS + LANES

REC_PREC = lax.Precision.HIGHEST


def _cparams(sem):
    return pltpu.CompilerParams(dimension_semantics=sem, vmem_limit_bytes=VMEM_LIMIT_BYTES)


def _mm(a, b, prec=REC_PREC):
    return jnp.dot(a, b, precision=prec, preferred_element_type=F32)


def _mm_nt(a, b, prec=REC_PREC):
    return lax.dot_general(a, b, (((1,), (1,)), ((), ())), precision=prec, preferred_element_type=F32)


def _mm_tn(a, b, prec=REC_PREC):
    return lax.dot_general(a, b, (((0,), (0,)), ((), ())), precision=prec, preferred_element_type=F32)


def _split3(a):
    hi = a.astype(BF16)
    r1 = a - hi.astype(F32)
    mid = r1.astype(BF16)
    lo = (r1 - mid.astype(F32)).astype(BF16)
    return hi, mid, lo


def _mm_exact_rhs(a, b_bf16):
    hi, mid, lo = _split3(a)
    d = lambda t: jnp.dot(t, b_bf16, preferred_element_type=F32)
    return d(hi) + d(mid) + d(lo)


def _mm_exact_lhs(a_bf16, b):
    hi, mid, lo = _split3(b)
    d = lambda t: jnp.dot(a_bf16, t, preferred_element_type=F32)
    return d(hi) + d(mid) + d(lo)


def _mm_tn_exact_rhs(a, b_f32):
    hi, mid, lo = _split3(a)
    d = lambda t: _mm_tn(t.astype(F32), b_f32, prec=None)
    return d(hi) + d(mid) + d(lo)


def _iota(shape, dim):
    return lax.broadcasted_iota(jnp.int32, shape, dim)


def _neumann_inverse(lmat, n):
    assert CHUNK // INV_BLOCK == 4
    row, col = _iota((n, n), 0), _iota((n, n), 1)
    eye = (row == col).astype(F32)
    in_blk = (row // INV_BLOCK) == (col // INV_BLOCK)
    lb = jnp.where(in_blk, lmat, 0.0)
    e = lmat - lb
    inv = eye + lb
    p = _mm(lb, lb)
    steps = INV_BLOCK.bit_length() - 2
    for s in range(steps):
        inv = inv + _mm(inv, p)
        if s + 1 < steps:
            p = _mm(p, p)
    nmat = _mm(inv, e)
    n2 = _mm(nmat, nmat)
    m = eye + nmat + n2 + _mm(nmat, n2)
    return _mm(m, inv)


def _block_ones(width, blk):
    row, col = _iota((width, width), 0), _iota((width, width), 1)
    return ((row // blk) == (col // blk)).astype(BF16)


def _tri_ones(n, dtype, upper=False):
    row, col = _iota((n, n), 0), _iota((n, n), 1)
    return ((row <= col) if upper else (row >= col)).astype(dtype)


def _inproj_kernel(x_ref, g_ref, w_ref, o_ref, h_scr):
    @pl.when(pl.program_id(1) == 0)
    def _():
        x = x_ref[...]
        y = x * lax.rsqrt(jnp.mean(x * x, axis=-1, keepdims=True) + NORM_EPS) * g_ref[...]
        h_scr[...] = y.astype(BF16)

    o_ref[...] = jnp.dot(h_scr[...], w_ref[...], preferred_element_type=F32)


def _pick_tile(n, prefs):
    for t in prefs:
        if n % t == 0:
            return t
    return n


def _inproj(x2, gain, w_bf16, name):
    m, d = x2.shape
    n = w_bf16.shape[1]
    tm = _pick_tile(m, (1024, 512, 256, 128, 64, 32, 16, 8))
    tn = _pick_tile(n, (512, 640, 384, 256, 128))
    return pl.pallas_call(
        _inproj_kernel,
        out_shape=jax.ShapeDtypeStruct((m, n), F32),
        grid=(m // tm, n // tn),
        in_specs=[
            pl.BlockSpec((tm, d), lambda i, j: (i, 0)),
            pl.BlockSpec((1, d), lambda i, j: (0, 0)),
            pl.BlockSpec((d, tn), lambda i, j: (0, j)),
        ],
        out_specs=pl.BlockSpec((tm, tn), lambda i, j: (i, j)),
        scratch_shapes=[pltpu.VMEM((tm, d), BF16)],
        compiler_params=_cparams(("parallel", "arbitrary")),
        name=name,
    )(x2, gain.reshape(1, d), w_bf16)


def _rwkv_kernel(has_vres, tb, *refs):
    if has_vres:
        (p_ref, vf_ref, mu_ref, w0_ref, a0_ref, wwa_ref, gup_ref, kk_ref, ka_ref, rk_ref, lnw_ref, lnb_ref,
         v0_ref, vup_ref, o_ref, carry, state, pl_scr, r_scr, k_scr, v_scr, lw_scr, a_scr, b_scr, y_scr) = refs
    else:
        (p_ref, mu_ref, w0_ref, a0_ref, wwa_ref, gup_ref, kk_ref, ka_ref, rk_ref, lnw_ref, lnb_ref,
         o_ref, vout_ref, carry, state, pl_scr, r_scr, k_scr, v_scr, lw_scr, a_scr, b_scr, y_scr) = refs
    W = RW_WIDTH
    tstep = pl.program_id(1)

    @pl.when(tstep == 0)
    def _():
        carry[...] = jnp.zeros_like(carry)
        state[...] = jnp.zeros_like(state)

    row0 = _iota((tb, LANES), 0) == 0
    for c0 in range(0, RW_GROUP, LANES):
        x = p_ref[:, c0:c0 + LANES]
        prev = pltpu.roll(x, 1, 0)
        prev = jnp.where(row0, carry[SUBLANES - 1:SUBLANES, c0:c0 + LANES], prev)
        pl_scr[:, c0:c0 + LANES] = x + (prev - x) * mu_ref[:, c0:c0 + LANES]
    carry[...] = p_ref[tb - SUBLANES:tb, :]

    ones_h = _block_ones(W, RW_HEAD_DIM)

    c_wd = 3 * W
    x128 = pl_scr[:, c_wd:c_wd + LANES]
    lane = _iota((tb, LANES), 1)
    xin = jnp.where(lane < RW_DECAY_RANK, jnp.tanh(x128), x128)
    wa = jnp.dot(xin.astype(BF16), wwa_ref[...], preferred_element_type=F32)
    w_log = -jax.nn.softplus(-(w0_ref[...] + wa[:, :W])) - 0.5
    lw_scr[...] = -jnp.exp(w_log)
    asig = jax.nn.sigmoid(a0_ref[...] + wa[:, W:])
    gd = pl_scr[:, c_wd + LANES:c_wd + 2 * LANES]
    g = jnp.dot(jax.nn.sigmoid(gd).astype(BF16), gup_ref[...], preferred_element_type=F32)

    v = pl_scr[:, 2 * W:3 * W]
    if has_vres:
        vd = pl_scr[:, RW_COLS:RW_COLS + LANES]
        vl = jnp.dot(vd.astype(BF16), vup_ref[...], preferred_element_type=F32)
        v = v + (vf_ref[...] - v) * jax.nn.sigmoid(v0_ref[...] + vl)
    else:
        vout_ref[...] = v
    v_scr[...] = v

    k = pl_scr[:, W:2 * W]
    kkraw = k * kk_ref[...]
    ssq = _mm_exact_rhs(kkraw * kkraw, ones_h)
    kk = kkraw * lax.rsqrt(ssq + 1e-12)
    kfin = k * (1.0 + (asig - 1.0) * ka_ref[...])
    k_scr[...] = kfin
    a_scr[...] = -kk
    b_scr[...] = kk * asig
    r = pl_scr[:, 0:W]
    r_scr[...] = r
    bonus = _mm_exact_rhs(r * kfin * rk_ref[...], ones_h) * v

    C = CHUNK
    P2 = 2 * C
    tril_c = _tri_ones(C, BF16)
    row, col = _iota((P2, P2), 0), _iota((P2, P2), 1)
    same_head = (row // C) == (col // C)
    strict = same_head & ((row % C) > (col % C))
    incl = same_head & ((row % C) >= (col % C))
    left = _iota((C, LANES), 1) < RW_HEAD_DIM
    leftf = left.astype(F32)
    rightf = 1.0 - leftf

    def sel(x2):
        return jnp.where(left, x2[:C], x2[C:])

    def chunk_body(c, _):
        rows = pl.ds(pl.multiple_of(c * C, C), C)
        cum_all = _mm_exact_lhs(tril_c, lw_scr[rows, :])
        for pair in range(RW_HEADS // 2):
            ln = slice(pair * LANES, (pair + 1) * LANES)
            cum = cum_all[:, ln]
            lw = lw_scr[rows, ln]
            e_pos = jnp.exp(cum)
            e_neg = jnp.exp(-cum)
            rt = r_scr[rows, ln] * e_pos
            kt = k_scr[rows, ln] * e_neg
            bt = b_scr[rows, ln] * e_neg
            at = a_scr[rows, ln] * jnp.exp(cum - lw)
            vp = v_scr[rows, ln]
            lhs = jnp.concatenate([at * leftf, rt * leftf, at * rightf, rt * rightf], axis=0)
            out_a = _mm_nt(lhs, jnp.concatenate([bt, kt], axis=0))
            out_b = _mm_nt(lhs, jnp.concatenate([kt, bt], axis=0))
            ab = jnp.where(strict, jnp.concatenate([out_a[0:C], out_b[2 * C:3 * C]], axis=0), 0.0)
            ak = jnp.where(strict, jnp.concatenate([out_b[0:C], out_a[2 * C:3 * C]], axis=0), 0.0)
            rb = jnp.where(incl, jnp.concatenate([out_a[C:2 * C], out_b[3 * C:4 * C]], axis=0), 0.0)
            rk = jnp.where(incl, jnp.concatenate([out_b[C:2 * C], out_a[3 * C:4 * C]], axis=0), 0.0)
            tinv = _neumann_inverse(ab, P2)
            s = state[pair]
            rhs = _mm_nt(at, s) + sel(_mm(ak, jnp.concatenate([vp, vp], axis=0)))
            u = sel(_mm(tinv, jnp.concatenate([rhs, rhs], axis=0)))
            y = _mm_nt(rt, s) + sel(_mm(jnp.concatenate([rb, rk], axis=1),
                                        jnp.concatenate([u, u, vp, vp], axis=0)))
            pc = e_pos[C - 1:C, :]
            s_new = s * pc + _mm_tn(jnp.concatenate([u, vp], axis=0),
                                    jnp.concatenate([bt * pc, kt * pc], axis=0))
            state[pair] = jnp.where(same_head, s_new, 0.0)
            y_scr[rows, ln] = y
        return 0

    lax.fori_loop(0, tb // C, chunk_body, 0)

    y = y_scr[...]
    inv_n = 1.0 / RW_HEAD_DIM
    mean = _mm_exact_rhs(y, ones_h) * inv_n
    yc = y - mean
    var = _mm_exact_rhs(yc * yc, ones_h) * inv_n
    yn = yc * lax.rsqrt(var + RW_GN_EPS) * lnw_ref[...] + lnb_ref[...]
    o_ref[...] = (yn + bonus) * g


def _rwkv_mixer(p_rw, v_first, prm, bsz, seq, has_vres):
    m = p_rw.shape[0]
    tb = _pick_tile(seq, (256, 128, 64))
    nt = seq // tb
    W = RW_WIDTH
    row_spec = lambda width: pl.BlockSpec((tb, width), lambda b, t: (b * nt + t, 0))
    full = lambda a: pl.BlockSpec(a.shape, lambda b, t: (0,) * a.ndim)
    names = ["mu", "w0", "a0", "wwa", "gup", "kk", "ka", "rk", "lnw", "lnb"] + (["v0", "vup"] if has_vres else [])
    params = [prm[n] for n in names]
    ins = [p_rw] + ([v_first] if has_vres else []) + params
    in_specs = [row_spec(RW_GROUP)] + ([row_spec(W)] if has_vres else []) + [full(a) for a in params]
    if has_vres:
        out_shape = jax.ShapeDtypeStruct((m, W), F32)
        out_specs = row_spec(W)
    else:
        out_shape = (jax.ShapeDtypeStruct((m, W), F32), jax.ShapeDtypeStruct((m, W), F32))
        out_specs = (row_spec(W), row_spec(W))
    scratch = [
        pltpu.VMEM((SUBLANES, RW_GROUP), F32),
        pltpu.VMEM((RW_HEADS // 2, LANES, LANES), F32),
        pltpu.VMEM((tb, RW_GROUP), F32),
    ] + [pltpu.VMEM((tb, W), F32) for _ in range(7)]
    res = pl.pallas_call(
        functools.partial(_rwkv_kernel, has_vres, tb),
        out_shape=out_shape,
        grid=(bsz, nt),
        in_specs=in_specs,
        out_specs=out_specs,
        scratch_shapes=scratch,
        compiler_params=_cparams(("parallel", "arbitrary")),
        name="rwkv7_vres" if has_vres else "rwkv7",
    )(*ins)
    if has_vres:
        return res, None
    return res


def _gdn_kernel(tb, pg_ref, slab_ref, cw_ref, alog_ref, dtb_ref, onorm_ref, o_ref,
                carry, state, ext, q_scr, k_scr, v_scr, sig_scr, g_scr, o_scr):
    H, Dh, Wd = GDN_HEADS, GDN_HEAD_DIM, GDN_WIDTH
    QKV = 3 * Wd
    tstep = pl.program_id(1)

    @pl.when(tstep == 0)
    def _():
        carry[...] = jnp.zeros_like(carry)
        state[...] = jnp.zeros_like(state)

    ext[0:SUBLANES, :] = carry[...]
    ext[SUBLANES:, :] = pg_ref[:, 0:QKV]
    carry[...] = pg_ref[tb - SUBLANES:tb, 0:QKV]
    for j in range(QKV // Dh):
        ln = slice(j * Dh, (j + 1) * Dh)
        acc = jnp.zeros((tb, Dh), F32)
        for kk in range(GDN_CONV):
            off = SUBLANES - (GDN_CONV - 1) + kk
            acc = acc + ext[pl.ds(off, tb), ln] * cw_ref[kk:kk + 1, ln]
        act = acc * jax.nn.sigmoid(acc)
        which, h = divmod(j, H)
        hl = slice(h * Dh, (h + 1) * Dh)
        if which == 0:
            nrm = lax.rsqrt(jnp.sum(act * act, axis=-1, keepdims=True) + 1e-6)
            q_scr[:, hl] = act * nrm * (Dh ** -0.5)
        elif which == 1:
            nrm = lax.rsqrt(jnp.sum(act * act, axis=-1, keepdims=True) + 1e-6)
            k_scr[:, hl] = act * nrm
        else:
            v_scr[:, hl] = act

    slab = slab_ref[...]
    sig_scr[...] = jax.nn.sigmoid(slab)
    gsl = -jnp.exp(alog_ref[...]) * jax.nn.softplus(slab + dtb_ref[...])
    lane = _iota((tb, LANES), 1)
    gsl = jnp.where((lane >= SLAB_ALPHA0) & (lane < SLAB_ALPHA0 + H), gsl, 0.0)
    g_scr[...] = gsl

    C = CHUNK
    tril_c = _tri_ones(C, BF16)
    triu_c = _tri_ones(C, F32, upper=True)
    row, col = _iota((C, C), 0), _iota((C, C), 1)
    causal = row >= col
    strict = row > col

    def chunk_body(c, _):
        r0 = pl.multiple_of(c * C, C)
        rows = pl.ds(r0, C)
        g_chunk = g_scr[rows, :]
        gcol_all = _mm_exact_lhs(tril_c, g_chunk)
        grow_all = _mm_tn_exact_rhs(g_chunk, triu_c)
        sig = sig_scr[rows, :]
        for h in range(H):
            hl = slice(h * Dh, (h + 1) * Dh)
            gc = gcol_all[:, SLAB_ALPHA0 + h:SLAB_ALPHA0 + h + 1]
            gr = grow_all[SLAB_ALPHA0 + h:SLAB_ALPHA0 + h + 1, :]
            beta = sig[:, SLAB_BETA0 + h:SLAB_BETA0 + h + 1]
            dec = jnp.exp(jnp.where(causal, gc - gr, -jnp.inf))
            qh, kh, vh = q_scr[rows, hl], k_scr[rows, hl], v_scr[rows, hl]
            kb = kh * beta
            vb = vh * beta
            prod = _mm_nt(jnp.concatenate([kb, qh], axis=0), kh)
            lmat = jnp.where(strict, prod[:C] * dec, 0.0)
            intra = prod[C:] * dec
            tinv = _neumann_inverse(-lmat, C)
            eg = jnp.exp(gc)
            sol = _mm(tinv, jnp.concatenate([vb, kb * eg], axis=1))
            u, w = sol[:, :Dh], sol[:, Dh:]
            s = state[h]
            v_new = u - _mm(w, s)
            o = _mm(qh * eg, s) + _mm(intra, v_new)
            g_last = gc[C - 1:C, :]
            k_tail = kh * jnp.exp(g_last - gc)
            state[h] = s * jnp.exp(g_last) + _mm_tn(k_tail, v_new)
            o_scr[rows, hl] = o
        return 0

    lax.fori_loop(0, tb // C, chunk_body, 0)

    for h in range(H):
        hl = slice(h * Dh, (h + 1) * Dh)
        o = o_scr[:, hl]
        o = o * lax.rsqrt(jnp.mean(o * o, axis=-1, keepdims=True) + NORM_EPS) * onorm_ref[...]
        z = pg_ref[:, QKV + h * Dh:QKV + (h + 1) * Dh]
        o_ref[:, hl] = o * (z * jax.nn.sigmoid(z))


def _gdn_mixer(p_gdn, p_rw, prm, bsz, seq):
    m = p_gdn.shape[0]
    tb = _pick_tile(seq, (256, 128))
    nt = seq // tb
    Wd = GDN_WIDTH
    params = [prm["cw"], prm["alog"], prm["dtb"], prm["onorm"]]
    full = lambda a: pl.BlockSpec(a.shape, lambda b, t: (0,) * a.ndim)
    slab_blk = RW_COLS // LANES
    return pl.pallas_call(
        functools.partial(_gdn_kernel, tb),
        out_shape=jax.ShapeDtypeStruct((m, Wd), F32),
        grid=(bsz, nt),
        in_specs=[pl.BlockSpec((tb, 4 * Wd), lambda b, t: (b * nt + t, 0)),
                  pl.BlockSpec((tb, LANES), lambda b, t: (b * nt + t, slab_blk))] + [full(a) for a in params],
        out_specs=pl.BlockSpec((tb, Wd), lambda b, t: (b * nt + t, 0)),
        scratch_shapes=[
            pltpu.VMEM((SUBLANES, 3 * Wd), F32),
            pltpu.VMEM((GDN_HEADS, GDN_HEAD_DIM, GDN_HEAD_DIM), F32),
            pltpu.VMEM((tb + SUBLANES, 3 * Wd), F32),
            pltpu.VMEM((tb, Wd), F32), pltpu.VMEM((tb, Wd), F32), pltpu.VMEM((tb, Wd), F32),
            pltpu.VMEM((tb, LANES), F32), pltpu.VMEM((tb, LANES), F32),
            pltpu.VMEM((tb, Wd), F32),
        ],
        compiler_params=_cparams(("parallel", "arbitrary")),
        name="gdn",
    )(p_gdn, p_rw, *params)


def _swa_kernel(ps_ref, pos_ref, freq_ref, sgn_lo_ref, sgn_hi_ref, qn_ref, kn_ref, sink_ref, o_ref,
                kprev, vprev):
    Wn = SWA_WINDOW
    hd = SWA_HEAD_DIM
    G = SWA_Q_HEADS // SWA_KV_HEADS
    QW, KW = SWA_WIDTH, SWA_KV_WIDTH
    n = pl.program_id(1)

    @pl.when(n == 0)
    def _():
        kprev[...] = jnp.zeros_like(kprev)
        vprev[...] = jnp.zeros_like(vprev)

    pos = pos_ref[...]
    half = ROPE_DIM // 2

    def norm_rope(x, gain_row, width):
        ones_h = _block_ones(width, hd)
        ms = _mm_exact_rhs(x * x, ones_h) * (1.0 / hd)
        y = x * lax.rsqrt(ms + NORM_EPS) * gain_row
        ang = pos * freq_ref[:, :width]
        cs, sn = jnp.cos(ang), jnp.sin(ang)
        up = pltpu.roll(y, width - half, 1)
        dn = pltpu.roll(y, half, 1)
        return y * cs + up * (sn * sgn_lo_ref[:, :width]) + dn * (sn * sgn_hi_ref[:, :width])

    q = norm_rope(ps_ref[:, 0:QW], qn_ref[...], QW)
    k = norm_rope(ps_ref[:, QW:QW + KW], kn_ref[...], KW)
    v = ps_ref[:, QW + KW:QW + 2 * KW]

    qi = _iota((G * Wn, 2 * Wn), 0) % Wn
    kj = _iota((G * Wn, 2 * Wn), 1)
    rel = qi + Wn - kj
    allowed = (rel >= 0) & (rel < SWA_WINDOW) & ((kj >= Wn) | (n > 0))
    grp = _iota((G * Wn, 1), 0) // Wn
    sinks = sink_ref[...]

    for hk in range(SWA_KV_HEADS):
        kl = slice(hk * hd, (hk + 1) * hd)
        kband = jnp.concatenate([kprev[:, kl], k[:, kl]], axis=0)
        vband = jnp.concatenate([vprev[:, kl], v[:, kl]], axis=0)
        qs = jnp.concatenate([q[:, (hk * G + g) * hd:(hk * G + g + 1) * hd] for g in range(G)], axis=0)
        s = _mm_nt(qs, kband) * (hd ** -0.5)
        s = jnp.where(allowed, s, -jnp.inf)
        sink = jnp.zeros((G * Wn, 1), F32)
        for g in range(G):
            sg = sinks[:, hk * G + g:hk * G + g + 1]
            sink = jnp.where(grp == g, sg, sink)
        mx = jnp.maximum(jnp.max(s, axis=-1, keepdims=True), sink)
        p = jnp.exp(s - mx)
        den = jnp.sum(p, axis=-1, keepdims=True) + jnp.exp(sink - mx)
        o = _mm(p / den, vband)
        for g in range(G):
            o_ref[:, (hk * G + g) * hd:(hk * G + g + 1) * hd] = o[g * Wn:(g + 1) * Wn]

    kprev[...] = k
    vprev[...] = v


def _swa_mixer(p_swa, pos_col, prm, bsz, seq):
    m = p_swa.shape[0]
    Wn = SWA_WINDOW
    nb = seq // Wn
    params = [prm["freq"], prm["sgn_lo"], prm["sgn_hi"], prm["qn"], prm["kn"], prm["sinks"]]
    full = lambda a: pl.BlockSpec(a.shape, lambda b, t: (0,) * a.ndim)
    return pl.pallas_call(
        _swa_kernel,
        out_shape=jax.ShapeDtypeStruct((m, SWA_WIDTH), F32),
        grid=(bsz, nb),
        in_specs=[pl.BlockSpec((Wn, SWA_COLS), lambda b, t: (b * nb + t, 0)),
                  pl.BlockSpec((Wn, 1), lambda b, t: (b * nb + t, 0))] + [full(a) for a in params],
        out_specs=pl.BlockSpec((Wn, SWA_WIDTH), lambda b, t: (b * nb + t, 0)),
        scratch_shapes=[pltpu.VMEM((Wn, SWA_KV_WIDTH), F32), pltpu.VMEM((Wn, SWA_KV_WIDTH), F32)],
        compiler_params=_cparams(("parallel", "arbitrary")),
        name="swa",
    )(p_swa, pos_col, *params)


def _merge_kernel(x_ref, orw_ref, ogdn_ref, oswa_ref, g0_ref, g1_ref, g2_ref, wb0_ref, wb1_ref, wb2_ref,
                  wout_ref, o_ref, acc):
    j = pl.program_id(1)

    @pl.when(j == 0)
    def _():
        acc[...] = jnp.zeros_like(acc)

    d = lambda a_ref, w_ref: jnp.dot(a_ref[...].astype(BF16), w_ref[...], preferred_element_type=F32)
    merged = (jax.nn.sigmoid(g0_ref[...]) * d(orw_ref, wb0_ref)
              + jax.nn.sigmoid(g1_ref[...]) * d(ogdn_ref, wb1_ref)
              + jax.nn.sigmoid(g2_ref[...]) * d(oswa_ref, wb2_ref))
    acc[...] += jnp.dot(merged.astype(BF16), wout_ref[...], preferred_element_type=F32)

    @pl.when(j == pl.num_programs(1) - 1)
    def _():
        o_ref[...] = x_ref[...] + acc[...]


def _merge(x2, o_rw, o_gdn, o_swa, p_gate, wb_rw, wb_gdn, wb_swa, w_out):
    m, d = x2.shape
    tm = _pick_tile(m, (512, 256, 128, 64, 32, 16, 8))
    tn = 512
    nj = d // tn
    rows = lambda width: pl.BlockSpec((tm, width), lambda i, j: (i, 0))
    gate = lambda b: pl.BlockSpec((tm, tn), lambda i, j, b=b: (i, b * nj + j))
    wcol = lambda kdim: pl.BlockSpec((kdim, tn), lambda i, j: (0, j))
    return pl.pallas_call(
        _merge_kernel,
        out_shape=jax.ShapeDtypeStruct((m, d), F32),
        grid=(m // tm, nj),
        in_specs=[rows(d), rows(RW_WIDTH), rows(GDN_WIDTH), rows(SWA_WIDTH), gate(0), gate(1), gate(2),
                  wcol(RW_WIDTH), wcol(GDN_WIDTH), wcol(SWA_WIDTH),
                  pl.BlockSpec((tn, d), lambda i, j: (j, 0))],
        out_specs=rows(d),
        scratch_shapes=[pltpu.VMEM((tm, d), F32)],
        compiler_params=_cparams(("parallel", "arbitrary")),
        name="merge",
    )(x2, o_rw, o_gdn, o_swa, p_gate, p_gate, p_gate, wb_rw, wb_gdn, wb_swa, w_out)


def _ffn_kernel(tm, seq, x_ref, halo_ref, gain_ref, wg_ref, wu_ref, cg_ref, cu_ref, wd_ref, o_ref,
                h_scr, u_scr, acc):
    i = pl.program_id(0)
    j = pl.program_id(1)
    tn = wg_ref.shape[1]

    @pl.when(j == 0)
    def _():
        def norm(x):
            return (x * lax.rsqrt(jnp.mean(x * x, axis=-1, keepdims=True) + NORM_EPS) * gain_ref[...]).astype(BF16)
        seq_start = (i * tm) % seq == 0
        halo = jnp.where(seq_start, 0.0, halo_ref[...])
        h_scr[0:2 * SUBLANES, :] = norm(jnp.concatenate([jnp.zeros_like(halo), halo], axis=0))
        h_scr[2 * SUBLANES:, :] = norm(x_ref[...])
        acc[...] = jnp.zeros_like(acc)

    h = h_scr[...]
    u_scr[:, 0:tn] = jnp.dot(h, wg_ref[...], preferred_element_type=F32)
    u_scr[:, tn:2 * tn] = jnp.dot(h, wu_ref[...], preferred_element_type=F32)

    def conv(c0, cw_ref):
        out = jnp.zeros((tm, tn), F32)
        for kk in range(FFN_CONV):
            off = 2 * SUBLANES - (FFN_CONV - 1) + kk
            out = out + u_scr[pl.ds(off, tm), c0:c0 + tn] * cw_ref[kk:kk + 1, :]
        return out

    gate = conv(0, cg_ref)
    up = conv(tn, cu_ref)
    act = gate * jax.nn.sigmoid(gate) * up
    acc[...] += jnp.dot(act.astype(BF16), wd_ref[...], preferred_element_type=F32)

    @pl.when(j == pl.num_programs(1) - 1)
    def _():
        o_ref[...] = x_ref[...] + acc[...]


def _conv_ffn(x2, gain, w_gate, w_up, c_gate, c_up, w_down, seq):
    m, d = x2.shape
    dff = w_gate.shape[1]
    tm = _pick_tile(seq, (512, 256, 128, 64, 32, 16))
    tn = _pick_tile(dff, (512, 256, 128))
    nj = dff // tn
    hb = tm // SUBLANES
    return pl.pallas_call(
        functools.partial(_ffn_kernel, tm, seq),
        out_shape=jax.ShapeDtypeStruct((m, d), F32),
        grid=(m // tm, nj),
        in_specs=[
            pl.BlockSpec((tm, d), lambda i, j: (i, 0)),
            pl.BlockSpec((SUBLANES, d), lambda i, j: (jnp.maximum(i * hb - 1, 0), 0)),
            pl.BlockSpec((1, d), lambda i, j: (0, 0)),
            pl.BlockSpec((d, tn), lambda i, j: (0, j)),
            pl.BlockSpec((d, tn), lambda i, j: (0, j)),
            pl.BlockSpec((FFN_CONV, tn), lambda i, j: (0, j)),
            pl.BlockSpec((FFN_CONV, tn), lambda i, j: (0, j)),
            pl.BlockSpec((tn, d), lambda i, j: (j, 0)),
        ],
        out_specs=pl.BlockSpec((tm, d), lambda i, j: (i, 0)),
        scratch_shapes=[pltpu.VMEM((tm + 2 * SUBLANES, d), BF16),
                        pltpu.VMEM((tm + 2 * SUBLANES, 2 * tn), F32),
                        pltpu.VMEM((tm, d), F32)],
        compiler_params=_cparams(("parallel", "arbitrary")),
        name="conv_ffn",
    )(x2, x2, gain.reshape(1, d), w_gate, w_up, c_gate, c_up, w_down)


def _row(a):
    return a.reshape(1, -1).astype(F32)


def _pad_cols(a, width):
    return jnp.pad(a, ((0, 0), (0, width - a.shape[1])))


def _layer_params(l, w_in, rw_mu, rw_w0, rw_w_up, rw_a0, rw_a_up, rw_g_up, rw_k_k, rw_k_a, rw_r_k, rw_ln_w,
                  rw_ln_b, vres_down, vres_mu, vres_v0, vres_v_up, gdn_conv, gdn_a_log, gdn_dt_bias, gdn_o_norm,
                  swa_q_norm, swa_k_norm, swa_sinks):
    d = w_in.shape[1]
    wl = w_in[l]
    c_gdn = RW_COLS
    c_ba = c_gdn + 4 * GDN_WIDTH
    c_swa = c_ba + 2 * GDN_HEADS
    c_gate = c_swa + SWA_COLS
    has_vres = l > 0
    vres_w = vres_down[l - 1] if has_vres else jnp.zeros((d, RW_VRES_RANK), F32)
    slab_w = _pad_cols(jnp.concatenate([vres_w, wl[:, c_ba:c_swa]], axis=1), LANES)
    w_rw = jnp.concatenate([wl[:, :RW_COLS], slab_w], axis=1).astype(BF16)
    w_gdn = wl[:, c_gdn:c_ba].astype(BF16)
    w_swa = wl[:, c_swa:c_gate].astype(BF16)
    w_gate = wl[:, c_gate:].astype(BF16)

    W = RW_WIDTH
    mu = jnp.concatenate([rw_mu[l], vres_mu[l - 1] if has_vres else jnp.zeros((RW_VRES_RANK,), F32),
                          jnp.zeros((LANES - RW_VRES_RANK,), F32)])
    wwa = jnp.zeros((LANES, 2 * W), F32)
    wwa = wwa.at[:RW_DECAY_RANK, :W].set(rw_w_up[l]).at[RW_DECAY_RANK:, W:].set(rw_a_up[l])
    rw = dict(mu=_row(mu), w0=_row(rw_w0[l]), a0=_row(rw_a0[l]), wwa=wwa.astype(BF16),
              gup=rw_g_up[l].astype(BF16), kk=_row(rw_k_k[l]), ka=_row(rw_k_a[l]), rk=_row(rw_r_k[l]),
              lnw=_row(rw_ln_w[l]), lnb=_row(rw_ln_b[l]))
    if has_vres:
        vup = jnp.zeros((LANES, W), F32).at[:RW_VRES_RANK].set(vres_v_up[l - 1])
        rw.update(v0=_row(vres_v0[l - 1]), vup=vup.astype(BF16))

    lanes_pad = lambda a, off: jnp.zeros((1, LANES), F32).at[0, off:off + a.shape[0]].set(a)
    gdn = dict(cw=gdn_conv[l].astype(F32), alog=lanes_pad(gdn_a_log[l], SLAB_ALPHA0),
               dtb=lanes_pad(gdn_dt_bias[l], SLAB_ALPHA0), onorm=_row(gdn_o_norm[l]))

    half = ROPE_DIM // 2
    lane = np.arange(SWA_WIDTH) % SWA_HEAD_DIM
    inv_freq = ROPE_THETA ** (-(lane % half).astype(np.float32) * 2.0 / ROPE_DIM)
    freq = np.where(lane < ROPE_DIM, inv_freq, 0.0).astype(np.float32)
    sgn_lo = np.where(lane < half, -1.0, 0.0).astype(np.float32)
    sgn_hi = np.where((lane >= half) & (lane < ROPE_DIM), 1.0, 0.0).astype(np.float32)
    swa = dict(freq=jnp.asarray(freq).reshape(1, -1), sgn_lo=jnp.asarray(sgn_lo).reshape(1, -1),
               sgn_hi=jnp.asarray(sgn_hi).reshape(1, -1),
               qn=_row(jnp.tile(swa_q_norm[l], SWA_Q_HEADS)), kn=_row(jnp.tile(swa_k_norm[l], SWA_KV_HEADS)),
               sinks=lanes_pad(swa_sinks[l], 0))
    return dict(w_rw=w_rw, w_gdn=w_gdn, w_swa=w_swa, w_gate=w_gate, rw=rw, gdn=gdn, swa=swa, has_vres=has_vres)


def kernel(x, positions, norm_mix, w_in, rw_mu, rw_w0, rw_w_up, rw_a0, rw_a_up, rw_g_up, rw_k_k, rw_k_a, rw_r_k, rw_ln_w, rw_ln_b, vres_down, vres_mu, vres_v0, vres_v_up, gdn_conv, gdn_a_log, gdn_dt_bias, gdn_o_norm, swa_q_norm, swa_k_norm, swa_sinks, w_branch, w_out, norm_ffn, ffn_up, ffn_conv, ffn_down):
    bsz, seq, d = x.shape
    depth = w_in.shape[0]
    x2 = x.reshape(bsz * seq, d)
    pos_col = positions.astype(F32).reshape(bsz * seq, 1)
    dff = ffn_down.shape[1]
    v_first = None
    for l in range(depth):
        lp = _layer_params(l, w_in, rw_mu, rw_w0, rw_w_up, rw_a0, rw_a_up, rw_g_up, rw_k_k, rw_k_a, rw_r_k,
                           rw_ln_w, rw_ln_b, vres_down, vres_mu, vres_v0, vres_v_up, gdn_conv, gdn_a_log,
                           gdn_dt_bias, gdn_o_norm, swa_q_norm, swa_k_norm, swa_sinks)
        p_rw = _inproj(x2, norm_mix[l], lp["w_rw"], "inproj_rw")
        p_gdn = _inproj(x2, norm_mix[l], lp["w_gdn"], "inproj_gdn")
        p_swa = _inproj(x2, norm_mix[l], lp["w_swa"], "inproj_swa")
        p_gate = _inproj(x2, norm_mix[l], lp["w_gate"], "inproj_gate")
        if lp["has_vres"]:
            o_rw, _ = _rwkv_mixer(p_rw, v_first, lp["rw"], bsz, seq, True)
        else:
            o_rw, v_first = _rwkv_mixer(p_rw, None, lp["rw"], bsz, seq, False)
        o_gdn = _gdn_mixer(p_gdn, p_rw, lp["gdn"], bsz, seq)
        o_swa = _swa_mixer(p_swa, pos_col, lp["swa"], bsz, seq)
        wb = w_branch[l].astype(BF16)
        x2 = _merge(x2, o_rw, o_gdn, o_swa, p_gate, wb[:RW_WIDTH], wb[RW_WIDTH:RW_WIDTH + GDN_WIDTH],
                    wb[RW_WIDTH + GDN_WIDTH:], w_out[l].astype(BF16))
        up = ffn_up[l].astype(BF16)
        x2 = _conv_ffn(x2, norm_ffn[l], up[:, :dff], up[:, dff:], ffn_conv[l][:, :dff], ffn_conv[l][:, dff:],
                       ffn_down[l].astype(BF16), seq)
    return x2.reshape(bsz, seq, d)
```

```python
import functools

import jax
import jax.numpy as jnp
import numpy as np
from jax import lax
from jax.experimental import pallas as pl
from jax.experimental.pallas import tpu as pltpu

F32 = jnp.float32
BF16 = jnp.bfloat16

D_MODEL = 2048
RW_HEADS, RW_HEAD_DIM = 8, 64
RW_WIDTH = RW_HEADS * RW_HEAD_DIM
RW_DECAY_RANK, RW_ICLR_RANK, RW_GATE_RANK, RW_VRES_RANK = 64, 64, 128, 32
RW_GN_EPS = 64e-5
RW_COLS = 3 * RW_WIDTH + RW_DECAY_RANK + RW_ICLR_RANK + RW_GATE_RANK
GDN_HEADS, GDN_HEAD_DIM = 6, 128
GDN_WIDTH = GDN_HEADS * GDN_HEAD_DIM
GDN_CONV = 4
SWA_Q_HEADS, SWA_KV_HEADS, SWA_HEAD_DIM = 12, 4, 64
SWA_WIDTH = SWA_Q_HEADS * SWA_HEAD_DIM
SWA_KV_WIDTH = SWA_KV_HEADS * SWA_HEAD_DIM
SWA_WINDOW = 128
SWA_COLS = SWA_WIDTH + 2 * SWA_KV_WIDTH
ROPE_DIM = SWA_HEAD_DIM // 4
ROPE_THETA = 500000.0
D_FF = 5632
FFN_CONV = 3
NORM_EPS = 1e-6

LANES = 128
SUBLANES = 8
VMEM_LIMIT_BYTES = 56 * 1024 * 1024

CHUNK = 64
INV_BLOCK = 16
SLAB_VRES0 = 0
SLAB_BETA0 = RW_VRES_RANK
SLAB_ALPHA0 = RW_VRES_RANK + GDN_HEADS
RW_GROUP = RW_COLS + LANES

def _cparams(sem):
    return pltpu.CompilerParams(dimension_semantics=sem, vmem_limit_bytes=VMEM_LIMIT_BYTES)


def _dg(a, b, dims):
    return lax.dot_general(a.astype(BF16), b.astype(BF16), dims, preferred_element_type=F32)


def _mm(a, b):
    return _dg(a, b, (((1,), (0,)), ((), ())))


def _mm_nt(a, b):
    return _dg(a, b, (((1,), (1,)), ((), ())))


def _mm_tn(a, b):
    return _dg(a, b, (((0,), (0,)), ((), ())))


def _bmm(a, b):
    return _dg(a, b, (((2,), (1,)), ((0,), (0,))))


def _bmm_nt(a, b):
    return _dg(a, b, (((2,), (2,)), ((0,), (0,))))


def _bmm_tn(a, b):
    return _dg(a, b, (((1,), (1,)), ((0,), (0,))))


def _split3(a):
    hi = a.astype(BF16)
    r1 = a - hi.astype(F32)
    mid = r1.astype(BF16)
    lo = (r1 - mid.astype(F32)).astype(BF16)
    return hi, mid, lo


def _mm_exact_rhs(a, b_bf16):
    hi, mid, lo = _split3(a)
    d = lambda t: jnp.dot(t, b_bf16, preferred_element_type=F32)
    return d(hi) + d(mid) + d(lo)


def _mm_exact_lhs(a_bf16, b):
    hi, mid, lo = _split3(b)
    d = lambda t: jnp.dot(a_bf16, t, preferred_element_type=F32)
    return d(hi) + d(mid) + d(lo)


def _mm_tn_exact_rhs(a, b_bf16):
    hi, mid, lo = _split3(a)
    d = lambda t: _mm_tn(t, b_bf16)
    return d(hi) + d(mid) + d(lo)


def _iota(shape, dim):
    return lax.broadcasted_iota(jnp.int32, shape, dim)


def _neumann_inverse(lmat):
    assert CHUNK // INV_BLOCK == 4
    n = lmat.shape[-1]
    row, col = _iota((1, n, n), 1), _iota((1, n, n), 2)
    eye = (row == col).astype(F32)
    in_blk = (row // INV_BLOCK) == (col // INV_BLOCK)
    lb = jnp.where(in_blk, lmat, 0.0)
    e = lmat - lb
    inv = eye + lb
    p = _bmm(lb, lb)
    steps = INV_BLOCK.bit_length() - 2
    for s in range(steps):
        inv = inv + _bmm(inv, p)
        if s + 1 < steps:
            p = _bmm(p, p)
    nmat = _bmm(inv, e)
    n2 = _bmm(nmat, nmat)
    m = eye + nmat + n2 + _bmm(nmat, n2)
    return _bmm(m, inv)


def _block_ones(width, blk):
    row, col = _iota((width, width), 0), _iota((width, width), 1)
    return ((row // blk) == (col // blk)).astype(BF16)


def _tri_ones(n, dtype, upper=False):
    row, col = _iota((n, n), 0), _iota((n, n), 1)
    return ((row <= col) if upper else (row >= col)).astype(dtype)


def _pack_kernel(shift, *refs):
    if shift == 0:
        a_ref, o_ref = refs
        o_ref[...] = a_ref[...].astype(BF16)
    else:
        a_ref, b_ref, o_ref = refs
        lane = _iota(a_ref.shape, 1)
        moved_a = pltpu.roll(a_ref[...], LANES - shift, 1)
        moved_b = pltpu.roll(b_ref[...], LANES - shift, 1)
        o_ref[...] = jnp.where(lane < LANES - shift, moved_a, moved_b).astype(BF16)


def _pack_cols(w3, layer, start, width):
    _, d, n = w3.shape
    q, shift = divmod(start, LANES)
    last_blk = pl.cdiv(n, LANES) - 1
    src = lambda off: pl.BlockSpec((None, d, LANES), lambda t: (layer, 0, jnp.minimum(q + t + off, last_blk)))
    ins = [w3] if shift == 0 else [w3, w3]
    return pl.pallas_call(
        functools.partial(_pack_kernel, shift),
        out_shape=jax.ShapeDtypeStruct((d, width), BF16),
        grid=(width // LANES,),
        in_specs=[src(0)] if shift == 0 else [src(0), src(1)],
        out_specs=pl.BlockSpec((d, LANES), lambda t: (0, t)),
        compiler_params=_cparams(("arbitrary",)),
        name="pack_cols",
    )(*ins)


def _inproj_kernel(x_ref, g_ref, w_ref, o_ref, h_scr):
    @pl.when(pl.program_id(1) == 0)
    def _():
        x = x_ref[...]
        y = x * lax.rsqrt(jnp.mean(x * x, axis=-1, keepdims=True) + NORM_EPS) * g_ref[...]
        h_scr[...] = y.astype(BF16)

    o_ref[...] = jnp.dot(h_scr[...], w_ref[...], preferred_element_type=F32)


def _pick_tile(n, prefs):
    for t in prefs:
        if n % t == 0:
            return t
    return n


def _inproj(x2, gain, w_bf16, name):
    m, d = x2.shape
    n = w_bf16.shape[1]
    tm = _pick_tile(m, (1024, 512, 256, 128, 64, 32, 16, 8))
    tn = _pick_tile(n, (512, 640, 384, 256, 128))
    return pl.pallas_call(
        _inproj_kernel,
        out_shape=jax.ShapeDtypeStruct((m, n), F32),
        grid=(m // tm, n // tn),
        in_specs=[
            pl.BlockSpec((tm, d), lambda i, j: (i, 0)),
            pl.BlockSpec((1, d), lambda i, j: (0, 0)),
            pl.BlockSpec((d, tn), lambda i, j: (0, j)),
        ],
        out_specs=pl.BlockSpec((tm, tn), lambda i, j: (i, j)),
        scratch_shapes=[pltpu.VMEM((tm, d), BF16)],
        compiler_params=_cparams(("parallel", "arbitrary")),
        name=name,
    )(x2, gain.reshape(1, d), w_bf16)


def _rwkv_kernel(has_vres, tb, *refs):
    scr = refs[-14:]
    (carry, state, pl_scr, at_scr, rt_scr, bt_scr, kt_scr, v_scr, y_scr,
     t_scr, rb_scr, akv_scr, rkv_scr, btk_scr) = scr
    if has_vres:
        (p_ref, vf_ref, mu_ref, w0_ref, a0_ref, wwa_ref, gup_ref, kk_ref, ka_ref, rk_ref, lnw_ref, lnb_ref,
         v0_ref, vup_ref, o_ref) = refs[:-14]
    else:
        (p_ref, mu_ref, w0_ref, a0_ref, wwa_ref, gup_ref, kk_ref, ka_ref, rk_ref, lnw_ref, lnb_ref,
         o_ref, vout_ref) = refs[:-14]
    W = RW_WIDTH
    tstep = pl.program_id(1)

    @pl.when(tstep == 0)
    def _():
        carry[...] = jnp.zeros_like(carry)
        state[...] = jnp.zeros_like(state)

    row0 = _iota((tb, LANES), 0) == 0
    for c0 in range(0, RW_GROUP, LANES):
        x = p_ref[:, c0:c0 + LANES]
        prev = pltpu.roll(x, 1, 0)
        prev = jnp.where(row0, carry[SUBLANES - 1:SUBLANES, c0:c0 + LANES], prev)
        pl_scr[:, c0:c0 + LANES] = x + (prev - x) * mu_ref[:, c0:c0 + LANES]
    carry[...] = p_ref[tb - SUBLANES:tb, :]

    ones_h = _block_ones(W, RW_HEAD_DIM)

    c_wd = 3 * W
    x128 = pl_scr[:, c_wd:c_wd + LANES]
    lane = _iota((tb, LANES), 1)
    xin = jnp.where(lane < RW_DECAY_RANK, jnp.tanh(x128), x128)
    wa = jnp.dot(xin.astype(BF16), wwa_ref[...], preferred_element_type=F32)
    w_log = -jax.nn.softplus(-(w0_ref[...] + wa[:, :W])) - 0.5
    lw = -jnp.exp(w_log)
    asig = jax.nn.sigmoid(a0_ref[...] + wa[:, W:])
    gd = pl_scr[:, c_wd + LANES:c_wd + 2 * LANES]
    g = jnp.dot(jax.nn.sigmoid(gd).astype(BF16), gup_ref[...], preferred_element_type=F32)

    v = pl_scr[:, 2 * W:3 * W]
    if has_vres:
        vd = pl_scr[:, RW_COLS:RW_COLS + LANES]
        vl = jnp.dot(vd.astype(BF16), vup_ref[...], preferred_element_type=F32)
        v = v + (vf_ref[...] - v) * jax.nn.sigmoid(v0_ref[...] + vl)
    else:
        vout_ref[...] = v
    v_scr[...] = v

    k = pl_scr[:, W:2 * W]
    kkraw = k * kk_ref[...]
    ssq = _mm_exact_rhs(kkraw * kkraw, ones_h)
    kk = kkraw * lax.rsqrt(ssq + 1e-12)
    kfin = k * (1.0 + (asig - 1.0) * ka_ref[...])
    r = pl_scr[:, 0:W]
    bonus = _mm_exact_rhs(r * kfin * rk_ref[...], ones_h) * v

    C = CHUNK
    P2 = 2 * C
    NC, NP = tb // C, RW_HEADS // 2
    trow, tcol = _iota((tb, tb), 0), _iota((tb, tb), 1)
    chunk_tril = (((trow // C) == (tcol // C)) & (trow >= tcol)).astype(BF16)
    cum = _mm_exact_lhs(chunk_tril, lw)
    e_pos = jnp.exp(cum)
    e_neg = jnp.exp(-cum)
    rt_scr[...] = r * e_pos
    kt_scr[...] = kfin * e_neg
    bt_scr[...] = (kk * asig) * e_neg
    at_scr[...] = -kk * jnp.exp(cum - lw)

    def to_b(ref):
        return jnp.concatenate([ref[c * C:(c + 1) * C, p * LANES:(p + 1) * LANES][None]
                                for c in range(NC) for p in range(NP)], axis=0)

    row, col = _iota((1, P2, P2), 1), _iota((1, P2, P2), 2)
    same_head = (row // C) == (col // C)
    strict = same_head & ((row % C) > (col % C))
    incl = same_head & ((row % C) >= (col % C))
    left = _iota((1, C, LANES), 2) < RW_HEAD_DIM
    leftf = left.astype(F32)
    rightf = 1.0 - leftf

    def sel(x3):
        return jnp.where(left, x3[:, :C], x3[:, C:])

    at_b, rt_b, bt_b, kt_b, vp_b = to_b(at_scr), to_b(rt_scr), to_b(bt_scr), to_b(kt_scr), to_b(v_scr)
    lhs = jnp.concatenate([at_b * leftf, rt_b * leftf, at_b * rightf, rt_b * rightf], axis=1)
    out_a = _bmm_nt(lhs, jnp.concatenate([bt_b, kt_b], axis=1))
    out_b = _bmm_nt(lhs, jnp.concatenate([kt_b, bt_b], axis=1))
    ab = jnp.where(strict, jnp.concatenate([out_a[:, 0:C], out_b[:, 2 * C:3 * C]], axis=1), 0.0)
    ak = jnp.where(strict, jnp.concatenate([out_b[:, 0:C], out_a[:, 2 * C:3 * C]], axis=1), 0.0)
    rb = jnp.where(incl, jnp.concatenate([out_a[:, C:2 * C], out_b[:, 3 * C:4 * C]], axis=1), 0.0)
    rk = jnp.where(incl, jnp.concatenate([out_b[:, C:2 * C], out_a[:, 3 * C:4 * C]], axis=1), 0.0)
    t_scr[...] = _neumann_inverse(ab)
    rb_scr[...] = rb
    vv = jnp.concatenate([vp_b, vp_b], axis=1)
    akv_scr[...] = sel(_bmm(ak, vv))
    rkv_scr[...] = sel(_bmm(rk, vv))
    pc_rows = jnp.concatenate([e_pos[(c + 1) * C - 1:(c + 1) * C, p * LANES:(p + 1) * LANES][None]
                               for c in range(NC) for p in range(NP)], axis=0)
    btk_scr[...] = jnp.concatenate([bt_b * pc_rows, kt_b * pc_rows], axis=1)

    for c in range(NC):
        items = slice(c * NP, (c + 1) * NP)
        s = state[...]
        a_c, r_c, v_c = at_b[items], rt_b[items], vp_b[items]
        ars = _bmm_nt(jnp.concatenate([a_c, r_c], axis=1), s)
        rhs = ars[:, :C] + akv_scr[items]
        u = sel(_bmm(t_scr[items], jnp.concatenate([rhs, rhs], axis=1)))
        y = ars[:, C:] + sel(_bmm(rb_scr[items], jnp.concatenate([u, u], axis=1))) + rkv_scr[items]
        s_new = s * pc_rows[items] + _bmm_tn(jnp.concatenate([u, v_c], axis=1), btk_scr[items])
        state[...] = jnp.where(same_head, s_new, 0.0)
        for p in range(NP):
            y_scr[c * C:(c + 1) * C, p * LANES:(p + 1) * LANES] = y[p]

    y = y_scr[...]
    inv_n = 1.0 / RW_HEAD_DIM
    mean = _mm_exact_rhs(y, ones_h) * inv_n
    yc = y - mean
    var = _mm_exact_rhs(yc * yc, ones_h) * inv_n
    yn = yc * lax.rsqrt(var + RW_GN_EPS) * lnw_ref[...] + lnb_ref[...]
    o_ref[...] = (yn + bonus) * g


def _rwkv_mixer(p_rw, v_first, prm, bsz, seq, has_vres):
    m = p_rw.shape[0]
    tb = _pick_tile(seq, (256, 128, 64))
    nt = seq // tb
    nb = (tb // CHUNK) * (RW_HEADS // 2)
    W = RW_WIDTH
    row_spec = lambda width: pl.BlockSpec((tb, width), lambda b, t: (b * nt + t, 0))
    full = lambda a: pl.BlockSpec(a.shape, lambda b, t: (0,) * a.ndim)
    names = ["mu", "w0", "a0", "wwa", "gup", "kk", "ka", "rk", "lnw", "lnb"] + (["v0", "vup"] if has_vres else [])
    params = [prm[n] for n in names]
    ins = [p_rw] + ([v_first] if has_vres else []) + params
    in_specs = [row_spec(RW_GROUP)] + ([row_spec(W)] if has_vres else []) + [full(a) for a in params]
    if has_vres:
        out_shape = jax.ShapeDtypeStruct((m, W), F32)
        out_specs = row_spec(W)
    else:
        out_shape = (jax.ShapeDtypeStruct((m, W), F32), jax.ShapeDtypeStruct((m, W), F32))
        out_specs = (row_spec(W), row_spec(W))
    scratch = [
        pltpu.VMEM((SUBLANES, RW_GROUP), F32),
        pltpu.VMEM((RW_HEADS // 2, LANES, LANES), F32),
        pltpu.VMEM((tb, RW_GROUP), F32),
    ] + [pltpu.VMEM((tb, W), F32) for _ in range(6)] + [
        pltpu.VMEM((nb, 2 * CHUNK, 2 * CHUNK), F32),
        pltpu.VMEM((nb, 2 * CHUNK, 2 * CHUNK), F32),
        pltpu.VMEM((nb, CHUNK, LANES), F32),
        pltpu.VMEM((nb, CHUNK, LANES), F32),
        pltpu.VMEM((nb, 2 * CHUNK, LANES), F32),
    ]
    res = pl.pallas_call(
        functools.partial(_rwkv_kernel, has_vres, tb),
        out_shape=out_shape,
        grid=(bsz, nt),
        in_specs=in_specs,
        out_specs=out_specs,
        scratch_shapes=scratch,
        compiler_params=_cparams(("parallel", "arbitrary")),
        name="rwkv7_vres" if has_vres else "rwkv7",
    )(*ins)
    if has_vres:
        return res, None
    return res


def _gdn_kernel(tb, pg_ref, slab_ref, cw_ref, alog_ref, dtb_ref, onorm_ref, o_ref,
                carry, state, ext, q_scr, k_scr, v_scr, sig_scr, gcol_scr, grow_scr, o_scr,
                intra_scr, u_scr, wq_scr, kt_scr):
    H, Dh, Wd = GDN_HEADS, GDN_HEAD_DIM, GDN_WIDTH
    QKV = 3 * Wd
    tstep = pl.program_id(1)

    @pl.when(tstep == 0)
    def _():
        carry[...] = jnp.zeros_like(carry)
        state[...] = jnp.zeros_like(state)

    ext[0:SUBLANES, :] = carry[...]
    ext[SUBLANES:, :] = pg_ref[:, 0:QKV]
    carry[...] = pg_ref[tb - SUBLANES:tb, 0:QKV]
    for j in range(QKV // Dh):
        ln = slice(j * Dh, (j + 1) * Dh)
        acc = jnp.zeros((tb, Dh), F32)
        for kk in range(GDN_CONV):
            off = SUBLANES - (GDN_CONV - 1) + kk
            acc = acc + ext[pl.ds(off, tb), ln] * cw_ref[kk:kk + 1, ln]
        act = acc * jax.nn.sigmoid(acc)
        which, h = divmod(j, H)
        hl = slice(h * Dh, (h + 1) * Dh)
        if which == 0:
            nrm = lax.rsqrt(jnp.sum(act * act, axis=-1, keepdims=True) + 1e-6)
            q_scr[:, hl] = act * nrm * (Dh ** -0.5)
        elif which == 1:
            nrm = lax.rsqrt(jnp.sum(act * act, axis=-1, keepdims=True) + 1e-6)
            k_scr[:, hl] = act * nrm
        else:
            v_scr[:, hl] = act

    slab = slab_ref[...]
    sig_scr[...] = jax.nn.sigmoid(slab)
    gsl = -jnp.exp(alog_ref[...]) * jax.nn.softplus(slab + dtb_ref[...])
    lane = _iota((tb, LANES), 1)
    gsl = jnp.where((lane >= SLAB_ALPHA0) & (lane < SLAB_ALPHA0 + H), gsl, 0.0)

    C = CHUNK
    NC = tb // C
    trow, tcol = _iota((tb, tb), 0), _iota((tb, tb), 1)
    same_chunk = (trow // C) == (tcol // C)
    gcol_scr[...] = _mm_exact_lhs((same_chunk & (trow >= tcol)).astype(BF16), gsl)
    grow_scr[...] = _mm_tn_exact_rhs(gsl, (same_chunk & (trow <= tcol)).astype(BF16))

    def items(fn):
        return jnp.concatenate([fn(c, h)[None] for c in range(NC) for h in range(H)], axis=0)

    rows = lambda c: slice(c * C, (c + 1) * C)
    head = lambda h: slice(h * Dh, (h + 1) * Dh)
    gc = items(lambda c, h: gcol_scr[rows(c), SLAB_ALPHA0 + h:SLAB_ALPHA0 + h + 1])
    gr = items(lambda c, h: grow_scr[SLAB_ALPHA0 + h:SLAB_ALPHA0 + h + 1, rows(c)])
    beta = items(lambda c, h: sig_scr[rows(c), SLAB_BETA0 + h:SLAB_BETA0 + h + 1])
    q_b = items(lambda c, h: q_scr[rows(c), head(h)])
    k_b = items(lambda c, h: k_scr[rows(c), head(h)])
    v_b = items(lambda c, h: v_scr[rows(c), head(h)])
    row, col = _iota((1, C, C), 1), _iota((1, C, C), 2)
    dec = jnp.exp(jnp.where(row >= col, gc - gr, -jnp.inf))
    kb = k_b * beta
    prod = _bmm_nt(jnp.concatenate([kb, q_b], axis=1), k_b)
    lmat = jnp.where(row > col, prod[:, :C] * dec, 0.0)
    intra_scr[...] = prod[:, C:] * dec
    nb = NC * H
    l2 = (-lmat).reshape(nb // 2, 2 * C, C)
    prow, pcol = _iota((1, 2 * C, 2 * C), 1), _iota((1, 2 * C, 2 * C), 2)
    lp = jnp.where((prow // C) == (pcol // C), jnp.concatenate([l2, l2], axis=2), 0.0)
    tp = _neumann_inverse(lp)
    eg = jnp.exp(gc)
    rhs = jnp.concatenate([v_b * beta, kb * eg], axis=2).reshape(nb // 2, 2 * C, 2 * Dh)
    sol = _bmm(tp, rhs).reshape(nb, C, 2 * Dh)
    u_scr[...] = sol[:, :, :Dh]
    wq_scr[...] = jnp.concatenate([sol[:, :, Dh:], q_b * eg], axis=1)
    g_last = gc[:, C - 1:C, :]
    kt_scr[...] = k_b * jnp.exp(g_last - gc)
    eg_last = jnp.exp(g_last)

    for c in range(NC):
        it = slice(c * H, (c + 1) * H)
        s = state[...]
        ws = _bmm(wq_scr[it], s)
        v_new = u_scr[it] - ws[:, :C]
        o = ws[:, C:] + _bmm(intra_scr[it], v_new)
        state[...] = s * eg_last[it] + _bmm_tn(kt_scr[it], v_new)
        for h in range(H):
            o_scr[rows(c), head(h)] = o[h]

    for h in range(H):
        hl = slice(h * Dh, (h + 1) * Dh)
        o = o_scr[:, hl]
        o = o * lax.rsqrt(jnp.mean(o * o, axis=-1, keepdims=True) + NORM_EPS) * onorm_ref[...]
        z = pg_ref[:, QKV + h * Dh:QKV + (h + 1) * Dh]
        o_ref[:, hl] = o * (z * jax.nn.sigmoid(z))


def _gdn_mixer(p_gdn, p_rw, prm, bsz, seq):
    m = p_gdn.shape[0]
    tb = _pick_tile(seq, (256, 128))
    nt = seq // tb
    nb = (tb // CHUNK) * GDN_HEADS
    Wd = GDN_WIDTH
    params = [prm["cw"], prm["alog"], prm["dtb"], prm["onorm"]]
    full = lambda a: pl.BlockSpec(a.shape, lambda b, t: (0,) * a.ndim)
    slab_blk = RW_COLS // LANES
    return pl.pallas_call(
        functools.partial(_gdn_kernel, tb),
        out_shape=jax.ShapeDtypeStruct((m, Wd), F32),
        grid=(bsz, nt),
        in_specs=[pl.BlockSpec((tb, 4 * Wd), lambda b, t: (b * nt + t, 0)),
                  pl.BlockSpec((tb, LANES), lambda b, t: (b * nt + t, slab_blk))] + [full(a) for a in params],
        out_specs=pl.BlockSpec((tb, Wd), lambda b, t: (b * nt + t, 0)),
        scratch_shapes=[
            pltpu.VMEM((SUBLANES, 3 * Wd), F32),
            pltpu.VMEM((GDN_HEADS, GDN_HEAD_DIM, GDN_HEAD_DIM), F32),
            pltpu.VMEM((tb + SUBLANES, 3 * Wd), F32),
            pltpu.VMEM((tb, Wd), F32), pltpu.VMEM((tb, Wd), F32), pltpu.VMEM((tb, Wd), F32),
            pltpu.VMEM((tb, LANES), F32), pltpu.VMEM((tb, LANES), F32), pltpu.VMEM((LANES, tb), F32),
            pltpu.VMEM((tb, Wd), F32),
            pltpu.VMEM((nb, CHUNK, CHUNK), F32),
            pltpu.VMEM((nb, CHUNK, GDN_HEAD_DIM), F32),
            pltpu.VMEM((nb, 2 * CHUNK, GDN_HEAD_DIM), F32),
            pltpu.VMEM((nb, CHUNK, GDN_HEAD_DIM), F32),
        ],
        compiler_params=_cparams(("parallel", "arbitrary")),
        name="gdn",
    )(p_gdn, p_rw, *params)


def _swa_kernel(ps_ref, pos_ref, freq_ref, sgn_lo_ref, sgn_hi_ref, qn_ref, kn_ref, sink_ref, ones_ref, o_ref,
                kprev, vprev):
    Wn = SWA_WINDOW
    hd = SWA_HEAD_DIM
    G = SWA_Q_HEADS // SWA_KV_HEADS
    QW, KW = SWA_WIDTH, SWA_KV_WIDTH
    n = pl.program_id(1)

    @pl.when(n == 0)
    def _():
        kprev[...] = jnp.zeros_like(kprev)
        vprev[...] = jnp.zeros_like(vprev)

    half = ROPE_DIM // 2
    ang = pos_ref[...] * freq_ref[...]
    cs1, sn1 = jnp.cos(ang), jnp.sin(ang)
    lo1, hi1 = sn1 * sgn_lo_ref[...], sn1 * sgn_hi_ref[...]

    def norm_rope(x, gain_row, width):
        rep = lambda t: jnp.concatenate([t] * (width // LANES), axis=1)
        ms = _mm_exact_rhs(x * x, ones_ref[0:width, 0:width]) * (1.0 / hd)
        y = x * lax.rsqrt(ms + NORM_EPS) * gain_row
        up = pltpu.roll(y, width - half, 1)
        dn = pltpu.roll(y, half, 1)
        return y * rep(cs1) + up * rep(lo1) + dn * rep(hi1)

    q = norm_rope(ps_ref[:, 0:QW], qn_ref[...], QW) * (hd ** -0.5)
    k = norm_rope(ps_ref[:, QW:QW + KW], kn_ref[...], KW)
    v = ps_ref[:, QW + KW:QW + 2 * KW]

    qi = _iota((G * Wn, 2 * Wn), 0) % Wn
    kj = _iota((G * Wn, 2 * Wn), 1)
    rel = qi + Wn - kj
    allowed = (rel >= 0) & (rel < SWA_WINDOW) & ((kj >= Wn) | (n > 0))
    grp = _iota((G * Wn, 1), 0) // Wn
    sinks = sink_ref[...]

    for hk in range(SWA_KV_HEADS):
        kl = slice(hk * hd, (hk + 1) * hd)
        kband = jnp.concatenate([kprev[:, kl], k[:, kl]], axis=0)
        vband = jnp.concatenate([vprev[:, kl], v[:, kl]], axis=0)
        qs = jnp.concatenate([q[:, (hk * G + g) * hd:(hk * G + g + 1) * hd] for g in range(G)], axis=0)
        s = jnp.where(allowed, _mm_nt(qs, kband), -jnp.inf)
        sink = jnp.zeros((G * Wn, 1), F32)
        for g in range(G):
            sg = sinks[:, hk * G + g:hk * G + g + 1]
            sink = jnp.where(grp == g, sg, sink)
        mx = jnp.maximum(jnp.max(s, axis=-1, keepdims=True), sink)
        p = jnp.exp(s - mx)
        den = jnp.sum(p, axis=-1, keepdims=True) + jnp.exp(sink - mx)
        o = _mm(p, vband) / den
        for g in range(G):
            o_ref[:, (hk * G + g) * hd:(hk * G + g + 1) * hd] = o[g * Wn:(g + 1) * Wn]

    kprev[...] = k
    vprev[...] = v


def _swa_mixer(p_swa, pos_col, prm, bsz, seq):
    m = p_swa.shape[0]
    Wn = SWA_WINDOW
    nb = seq // Wn
    idx = np.arange(SWA_WIDTH) // SWA_HEAD_DIM
    head_ones = jnp.asarray(idx[:, None] == idx[None, :], BF16)
    params = [prm["freq"], prm["sgn_lo"], prm["sgn_hi"], prm["qn"], prm["kn"], prm["sinks"], head_ones]
    full = lambda a: pl.BlockSpec(a.shape, lambda b, t: (0,) * a.ndim)
    return pl.pallas_call(
        _swa_kernel,
        out_shape=jax.ShapeDtypeStruct((m, SWA_WIDTH), F32),
        grid=(bsz, nb),
        in_specs=[pl.BlockSpec((Wn, SWA_COLS), lambda b, t: (b * nb + t, 0)),
                  pl.BlockSpec((Wn, 1), lambda b, t: (b * nb + t, 0))] + [full(a) for a in params],
        out_specs=pl.BlockSpec((Wn, SWA_WIDTH), lambda b, t: (b * nb + t, 0)),
        scratch_shapes=[pltpu.VMEM((Wn, SWA_KV_WIDTH), F32), pltpu.VMEM((Wn, SWA_KV_WIDTH), F32)],
        compiler_params=_cparams(("parallel", "arbitrary")),
        name="swa",
    )(p_swa, pos_col, *params)


def _merge_kernel(x_ref, orw_ref, ogdn_ref, oswa_ref, g0_ref, g1_ref, g2_ref, wb0_ref, wb1_ref, wb2_ref,
                  wout_ref, o_ref, acc):
    j = pl.program_id(1)

    @pl.when(j == 0)
    def _():
        acc[...] = jnp.zeros_like(acc)

    d = lambda a_ref, w_ref: jnp.dot(a_ref[...].astype(BF16), w_ref[...], preferred_element_type=F32)
    merged = (jax.nn.sigmoid(g0_ref[...]) * d(orw_ref, wb0_ref)
              + jax.nn.sigmoid(g1_ref[...]) * d(ogdn_ref, wb1_ref)
              + jax.nn.sigmoid(g2_ref[...]) * d(oswa_ref, wb2_ref))
    acc[...] += jnp.dot(merged.astype(BF16), wout_ref[...], preferred_element_type=F32)

    @pl.when(j == pl.num_programs(1) - 1)
    def _():
        o_ref[...] = x_ref[...] + acc[...]


def _merge(x2, o_rw, o_gdn, o_swa, p_gate, wb_rw, wb_gdn, wb_swa, w_out):
    m, d = x2.shape
    tm = _pick_tile(m, (512, 256, 128, 64, 32, 16, 8))
    tn = 512
    nj = d // tn
    rows = lambda width: pl.BlockSpec((tm, width), lambda i, j: (i, 0))
    gate = lambda b: pl.BlockSpec((tm, tn), lambda i, j, b=b: (i, b * nj + j))
    wcol = lambda kdim: pl.BlockSpec((kdim, tn), lambda i, j: (0, j))
    return pl.pallas_call(
        _merge_kernel,
        out_shape=jax.ShapeDtypeStruct((m, d), F32),
        grid=(m // tm, nj),
        in_specs=[rows(d), rows(RW_WIDTH), rows(GDN_WIDTH), rows(SWA_WIDTH), gate(0), gate(1), gate(2),
                  wcol(RW_WIDTH), wcol(GDN_WIDTH), wcol(SWA_WIDTH),
                  pl.BlockSpec((tn, d), lambda i, j: (j, 0))],
        out_specs=rows(d),
        scratch_shapes=[pltpu.VMEM((tm, d), F32)],
        compiler_params=_cparams(("parallel", "arbitrary")),
        name="merge",
    )(x2, o_rw, o_gdn, o_swa, p_gate, p_gate, p_gate, wb_rw, wb_gdn, wb_swa, w_out)


def _ffn_kernel(tm, seq, x_ref, halo_ref, gain_ref, wg_ref, wu_ref, cg_ref, cu_ref, wd_ref, o_ref,
                h_scr, u_scr, acc):
    i = pl.program_id(0)
    j = pl.program_id(1)
    tn = wg_ref.shape[1]

    @pl.when(j == 0)
    def _():
        def norm(x):
            return (x * lax.rsqrt(jnp.mean(x * x, axis=-1, keepdims=True) + NORM_EPS) * gain_ref[...]).astype(BF16)
        seq_start = (i * tm) % seq == 0
        halo = jnp.where(seq_start, 0.0, halo_ref[...])
        h_scr[0:2 * SUBLANES, :] = norm(jnp.concatenate([jnp.zeros_like(halo), halo], axis=0))
        h_scr[2 * SUBLANES:, :] = norm(x_ref[...])
        acc[...] = jnp.zeros_like(acc)

    h = h_scr[...]
    u_scr[:, 0:tn] = jnp.dot(h, wg_ref[...], preferred_element_type=F32)
    u_scr[:, tn:2 * tn] = jnp.dot(h, wu_ref[...], preferred_element_type=F32)

    def conv(c0, cw_ref):
        out = jnp.zeros((tm, tn), F32)
        for kk in range(FFN_CONV):
            off = 2 * SUBLANES - (FFN_CONV - 1) + kk
            out = out + u_scr[pl.ds(off, tm), c0:c0 + tn] * cw_ref[kk:kk + 1, :]
        return out

    gate = conv(0, cg_ref)
    up = conv(tn, cu_ref)
    act = gate * jax.nn.sigmoid(gate) * up
    acc[...] += jnp.dot(act.astype(BF16), wd_ref[...], preferred_element_type=F32)

    @pl.when(j == pl.num_programs(1) - 1)
    def _():
        o_ref[...] = x_ref[...] + acc[...]


def _conv_ffn(x2, gain, w_gate, w_up, c_gate, c_up, w_down, seq):
    m, d = x2.shape
    dff = w_gate.shape[1]
    tm = _pick_tile(seq, (512, 256, 128, 64, 32, 16))
    tn = _pick_tile(dff, (512, 256, 128))
    nj = dff // tn
    hb = tm // SUBLANES
    return pl.pallas_call(
        functools.partial(_ffn_kernel, tm, seq),
        out_shape=jax.ShapeDtypeStruct((m, d), F32),
        grid=(m // tm, nj),
        in_specs=[
            pl.BlockSpec((tm, d), lambda i, j: (i, 0)),
            pl.BlockSpec((SUBLANES, d), lambda i, j: (jnp.maximum(i * hb - 1, 0), 0)),
            pl.BlockSpec((1, d), lambda i, j: (0, 0)),
            pl.BlockSpec((d, tn), lambda i, j: (0, j)),
            pl.BlockSpec((d, tn), lambda i, j: (0, j)),
            pl.BlockSpec((FFN_CONV, tn), lambda i, j: (0, j)),
            pl.BlockSpec((FFN_CONV, tn), lambda i, j: (0, j)),
            pl.BlockSpec((tn, d), lambda i, j: (j, 0)),
        ],
        out_specs=pl.BlockSpec((tm, d), lambda i, j: (i, 0)),
        scratch_shapes=[pltpu.VMEM((tm + 2 * SUBLANES, d), BF16),
                        pltpu.VMEM((tm + 2 * SUBLANES, 2 * tn), F32),
                        pltpu.VMEM((tm, d), F32)],
        compiler_params=_cparams(("parallel", "arbitrary")),
        name="conv_ffn",
    )(x2, x2, gain.reshape(1, d), w_gate, w_up, c_gate, c_up, w_down)


def _row(a):
    return a.reshape(1, -1).astype(F32)


def _pad_cols(a, width):
    return jnp.pad(a, ((0, 0), (0, width - a.shape[1])))


def _layer_params(l, w_in, rw_mu, rw_w0, rw_w_up, rw_a0, rw_a_up, rw_g_up, rw_k_k, rw_k_a, rw_r_k, rw_ln_w,
                  rw_ln_b, vres_down, vres_mu, vres_v0, vres_v_up, gdn_conv, gdn_a_log, gdn_dt_bias, gdn_o_norm,
                  swa_q_norm, swa_k_norm, swa_sinks):
    d = w_in.shape[1]
    c_gdn = RW_COLS
    c_ba = c_gdn + 4 * GDN_WIDTH
    c_swa = c_ba + 2 * GDN_HEADS
    c_gate = c_swa + SWA_COLS
    has_vres = l > 0
    vres_w = vres_down[l - 1] if has_vres else jnp.zeros((d, RW_VRES_RANK), F32)
    ba_w = lax.dynamic_slice(w_in, (l, 0, c_ba), (1, d, 2 * GDN_HEADS))[0]
    slab_w = _pad_cols(jnp.concatenate([vres_w, ba_w], axis=1), LANES).astype(BF16)
    w_rw = jnp.concatenate([_pack_cols(w_in, l, 0, RW_COLS), slab_w], axis=1)
    w_gdn = _pack_cols(w_in, l, c_gdn, 4 * GDN_WIDTH)
    w_swa = _pack_cols(w_in, l, c_swa, SWA_COLS)
    w_gate = _pack_cols(w_in, l, c_gate, w_in.shape[2] - c_gate)

    W = RW_WIDTH
    mu = jnp.concatenate([rw_mu[l], vres_mu[l - 1] if has_vres else jnp.zeros((RW_VRES_RANK,), F32),
                          jnp.zeros((LANES - RW_VRES_RANK,), F32)])
    wwa = jnp.zeros((LANES, 2 * W), F32)
    wwa = wwa.at[:RW_DECAY_RANK, :W].set(rw_w_up[l]).at[RW_DECAY_RANK:, W:].set(rw_a_up[l])
    rw = dict(mu=_row(mu), w0=_row(rw_w0[l]), a0=_row(rw_a0[l]), wwa=wwa.astype(BF16),
              gup=rw_g_up[l].astype(BF16), kk=_row(rw_k_k[l]), ka=_row(rw_k_a[l]), rk=_row(rw_r_k[l]),
              lnw=_row(rw_ln_w[l]), lnb=_row(rw_ln_b[l]))
    if has_vres:
        vup = jnp.zeros((LANES, W), F32).at[:RW_VRES_RANK].set(vres_v_up[l - 1])
        rw.update(v0=_row(vres_v0[l - 1]), vup=vup.astype(BF16))

    lanes_pad = lambda a, off: jnp.zeros((1, LANES), F32).at[0, off:off + a.shape[0]].set(a)
    gdn = dict(cw=gdn_conv[l].astype(F32), alog=lanes_pad(gdn_a_log[l], SLAB_ALPHA0),
               dtb=lanes_pad(gdn_dt_bias[l], SLAB_ALPHA0), onorm=_row(gdn_o_norm[l]))

    half = ROPE_DIM // 2
    lane = np.arange(LANES) % SWA_HEAD_DIM
    inv_freq = ROPE_THETA ** (-(lane % half).astype(np.float32) * 2.0 / ROPE_DIM)
    freq = np.where(lane < ROPE_DIM, inv_freq, 0.0).astype(np.float32)
    sgn_lo = np.where(lane < half, -1.0, 0.0).astype(np.float32)
    sgn_hi = np.where((lane >= half) & (lane < ROPE_DIM), 1.0, 0.0).astype(np.float32)
    swa = dict(freq=jnp.asarray(freq).reshape(1, -1), sgn_lo=jnp.asarray(sgn_lo).reshape(1, -1),
               sgn_hi=jnp.asarray(sgn_hi).reshape(1, -1),
               qn=_row(jnp.tile(swa_q_norm[l], SWA_Q_HEADS)), kn=_row(jnp.tile(swa_k_norm[l], SWA_KV_HEADS)),
               sinks=lanes_pad(swa_sinks[l], 0))
    return dict(w_rw=w_rw, w_gdn=w_gdn, w_swa=w_swa, w_gate=w_gate, rw=rw, gdn=gdn, swa=swa, has_vres=has_vres)


def kernel(x, positions, norm_mix, w_in, rw_mu, rw_w0, rw_w_up, rw_a0, rw_a_up, rw_g_up, rw_k_k, rw_k_a, rw_r_k, rw_ln_w, rw_ln_b, vres_down, vres_mu, vres_v0, vres_v_up, gdn_conv, gdn_a_log, gdn_dt_bias, gdn_o_norm, swa_q_norm, swa_k_norm, swa_sinks, w_branch, w_out, norm_ffn, ffn_up, ffn_conv, ffn_down):
    bsz, seq, d = x.shape
    depth = w_in.shape[0]
    x2 = x.reshape(bsz * seq, d)
    pos_col = positions.astype(F32).reshape(bsz * seq, 1)
    dff = ffn_down.shape[1]
    v_first = None
    for l in range(depth):
        lp = _layer_params(l, w_in, rw_mu, rw_w0, rw_w_up, rw_a0, rw_a_up, rw_g_up, rw_k_k, rw_k_a, rw_r_k,
                           rw_ln_w, rw_ln_b, vres_down, vres_mu, vres_v0, vres_v_up, gdn_conv, gdn_a_log,
                           gdn_dt_bias, gdn_o_norm, swa_q_norm, swa_k_norm, swa_sinks)
        p_rw = _inproj(x2, norm_mix[l], lp["w_rw"], "inproj_rw")
        p_gdn = _inproj(x2, norm_mix[l], lp["w_gdn"], "inproj_gdn")
        p_swa = _inproj(x2, norm_mix[l], lp["w_swa"], "inproj_swa")
        p_gate = _inproj(x2, norm_mix[l], lp["w_gate"], "inproj_gate")
        if lp["has_vres"]:
            o_rw, _ = _rwkv_mixer(p_rw, v_first, lp["rw"], bsz, seq, True)
        else:
            o_rw, v_first = _rwkv_mixer(p_rw, None, lp["rw"], bsz, seq, False)
        o_gdn = _gdn_mixer(p_gdn, p_rw, lp["gdn"], bsz, seq)
        o_swa = _swa_mixer(p_swa, pos_col, lp["swa"], bsz, seq)
        wb = w_branch[l].astype(BF16)
        x2 = _merge(x2, o_rw, o_gdn, o_swa, p_gate, wb[:RW_WIDTH], wb[RW_WIDTH:RW_WIDTH + GDN_WIDTH],
                    wb[RW_WIDTH + GDN_WIDTH:], w_out[l].astype(BF16))
        up = ffn_up[l].astype(BF16)
        x2 = _conv_ffn(x2, norm_ffn[l], up[:, :dff], up[:, dff:], ffn_conv[l][:, :dff], ffn_conv[l][:, dff:],
                       ffn_down[l].astype(BF16), seq)
    return x2.reshape(bsz, seq, d)
```

```python
import functools

import jax
import jax.numpy as jnp
import numpy as np
from jax import lax
from jax.experimental import pallas as pl
from jax.experimental.pallas import tpu as pltpu

F32 = jnp.float32
BF16 = jnp.bfloat16

D_MODEL = 2048
RW_HEADS, RW_HEAD_DIM = 8, 64
RW_WIDTH = RW_HEADS * RW_HEAD_DIM
RW_DECAY_RANK, RW_ICLR_RANK, RW_GATE_RANK, RW_VRES_RANK = 64, 64, 128, 32
RW_GN_EPS = 64e-5
RW_COLS = 3 * RW_WIDTH + RW_DECAY_RANK + RW_ICLR_RANK + RW_GATE_RANK
GDN_HEADS, GDN_HEAD_DIM = 6, 128
GDN_WIDTH = GDN_HEADS * GDN_HEAD_DIM
GDN_CONV = 4
SWA_Q_HEADS, SWA_KV_HEADS, SWA_HEAD_DIM = 12, 4, 64
SWA_WIDTH = SWA_Q_HEADS * SWA_HEAD_DIM
SWA_KV_WIDTH = SWA_KV_HEADS * SWA_HEAD_DIM
SWA_WINDOW = 128
SWA_COLS = SWA_WIDTH + 2 * SWA_KV_WIDTH
ROPE_DIM = SWA_HEAD_DIM // 4
ROPE_THETA = 500000.0
D_FF = 5632
FFN_CONV = 3
NORM_EPS = 1e-6

LANES = 128
SUBLANES = 8
VMEM_LIMIT_BYTES = 56 * 1024 * 1024

CHUNK = 64
INV_BLOCK = 16
SLAB_VRES0 = 0
SLAB_BETA0 = RW_VRES_RANK
SLAB_ALPHA0 = RW_VRES_RANK + GDN_HEADS
RW_GROUP = RW_COLS + LANES
_G = SWA_Q_HEADS // SWA_KV_HEADS
SWA_Q_ORDER = tuple((2 * p + side) * _G + t for p in range(SWA_KV_HEADS // 2) for t in range(_G) for side in range(2))

def _cparams(sem):
    return pltpu.CompilerParams(dimension_semantics=sem, vmem_limit_bytes=VMEM_LIMIT_BYTES)


def _dg(a, b, dims):
    return lax.dot_general(a.astype(BF16), b.astype(BF16), dims, preferred_element_type=F32)


def _mm(a, b):
    return _dg(a, b, (((1,), (0,)), ((), ())))


def _mm_nt(a, b):
    return _dg(a, b, (((1,), (1,)), ((), ())))


def _mm_tn(a, b):
    return _dg(a, b, (((0,), (0,)), ((), ())))


def _bmm(a, b):
    return _dg(a, b, (((2,), (1,)), ((0,), (0,))))


def _bmm_nt(a, b):
    return _dg(a, b, (((2,), (2,)), ((0,), (0,))))


def _bmm_tn(a, b):
    return _dg(a, b, (((1,), (1,)), ((0,), (0,))))


def _split3(a):
    hi = a.astype(BF16)
    r1 = a - hi.astype(F32)
    mid = r1.astype(BF16)
    lo = (r1 - mid.astype(F32)).astype(BF16)
    return hi, mid, lo


def _mm_exact_lhs(a_bf16, b):
    hi, mid, lo = _split3(b)
    d = lambda t: jnp.dot(a_bf16, t, preferred_element_type=F32)
    return d(hi) + d(mid) + d(lo)


def _mm_tn_exact_rhs(a, b_bf16):
    hi, mid, lo = _split3(a)
    d = lambda t: _mm_tn(t, b_bf16)
    return d(hi) + d(mid) + d(lo)


def _iota(shape, dim):
    return lax.broadcasted_iota(jnp.int32, shape, dim)


def _sigmoid(x):
    return 0.5 * jnp.tanh(0.5 * x) + 0.5


def _neumann_inverse(lmat):
    assert CHUNK // INV_BLOCK == 4
    n = lmat.shape[-1]
    row, col = _iota((1, n, n), 1), _iota((1, n, n), 2)
    eye = (row == col).astype(F32)
    in_blk = (row // INV_BLOCK) == (col // INV_BLOCK)
    lb = jnp.where(in_blk, lmat, 0.0)
    e = lmat - lb
    inv = eye + lb
    p = _bmm(lb, lb)
    steps = INV_BLOCK.bit_length() - 2
    for s in range(steps):
        inv = inv + _bmm(inv, p)
        if s + 1 < steps:
            p = _bmm(p, p)
    nmat = _bmm(inv, e)
    n2 = _bmm(nmat, nmat)
    m = eye + nmat + n2 + _bmm(nmat, n2)
    return _bmm(m, inv)


def _pack_kernel(shift, *refs):
    if shift == 0:
        a_ref, o_ref = refs
        o_ref[...] = a_ref[...].astype(BF16)
    else:
        a_ref, b_ref, o_ref = refs
        lane = _iota(a_ref.shape, 1)
        moved_a = pltpu.roll(a_ref[...], LANES - shift, 1)
        moved_b = pltpu.roll(b_ref[...], LANES - shift, 1)
        o_ref[...] = jnp.where(lane < LANES - shift, moved_a, moved_b).astype(BF16)


def _pack_cols(w3, layer, start, width):
    _, d, n = w3.shape
    q, shift = divmod(start, LANES)
    last_blk = pl.cdiv(n, LANES) - 1
    src = lambda off: pl.BlockSpec((None, d, LANES), lambda t: (layer, 0, jnp.minimum(q + t + off, last_blk)))
    ins = [w3] if shift == 0 else [w3, w3]
    return pl.pallas_call(
        functools.partial(_pack_kernel, shift),
        out_shape=jax.ShapeDtypeStruct((d, width), BF16),
        grid=(width // LANES,),
        in_specs=[src(0)] if shift == 0 else [src(0), src(1)],
        out_specs=pl.BlockSpec((d, LANES), lambda t: (0, t)),
        compiler_params=_cparams(("arbitrary",)),
        name="pack_cols",
    )(*ins)


def _inproj_kernel(x_ref, g_ref, w_ref, o_ref, h_scr):
    @pl.when(pl.program_id(1) == 0)
    def _():
        x = x_ref[...]
        y = x * lax.rsqrt(jnp.mean(x * x, axis=-1, keepdims=True) + NORM_EPS) * g_ref[...]
        h_scr[...] = y.astype(BF16)

    o_ref[...] = jnp.dot(h_scr[...], w_ref[...], preferred_element_type=F32)


def _pick_tile(n, prefs):
    for t in prefs:
        if n % t == 0:
            return t
    return n


def _inproj(x2, gain, w_bf16, name):
    m, d = x2.shape
    n = w_bf16.shape[1]
    tm = _pick_tile(m, (1024, 512, 256, 128, 64, 32, 16, 8))
    tn = _pick_tile(n, (512, 640, 384, 256, 128))
    return pl.pallas_call(
        _inproj_kernel,
        out_shape=jax.ShapeDtypeStruct((m, n), F32),
        grid=(m // tm, n // tn),
        in_specs=[
            pl.BlockSpec((tm, d), lambda i, j: (i, 0)),
            pl.BlockSpec((1, d), lambda i, j: (0, 0)),
            pl.BlockSpec((d, tn), lambda i, j: (0, j)),
        ],
        out_specs=pl.BlockSpec((tm, tn), lambda i, j: (i, j)),
        scratch_shapes=[pltpu.VMEM((tm, d), BF16)],
        compiler_params=_cparams(("parallel", "arbitrary")),
        name=name,
    )(x2, gain.reshape(1, d), w_bf16)


def _rwkv_kernel(has_vres, tb, *refs):
    scr = refs[-14:]
    (carry, state, pl_scr, at_scr, rt_scr, bt_scr, kt_scr, v_scr, y_scr,
     t_scr, rb_scr, akv_scr, rkv_scr, btk_scr) = scr
    if has_vres:
        (p_ref, vf_ref, mu_ref, w0_ref, a0_ref, wwa_ref, gup_ref, kk_ref, ka_ref, rk_ref, lnw_ref, lnb_ref,
         hones_ref, v0_ref, vup_ref, o_ref) = refs[:-14]
    else:
        (p_ref, mu_ref, w0_ref, a0_ref, wwa_ref, gup_ref, kk_ref, ka_ref, rk_ref, lnw_ref, lnb_ref,
         hones_ref, o_ref, vout_ref) = refs[:-14]
    W = RW_WIDTH
    tstep = pl.program_id(1)

    @pl.when(tstep == 0)
    def _():
        carry[...] = jnp.zeros_like(carry)
        state[...] = jnp.zeros_like(state)

    row0 = _iota((tb, LANES), 0) == 0
    for c0 in range(0, RW_GROUP, LANES):
        x = p_ref[:, c0:c0 + LANES]
        prev = pltpu.roll(x, 1, 0)
        prev = jnp.where(row0, carry[SUBLANES - 1:SUBLANES, c0:c0 + LANES], prev)
        pl_scr[:, c0:c0 + LANES] = x + (prev - x) * mu_ref[:, c0:c0 + LANES]
    carry[...] = p_ref[tb - SUBLANES:tb, :]

    head_sum = lambda t: _mm(t, hones_ref[...])

    c_wd = 3 * W
    x128 = pl_scr[:, c_wd:c_wd + LANES]
    lane = _iota((tb, LANES), 1)
    xin = jnp.where(lane < RW_DECAY_RANK, jnp.tanh(x128), x128)
    wa = jnp.dot(xin.astype(BF16), wwa_ref[...], preferred_element_type=F32)
    w_log = -jax.nn.softplus(-(w0_ref[...] + wa[:, :W])) - 0.5
    lw = -jnp.exp(w_log)
    asig = _sigmoid(a0_ref[...] + wa[:, W:])
    gd = pl_scr[:, c_wd + LANES:c_wd + 2 * LANES]
    g = jnp.dot(_sigmoid(gd).astype(BF16), gup_ref[...], preferred_element_type=F32)

    v = pl_scr[:, 2 * W:3 * W]
    if has_vres:
        vd = pl_scr[:, RW_COLS:RW_COLS + LANES]
        vl = jnp.dot(vd.astype(BF16), vup_ref[...], preferred_element_type=F32)
        v = v + (vf_ref[...] - v) * _sigmoid(v0_ref[...] + vl)
    else:
        vout_ref[...] = v
    v_scr[...] = v

    k = pl_scr[:, W:2 * W]
    kkraw = k * kk_ref[...]
    ssq = head_sum(kkraw * kkraw)
    kk = kkraw * lax.rsqrt(ssq + 1e-12)
    kfin = k * (1.0 + (asig - 1.0) * ka_ref[...])
    r = pl_scr[:, 0:W]
    bonus = head_sum(r * kfin * rk_ref[...]) * v

    C = CHUNK
    P2 = 2 * C
    NC, NP = tb // C, RW_HEADS // 2
    tril_b = (_iota((NC, C, C), 1) >= _iota((NC, C, C), 2)).astype(BF16)
    cum = sum(_bmm(tril_b, t.reshape(NC, C, W)) for t in _split3(lw)).reshape(tb, W)
    e_pos = jnp.exp(cum)
    e_neg = jnp.exp(-cum)
    rt_scr[...] = r * e_pos
    kt_scr[...] = kfin * e_neg
    bt_scr[...] = (kk * asig) * e_neg
    at_scr[...] = -kk * jnp.exp(cum - lw)

    def to_b(ref):
        return jnp.concatenate([ref[c * C:(c + 1) * C, p * LANES:(p + 1) * LANES][None]
                                for c in range(NC) for p in range(NP)], axis=0)

    row, col = _iota((1, P2, P2), 1), _iota((1, P2, P2), 2)
    same_head = (row // C) == (col // C)
    strict = same_head & ((row % C) > (col % C))
    incl = same_head & ((row % C) >= (col % C))
    left = _iota((1, C, LANES), 2) < RW_HEAD_DIM
    leftf = left.astype(F32)
    rightf = 1.0 - leftf

    def sel(x3):
        return jnp.where(left, x3[:, :C], x3[:, C:])

    at_b, rt_b, bt_b, kt_b, vp_b = to_b(at_scr), to_b(rt_scr), to_b(bt_scr), to_b(kt_scr), to_b(v_scr)
    lhs = jnp.concatenate([at_b * leftf, rt_b * leftf, at_b * rightf, rt_b * rightf], axis=1)
    out_a = _bmm_nt(lhs, jnp.concatenate([bt_b, kt_b], axis=1))
    out_b = _bmm_nt(lhs, jnp.concatenate([kt_b, bt_b], axis=1))
    ab = jnp.where(strict, jnp.concatenate([out_a[:, 0:C], out_b[:, 2 * C:3 * C]], axis=1), 0.0)
    ak = jnp.where(strict, jnp.concatenate([out_b[:, 0:C], out_a[:, 2 * C:3 * C]], axis=1), 0.0)
    rb = jnp.where(incl, jnp.concatenate([out_a[:, C:2 * C], out_b[:, 3 * C:4 * C]], axis=1), 0.0)
    rk = jnp.where(incl, jnp.concatenate([out_b[:, C:2 * C], out_a[:, 3 * C:4 * C]], axis=1), 0.0)
    t_scr[...] = _neumann_inverse(ab)
    rb_scr[...] = rb
    vv = jnp.concatenate([vp_b, vp_b], axis=1)
    akv_scr[...] = sel(_bmm(ak, vv))
    rkv_scr[...] = sel(_bmm(rk, vv))
    pc_rows = jnp.concatenate([e_pos[(c + 1) * C - 1:(c + 1) * C, p * LANES:(p + 1) * LANES][None]
                               for c in range(NC) for p in range(NP)], axis=0)
    btk_scr[...] = jnp.concatenate([bt_b * pc_rows, kt_b * pc_rows], axis=1)

    for c in range(NC):
        items = slice(c * NP, (c + 1) * NP)
        s = state[...]
        a_c, r_c, v_c = at_b[items], rt_b[items], vp_b[items]
        ars = _bmm_nt(jnp.concatenate([a_c, r_c], axis=1), s)
        rhs = ars[:, :C] + akv_scr[items]
        u = sel(_bmm(t_scr[items], jnp.concatenate([rhs, rhs], axis=1)))
        y = ars[:, C:] + sel(_bmm(rb_scr[items], jnp.concatenate([u, u], axis=1))) + rkv_scr[items]
        s_new = s * pc_rows[items] + _bmm_tn(jnp.concatenate([u, v_c], axis=1), btk_scr[items])
        state[...] = jnp.where(same_head, s_new, 0.0)
        for p in range(NP):
            y_scr[c * C:(c + 1) * C, p * LANES:(p + 1) * LANES] = y[p]

    y = y_scr[...]
    inv_n = 1.0 / RW_HEAD_DIM
    y_hi = y.astype(BF16).astype(F32)
    mean = (head_sum(y_hi) + head_sum(y - y_hi)) * inv_n
    yc = y - mean
    var = head_sum(yc * yc) * inv_n
    yn = yc * lax.rsqrt(var + RW_GN_EPS) * lnw_ref[...] + lnb_ref[...]
    o_ref[...] = (yn + bonus) * g


def _rwkv_mixer(p_rw, v_first, prm, bsz, seq, has_vres):
    m = p_rw.shape[0]
    tb = _pick_tile(seq, (256, 128, 64))
    nt = seq // tb
    nb = (tb // CHUNK) * (RW_HEADS // 2)
    W = RW_WIDTH
    row_spec = lambda width: pl.BlockSpec((tb, width), lambda b, t: (b * nt + t, 0))
    full = lambda a: pl.BlockSpec(a.shape, lambda b, t: (0,) * a.ndim)
    names = (["mu", "w0", "a0", "wwa", "gup", "kk", "ka", "rk", "lnw", "lnb", "hones"]
             + (["v0", "vup"] if has_vres else []))
    params = [prm[n] for n in names]
    ins = [p_rw] + ([v_first] if has_vres else []) + params
    in_specs = [row_spec(RW_GROUP)] + ([row_spec(W)] if has_vres else []) + [full(a) for a in params]
    if has_vres:
        out_shape = jax.ShapeDtypeStruct((m, W), F32)
        out_specs = row_spec(W)
    else:
        out_shape = (jax.ShapeDtypeStruct((m, W), F32), jax.ShapeDtypeStruct((m, W), F32))
        out_specs = (row_spec(W), row_spec(W))
    scratch = [
        pltpu.VMEM((SUBLANES, RW_GROUP), F32),
        pltpu.VMEM((RW_HEADS // 2, LANES, LANES), F32),
        pltpu.VMEM((tb, RW_GROUP), F32),
    ] + [pltpu.VMEM((tb, W), F32) for _ in range(6)] + [
        pltpu.VMEM((nb, 2 * CHUNK, 2 * CHUNK), F32),
        pltpu.VMEM((nb, 2 * CHUNK, 2 * CHUNK), F32),
        pltpu.VMEM((nb, CHUNK, LANES), F32),
        pltpu.VMEM((nb, CHUNK, LANES), F32),
        pltpu.VMEM((nb, 2 * CHUNK, LANES), F32),
    ]
    res = pl.pallas_call(
        functools.partial(_rwkv_kernel, has_vres, tb),
        out_shape=out_shape,
        grid=(bsz, nt),
        in_specs=in_specs,
        out_specs=out_specs,
        scratch_shapes=scratch,
        compiler_params=_cparams(("parallel", "arbitrary")),
        name="rwkv7_vres" if has_vres else "rwkv7",
    )(*ins)
    if has_vres:
        return res, None
    return res


def _gdn_kernel(tb, pg_ref, slab_ref, cw_ref, alog_ref, dtb_ref, onorm_ref, o_ref,
                carry, state, ext, q_scr, k_scr, v_scr, sig_scr, gcol_scr, grow_scr, o_scr,
                intra_scr, u_scr, wq_scr, kt_scr):
    H, Dh, Wd = GDN_HEADS, GDN_HEAD_DIM, GDN_WIDTH
    QKV = 3 * Wd
    tstep = pl.program_id(1)

    @pl.when(tstep == 0)
    def _():
        carry[...] = jnp.zeros_like(carry)
        state[...] = jnp.zeros_like(state)

    ext[0:SUBLANES, :] = carry[...]
    ext[SUBLANES:, :] = pg_ref[:, 0:QKV]
    carry[...] = pg_ref[tb - SUBLANES:tb, 0:QKV]
    for j in range(QKV // Dh):
        ln = slice(j * Dh, (j + 1) * Dh)
        acc = jnp.zeros((tb, Dh), F32)
        for kk in range(GDN_CONV):
            off = SUBLANES - (GDN_CONV - 1) + kk
            acc = acc + ext[pl.ds(off, tb), ln] * cw_ref[kk:kk + 1, ln]
        act = acc * _sigmoid(acc)
        which, h = divmod(j, H)
        hl = slice(h * Dh, (h + 1) * Dh)
        if which == 0:
            nrm = lax.rsqrt(jnp.sum(act * act, axis=-1, keepdims=True) + 1e-6)
            q_scr[:, hl] = act * nrm * (Dh ** -0.5)
        elif which == 1:
            nrm = lax.rsqrt(jnp.sum(act * act, axis=-1, keepdims=True) + 1e-6)
            k_scr[:, hl] = act * nrm
        else:
            v_scr[:, hl] = act

    slab = slab_ref[...]
    sig_scr[...] = _sigmoid(slab)
    gsl = -jnp.exp(alog_ref[...]) * jax.nn.softplus(slab + dtb_ref[...])
    lane = _iota((tb, LANES), 1)
    gsl = jnp.where((lane >= SLAB_ALPHA0) & (lane < SLAB_ALPHA0 + H), gsl, 0.0)

    C = CHUNK
    NC = tb // C
    trow, tcol = _iota((tb, tb), 0), _iota((tb, tb), 1)
    same_chunk = (trow // C) == (tcol // C)
    gcol_scr[...] = _mm_exact_lhs((same_chunk & (trow >= tcol)).astype(BF16), gsl)
    grow_scr[...] = _mm_tn_exact_rhs(gsl, (same_chunk & (trow <= tcol)).astype(BF16))

    def items(fn):
        return jnp.concatenate([fn(c, h)[None] for c in range(NC) for h in range(H)], axis=0)

    rows = lambda c: slice(c * C, (c + 1) * C)
    head = lambda h: slice(h * Dh, (h + 1) * Dh)
    gc = items(lambda c, h: gcol_scr[rows(c), SLAB_ALPHA0 + h:SLAB_ALPHA0 + h + 1])
    gr = items(lambda c, h: grow_scr[SLAB_ALPHA0 + h:SLAB_ALPHA0 + h + 1, rows(c)])
    beta = items(lambda c, h: sig_scr[rows(c), SLAB_BETA0 + h:SLAB_BETA0 + h + 1])
    q_b = items(lambda c, h: q_scr[rows(c), head(h)])
    k_b = items(lambda c, h: k_scr[rows(c), head(h)])
    v_b = items(lambda c, h: v_scr[rows(c), head(h)])
    row, col = _iota((1, C, C), 1), _iota((1, C, C), 2)
    dec = jnp.exp(jnp.where(row >= col, gc - gr, -jnp.inf))
    kb = k_b * beta
    prod = _bmm_nt(jnp.concatenate([kb, q_b], axis=1), k_b)
    lmat = jnp.where(row > col, prod[:, :C] * dec, 0.0)
    intra_scr[...] = prod[:, C:] * dec
    nb = NC * H
    l2 = (-lmat).reshape(nb // 2, 2 * C, C)
    prow, pcol = _iota((1, 2 * C, 2 * C), 1), _iota((1, 2 * C, 2 * C), 2)
    lp = jnp.where((prow // C) == (pcol // C), jnp.concatenate([l2, l2], axis=2), 0.0)
    tp = _neumann_inverse(lp)
    eg = jnp.exp(gc)
    rhs = jnp.concatenate([v_b * beta, kb * eg], axis=2).reshape(nb // 2, 2 * C, 2 * Dh)
    sol = _bmm(tp, rhs).reshape(nb, C, 2 * Dh)
    u_scr[...] = sol[:, :, :Dh]
    wq_scr[...] = jnp.concatenate([sol[:, :, Dh:], q_b * eg], axis=1)
    g_last = gc[:, C - 1:C, :]
    kt_scr[...] = k_b * jnp.exp(g_last - gc)
    eg_last = jnp.exp(g_last)

    for c in range(NC):
        it = slice(c * H, (c + 1) * H)
        s = state[...]
        ws = _bmm(wq_scr[it], s)
        v_new = u_scr[it] - ws[:, :C]
        o = ws[:, C:] + _bmm(intra_scr[it], v_new)
        state[...] = s * eg_last[it] + _bmm_tn(kt_scr[it], v_new)
        for h in range(H):
            o_scr[rows(c), head(h)] = o[h]

    for h in range(H):
        hl = slice(h * Dh, (h + 1) * Dh)
        o = o_scr[:, hl]
        o = o * lax.rsqrt(jnp.mean(o * o, axis=-1, keepdims=True) + NORM_EPS) * onorm_ref[...]
        z = pg_ref[:, QKV + h * Dh:QKV + (h + 1) * Dh]
        o_ref[:, hl] = o * (z * _sigmoid(z))


def _gdn_mixer(p_gdn, p_rw, prm, bsz, seq):
    m = p_gdn.shape[0]
    tb = _pick_tile(seq, (256, 128))
    nt = seq // tb
    nb = (tb // CHUNK) * GDN_HEADS
    Wd = GDN_WIDTH
    params = [prm["cw"], prm["alog"], prm["dtb"], prm["onorm"]]
    full = lambda a: pl.BlockSpec(a.shape, lambda b, t: (0,) * a.ndim)
    slab_blk = RW_COLS // LANES
    return pl.pallas_call(
        functools.partial(_gdn_kernel, tb),
        out_shape=jax.ShapeDtypeStruct((m, Wd), F32),
        grid=(bsz, nt),
        in_specs=[pl.BlockSpec((tb, 4 * Wd), lambda b, t: (b * nt + t, 0)),
                  pl.BlockSpec((tb, LANES), lambda b, t: (b * nt + t, slab_blk))] + [full(a) for a in params],
        out_specs=pl.BlockSpec((tb, Wd), lambda b, t: (b * nt + t, 0)),
        scratch_shapes=[
            pltpu.VMEM((SUBLANES, 3 * Wd), F32),
            pltpu.VMEM((GDN_HEADS, GDN_HEAD_DIM, GDN_HEAD_DIM), F32),
            pltpu.VMEM((tb + SUBLANES, 3 * Wd), F32),
            pltpu.VMEM((tb, Wd), F32), pltpu.VMEM((tb, Wd), F32), pltpu.VMEM((tb, Wd), F32),
            pltpu.VMEM((tb, LANES), F32), pltpu.VMEM((tb, LANES), F32), pltpu.VMEM((LANES, tb), F32),
            pltpu.VMEM((tb, Wd), F32),
            pltpu.VMEM((nb, CHUNK, CHUNK), F32),
            pltpu.VMEM((nb, CHUNK, GDN_HEAD_DIM), F32),
            pltpu.VMEM((nb, 2 * CHUNK, GDN_HEAD_DIM), F32),
            pltpu.VMEM((nb, CHUNK, GDN_HEAD_DIM), F32),
        ],
        compiler_params=_cparams(("parallel", "arbitrary")),
        name="gdn",
    )(p_gdn, p_rw, *params)


def _swa_kernel(ps_ref, pos_ref, freq_ref, sgn_lo_ref, sgn_hi_ref, qn_ref, kn_ref, sink_ref, ones_ref, o_ref,
                kprev, vprev):
    Wn = SWA_WINDOW
    hd = SWA_HEAD_DIM
    G = SWA_Q_HEADS // SWA_KV_HEADS
    QW, KW = SWA_WIDTH, SWA_KV_WIDTH
    n = pl.program_id(1)

    @pl.when(n == 0)
    def _():
        kprev[...] = jnp.zeros_like(kprev)
        vprev[...] = jnp.zeros_like(vprev)

    half = ROPE_DIM // 2
    ang = pos_ref[...] * freq_ref[...]
    cs1, sn1 = jnp.cos(ang), jnp.sin(ang)
    lo1, hi1 = sn1 * sgn_lo_ref[...], sn1 * sgn_hi_ref[...]

    def norm_rope(x, gain_row, width):
        rep = lambda t: jnp.concatenate([t] * (width // LANES), axis=1)
        ms = _mm(x * x, ones_ref[0:width, 0:width]) * (1.0 / hd)
        y = x * lax.rsqrt(ms + NORM_EPS) * gain_row
        up = pltpu.roll(y, width - half, 1)
        dn = pltpu.roll(y, half, 1)
        return y * rep(cs1) + up * rep(lo1) + dn * rep(hi1)

    q = norm_rope(ps_ref[:, 0:QW], qn_ref[...], QW) * (hd ** -0.5)
    k = norm_rope(ps_ref[:, QW:QW + KW], kn_ref[...], KW)
    v = ps_ref[:, QW + KW:QW + 2 * KW]

    qi = _iota((G * Wn, 2 * Wn), 0) % Wn
    kj = _iota((G * Wn, 2 * Wn), 1)
    rel = qi + Wn - kj
    allowed = (rel >= 0) & (rel < SWA_WINDOW) & ((kj >= Wn) | (n > 0))
    grp = _iota((G * Wn, 1), 0) // Wn
    sinks = sink_ref[...]
    first_half = _iota((G * Wn, LANES), 1) < hd
    side_mask = (first_half.astype(F32), 1.0 - first_half.astype(F32))
    ones_kv = jnp.ones((2 * Wn, LANES), BF16)

    for pair in range(SWA_KV_HEADS // 2):
        kl = slice(pair * LANES, (pair + 1) * LANES)
        kband = jnp.concatenate([kprev[:, kl], k[:, kl]], axis=0).astype(BF16)
        vband = jnp.concatenate([vprev[:, kl], v[:, kl]], axis=0).astype(BF16)
        qs = jnp.concatenate([q[:, (pair * G + t) * LANES:(pair * G + t + 1) * LANES] for t in range(G)], axis=0)
        outs = []
        for side in range(2):
            s = jnp.where(allowed, _mm_nt(qs * side_mask[side], kband), -jnp.inf)
            sink = jnp.zeros((G * Wn, 1), F32)
            for t in range(G):
                h = SWA_Q_ORDER[(pair * G + t) * 2 + side]
                sink = jnp.where(grp == t, sinks[:, h:h + 1], sink)
            mx = jnp.maximum(jnp.max(s, axis=-1, keepdims=True), sink)
            p = jnp.exp(s - mx).astype(BF16)
            den = jnp.dot(p, ones_kv, preferred_element_type=F32) + jnp.exp(sink - mx)
            outs.append(jnp.dot(p, vband, preferred_element_type=F32) / den)
        o = jnp.where(first_half, outs[0], outs[1])
        for t in range(G):
            o_ref[:, (pair * G + t) * LANES:(pair * G + t + 1) * LANES] = o[t * Wn:(t + 1) * Wn]

    kprev[...] = k
    vprev[...] = v


def _swa_mixer(p_swa, pos_col, prm, bsz, seq):
    m = p_swa.shape[0]
    Wn = SWA_WINDOW
    nb = seq // Wn
    head_ones = _head_ones(SWA_WIDTH, SWA_HEAD_DIM)
    params = [prm["freq"], prm["sgn_lo"], prm["sgn_hi"], prm["qn"], prm["kn"], prm["sinks"], head_ones]
    full = lambda a: pl.BlockSpec(a.shape, lambda b, t: (0,) * a.ndim)
    return pl.pallas_call(
        _swa_kernel,
        out_shape=jax.ShapeDtypeStruct((m, SWA_WIDTH), F32),
        grid=(bsz, nb),
        in_specs=[pl.BlockSpec((Wn, SWA_COLS), lambda b, t: (b * nb + t, 0)),
                  pl.BlockSpec((Wn, 1), lambda b, t: (b * nb + t, 0))] + [full(a) for a in params],
        out_specs=pl.BlockSpec((Wn, SWA_WIDTH), lambda b, t: (b * nb + t, 0)),
        scratch_shapes=[pltpu.VMEM((Wn, SWA_KV_WIDTH), F32), pltpu.VMEM((Wn, SWA_KV_WIDTH), F32)],
        compiler_params=_cparams(("parallel", "arbitrary")),
        name="swa",
    )(p_swa, pos_col, *params)


def _merge_kernel(x_ref, orw_ref, ogdn_ref, oswa_ref, g0_ref, g1_ref, g2_ref, wb0_ref, wb1_ref, wb2_ref,
                  wout_ref, o_ref, acc):
    j = pl.program_id(1)

    @pl.when(j == 0)
    def _():
        acc[...] = jnp.zeros_like(acc)

    d = lambda a_ref, w_ref: jnp.dot(a_ref[...].astype(BF16), w_ref[...], preferred_element_type=F32)
    merged = (_sigmoid(g0_ref[...]) * d(orw_ref, wb0_ref)
              + _sigmoid(g1_ref[...]) * d(ogdn_ref, wb1_ref)
              + _sigmoid(g2_ref[...]) * d(oswa_ref, wb2_ref))
    acc[...] += jnp.dot(merged.astype(BF16), wout_ref[...], preferred_element_type=F32)

    @pl.when(j == pl.num_programs(1) - 1)
    def _():
        o_ref[...] = x_ref[...] + acc[...]


def _merge(x2, o_rw, o_gdn, o_swa, p_gate, wb_rw, wb_gdn, wb_swa, w_out):
    m, d = x2.shape
    tm = _pick_tile(m, (512, 256, 128, 64, 32, 16, 8))
    tn = 512
    nj = d // tn
    rows = lambda width: pl.BlockSpec((tm, width), lambda i, j: (i, 0))
    gate = lambda b: pl.BlockSpec((tm, tn), lambda i, j, b=b: (i, b * nj + j))
    wcol = lambda kdim: pl.BlockSpec((kdim, tn), lambda i, j: (0, j))
    return pl.pallas_call(
        _merge_kernel,
        out_shape=jax.ShapeDtypeStruct((m, d), F32),
        grid=(m // tm, nj),
        in_specs=[rows(d), rows(RW_WIDTH), rows(GDN_WIDTH), rows(SWA_WIDTH), gate(0), gate(1), gate(2),
                  wcol(RW_WIDTH), wcol(GDN_WIDTH), wcol(SWA_WIDTH),
                  pl.BlockSpec((tn, d), lambda i, j: (j, 0))],
        out_specs=rows(d),
        scratch_shapes=[pltpu.VMEM((tm, d), F32)],
        compiler_params=_cparams(("parallel", "arbitrary")),
        name="merge",
    )(x2, o_rw, o_gdn, o_swa, p_gate, p_gate, p_gate, wb_rw, wb_gdn, wb_swa, w_out)


def _ffn_kernel(tm, seq, x_ref, halo_ref, gain_ref, wg_ref, wu_ref, cg_ref, cu_ref, wd_ref, o_ref,
                h_scr, u_scr, acc):
    i = pl.program_id(0)
    j = pl.program_id(1)
    tn = wg_ref.shape[1]

    @pl.when(j == 0)
    def _():
        def norm(x):
            return (x * lax.rsqrt(jnp.mean(x * x, axis=-1, keepdims=True) + NORM_EPS) * gain_ref[...]).astype(BF16)
        seq_start = (i * tm) % seq == 0
        halo = jnp.where(seq_start, 0.0, halo_ref[...])
        h_scr[0:2 * SUBLANES, :] = norm(jnp.concatenate([jnp.zeros_like(halo), halo], axis=0))
        h_scr[2 * SUBLANES:, :] = norm(x_ref[...])
        acc[...] = jnp.zeros_like(acc)

    h = h_scr[...]
    u_scr[:, 0:tn] = jnp.dot(h, wg_ref[...], preferred_element_type=F32)
    u_scr[:, tn:2 * tn] = jnp.dot(h, wu_ref[...], preferred_element_type=F32)

    def conv(c0, cw_ref):
        out = jnp.zeros((tm, tn), F32)
        for kk in range(FFN_CONV):
            off = 2 * SUBLANES - (FFN_CONV - 1) + kk
            out = out + u_scr[pl.ds(off, tm), c0:c0 + tn] * cw_ref[kk:kk + 1, :]
        return out

    gate = conv(0, cg_ref)
    up = conv(tn, cu_ref)
    act = gate * _sigmoid(gate) * up
    acc[...] += jnp.dot(act.astype(BF16), wd_ref[...], preferred_element_type=F32)

    @pl.when(j == pl.num_programs(1) - 1)
    def _():
        o_ref[...] = x_ref[...] + acc[...]


def _conv_ffn(x2, gain, w_gate, w_up, c_gate, c_up, w_down, seq):
    m, d = x2.shape
    dff = w_gate.shape[1]
    tm = _pick_tile(seq, (512, 256, 128, 64, 32, 16))
    tn = _pick_tile(dff, (512, 256, 128))
    nj = dff // tn
    hb = tm // SUBLANES
    return pl.pallas_call(
        functools.partial(_ffn_kernel, tm, seq),
        out_shape=jax.ShapeDtypeStruct((m, d), F32),
        grid=(m // tm, nj),
        in_specs=[
            pl.BlockSpec((tm, d), lambda i, j: (i, 0)),
            pl.BlockSpec((SUBLANES, d), lambda i, j: (jnp.maximum(i * hb - 1, 0), 0)),
            pl.BlockSpec((1, d), lambda i, j: (0, 0)),
            pl.BlockSpec((d, tn), lambda i, j: (0, j)),
            pl.BlockSpec((d, tn), lambda i, j: (0, j)),
            pl.BlockSpec((FFN_CONV, tn), lambda i, j: (0, j)),
            pl.BlockSpec((FFN_CONV, tn), lambda i, j: (0, j)),
            pl.BlockSpec((tn, d), lambda i, j: (j, 0)),
        ],
        out_specs=pl.BlockSpec((tm, d), lambda i, j: (i, 0)),
        scratch_shapes=[pltpu.VMEM((tm + 2 * SUBLANES, d), BF16),
                        pltpu.VMEM((tm + 2 * SUBLANES, 2 * tn), F32),
                        pltpu.VMEM((tm, d), F32)],
        compiler_params=_cparams(("parallel", "arbitrary")),
        name="conv_ffn",
    )(x2, x2, gain.reshape(1, d), w_gate, w_up, c_gate, c_up, w_down)


def _row(a):
    return a.reshape(1, -1).astype(F32)


def _head_ones(width, head_dim):
    idx = np.arange(width) // head_dim
    return jnp.asarray(idx[:, None] == idx[None, :], BF16)


def _swa_head_order(a, axis):
    hd = SWA_HEAD_DIM
    take = lambda h: lax.slice_in_dim(a, h * hd, (h + 1) * hd, axis=axis)
    return jnp.concatenate([take(h) for h in SWA_Q_ORDER], axis=axis)


def _pad_cols(a, width):
    return jnp.pad(a, ((0, 0), (0, width - a.shape[1])))


def _layer_params(l, w_in, rw_mu, rw_w0, rw_w_up, rw_a0, rw_a_up, rw_g_up, rw_k_k, rw_k_a, rw_r_k, rw_ln_w,
                  rw_ln_b, vres_down, vres_mu, vres_v0, vres_v_up, gdn_conv, gdn_a_log, gdn_dt_bias, gdn_o_norm,
                  swa_q_norm, swa_k_norm, swa_sinks):
    d = w_in.shape[1]
    c_gdn = RW_COLS
    c_ba = c_gdn + 4 * GDN_WIDTH
    c_swa = c_ba + 2 * GDN_HEADS
    c_gate = c_swa + SWA_COLS
    has_vres = l > 0
    vres_w = vres_down[l - 1] if has_vres else jnp.zeros((d, RW_VRES_RANK), F32)
    assert c_ba % LANES == 0
    ba_w = _pack_cols(w_in, l, c_ba, LANES)[:, :2 * GDN_HEADS]
    slab_w = _pad_cols(jnp.concatenate([vres_w.astype(BF16), ba_w], axis=1), LANES)
    w_rw = jnp.concatenate([_pack_cols(w_in, l, 0, RW_COLS), slab_w], axis=1)
    w_gdn = _pack_cols(w_in, l, c_gdn, 4 * GDN_WIDTH)
    w_swa = _pack_cols(w_in, l, c_swa, SWA_COLS)
    w_swa = jnp.concatenate([_swa_head_order(w_swa[:, :SWA_WIDTH], axis=1), w_swa[:, SWA_WIDTH:]], axis=1)
    w_gate = _pack_cols(w_in, l, c_gate, w_in.shape[2] - c_gate)

    W = RW_WIDTH
    mu = jnp.concatenate([rw_mu[l], vres_mu[l - 1] if has_vres else jnp.zeros((RW_VRES_RANK,), F32),
                          jnp.zeros((LANES - RW_VRES_RANK,), F32)])
    wwa = jnp.zeros((LANES, 2 * W), F32)
    wwa = wwa.at[:RW_DECAY_RANK, :W].set(rw_w_up[l]).at[RW_DECAY_RANK:, W:].set(rw_a_up[l])
    rw = dict(mu=_row(mu), w0=_row(rw_w0[l]), a0=_row(rw_a0[l]), wwa=wwa.astype(BF16),
              gup=rw_g_up[l].astype(BF16), kk=_row(rw_k_k[l]), ka=_row(rw_k_a[l]), rk=_row(rw_r_k[l]),
              lnw=_row(rw_ln_w[l]), lnb=_row(rw_ln_b[l]), hones=_head_ones(W, RW_HEAD_DIM))
    if has_vres:
        vup = jnp.zeros((LANES, W), F32).at[:RW_VRES_RANK].set(vres_v_up[l - 1])
        rw.update(v0=_row(vres_v0[l - 1]), vup=vup.astype(BF16))

    lanes_pad = lambda a, off: jnp.zeros((1, LANES), F32).at[0, off:off + a.shape[0]].set(a)
    gdn = dict(cw=gdn_conv[l].astype(F32), alog=lanes_pad(gdn_a_log[l], SLAB_ALPHA0),
               dtb=lanes_pad(gdn_dt_bias[l], SLAB_ALPHA0), onorm=_row(gdn_o_norm[l]))

    half = ROPE_DIM // 2
    lane = np.arange(LANES) % SWA_HEAD_DIM
    inv_freq = ROPE_THETA ** (-(lane % half).astype(np.float32) * 2.0 / ROPE_DIM)
    freq = np.where(lane < ROPE_DIM, inv_freq, 0.0).astype(np.float32)
    sgn_lo = np.where(lane < half, -1.0, 0.0).astype(np.float32)
    sgn_hi = np.where((lane >= half) & (lane < ROPE_DIM), 1.0, 0.0).astype(np.float32)
    swa = dict(freq=jnp.asarray(freq).reshape(1, -1), sgn_lo=jnp.asarray(sgn_lo).reshape(1, -1),
               sgn_hi=jnp.asarray(sgn_hi).reshape(1, -1),
               qn=_row(jnp.tile(swa_q_norm[l], SWA_Q_HEADS)), kn=_row(jnp.tile(swa_k_norm[l], SWA_KV_HEADS)),
               sinks=lanes_pad(swa_sinks[l], 0))
    return dict(w_rw=w_rw, w_gdn=w_gdn, w_swa=w_swa, w_gate=w_gate, rw=rw, gdn=gdn, swa=swa, has_vres=has_vres)


def kernel(x, positions, norm_mix, w_in, rw_mu, rw_w0, rw_w_up, rw_a0, rw_a_up, rw_g_up, rw_k_k, rw_k_a, rw_r_k, rw_ln_w, rw_ln_b, vres_down, vres_mu, vres_v0, vres_v_up, gdn_conv, gdn_a_log, gdn_dt_bias, gdn_o_norm, swa_q_norm, swa_k_norm, swa_sinks, w_branch, w_out, norm_ffn, ffn_up, ffn_conv, ffn_down):
    bsz, seq, d = x.shape
    depth = w_in.shape[0]
    x2 = x.reshape(bsz * seq, d)
    pos_col = positions.astype(F32).reshape(bsz * seq, 1)
    dff = ffn_down.shape[1]
    v_first = None
    for l in range(depth):
        lp = _layer_params(l, w_in, rw_mu, rw_w0, rw_w_up, rw_a0, rw_a_up, rw_g_up, rw_k_k, rw_k_a, rw_r_k,
                           rw_ln_w, rw_ln_b, vres_down, vres_mu, vres_v0, vres_v_up, gdn_conv, gdn_a_log,
                           gdn_dt_bias, gdn_o_norm, swa_q_norm, swa_k_norm, swa_sinks)
        p_rw = _inproj(x2, norm_mix[l], lp["w_rw"], "inproj_rw")
        p_gdn = _inproj(x2, norm_mix[l], lp["w_gdn"], "inproj_gdn")
        p_swa = _inproj(x2, norm_mix[l], lp["w_swa"], "inproj_swa")
        p_gate = _inproj(x2, norm_mix[l], lp["w_gate"], "inproj_gate")
        if lp["has_vres"]:
            o_rw, _ = _rwkv_mixer(p_rw, v_first, lp["rw"], bsz, seq, True)
        else:
            o_rw, v_first = _rwkv_mixer(p_rw, None, lp["rw"], bsz, seq, False)
        o_gdn = _gdn_mixer(p_gdn, p_rw, lp["gdn"], bsz, seq)
        o_swa = _swa_mixer(p_swa, pos_col, lp["swa"], bsz, seq)
        wb = w_branch[l].astype(BF16)
        x2 = _merge(x2, o_rw, o_gdn, o_swa, p_gate, wb[:RW_WIDTH], wb[RW_WIDTH:RW_WIDTH + GDN_WIDTH],
                    _swa_head_order(wb[RW_WIDTH + GDN_WIDTH:], axis=0), w_out[l].astype(BF16))
        up = ffn_up[l].astype(BF16)
        x2 = _conv_ffn(x2, norm_ffn[l], up[:, :dff], up[:, dff:], ffn_conv[l][:, :dff], ffn_conv[l][:, dff:],
                       ffn_down[l].astype(BF16), seq)
    return x2.reshape(bsz, seq, d)
```

```python
import functools

import jax
import jax.numpy as jnp
import numpy as np
from jax import lax
from jax.experimental import pallas as pl
from jax.experimental.pallas import tpu as pltpu

F32 = jnp.float32
BF16 = jnp.bfloat16

D_MODEL = 2048
RW_HEADS, RW_HEAD_DIM = 8, 64
RW_WIDTH = RW_HEADS * RW_HEAD_DIM
RW_DECAY_RANK, RW_ICLR_RANK, RW_GATE_RANK, RW_VRES_RANK = 64, 64, 128, 32
RW_GN_EPS = 64e-5
RW_COLS = 3 * RW_WIDTH + RW_DECAY_RANK + RW_ICLR_RANK + RW_GATE_RANK
GDN_HEADS, GDN_HEAD_DIM = 6, 128
GDN_WIDTH = GDN_HEADS * GDN_HEAD_DIM
GDN_CONV = 4
SWA_Q_HEADS, SWA_KV_HEADS, SWA_HEAD_DIM = 12, 4, 64
SWA_WIDTH = SWA_Q_HEADS * SWA_HEAD_DIM
SWA_KV_WIDTH = SWA_KV_HEADS * SWA_HEAD_DIM
SWA_WINDOW = 128
SWA_COLS = SWA_WIDTH + 2 * SWA_KV_WIDTH
ROPE_DIM = SWA_HEAD_DIM // 4
ROPE_THETA = 500000.0
D_FF = 5632
FFN_CONV = 3
NORM_EPS = 1e-6

LANES = 128
SUBLANES = 8
VMEM_LIMIT_BYTES = 56 * 1024 * 1024

CHUNK = 64
INV_BLOCK = 16
SLAB_VRES0 = 0
SLAB_BETA0 = RW_VRES_RANK
SLAB_ALPHA0 = RW_VRES_RANK + GDN_HEADS
RW_GROUP = RW_COLS + LANES
_G = SWA_Q_HEADS // SWA_KV_HEADS
SWA_Q_ORDER = tuple((2 * p + side) * _G + t for p in range(SWA_KV_HEADS // 2) for t in range(_G) for side in range(2))

def _cparams(sem):
    return pltpu.CompilerParams(dimension_semantics=sem, vmem_limit_bytes=VMEM_LIMIT_BYTES)


def _dg(a, b, dims):
    return lax.dot_general(a.astype(BF16), b.astype(BF16), dims, preferred_element_type=F32)


def _mm(a, b):
    return _dg(a, b, (((1,), (0,)), ((), ())))


def _mm_nt(a, b):
    return _dg(a, b, (((1,), (1,)), ((), ())))


def _mm_tn(a, b):
    return _dg(a, b, (((0,), (0,)), ((), ())))


def _bmm(a, b):
    return _dg(a, b, (((2,), (1,)), ((0,), (0,))))


def _bmm_nt(a, b):
    return _dg(a, b, (((2,), (2,)), ((0,), (0,))))


def _bmm_tn(a, b):
    return _dg(a, b, (((1,), (1,)), ((0,), (0,))))


def _split3(a):
    hi = a.astype(BF16)
    r1 = a - hi.astype(F32)
    mid = r1.astype(BF16)
    lo = (r1 - mid.astype(F32)).astype(BF16)
    return hi, mid, lo


def _mm_exact_lhs(a_bf16, b):
    hi, mid, lo = _split3(b)
    d = lambda t: jnp.dot(a_bf16, t, preferred_element_type=F32)
    return d(hi) + d(mid) + d(lo)


def _mm_tn_exact_rhs(a, b_bf16):
    hi, mid, lo = _split3(a)
    d = lambda t: _mm_tn(t, b_bf16)
    return d(hi) + d(mid) + d(lo)


def _iota(shape, dim):
    return lax.broadcasted_iota(jnp.int32, shape, dim)


def _sigmoid(x):
    return 0.5 * jnp.tanh(0.5 * x) + 0.5


def _neumann_inverse(lmat):
    assert CHUNK // INV_BLOCK == 4
    n = lmat.shape[-1]
    row, col = _iota((1, n, n), 1), _iota((1, n, n), 2)
    eye = (row == col).astype(F32)
    in_blk = (row // INV_BLOCK) == (col // INV_BLOCK)
    lb = jnp.where(in_blk, lmat, 0.0)
    e = lmat - lb
    inv = eye + lb
    p = _bmm(lb, lb)
    steps = INV_BLOCK.bit_length() - 2
    for s in range(steps):
        inv = inv + _bmm(inv, p)
        if s + 1 < steps:
            p = _bmm(p, p)
    nmat = _bmm(inv, e)
    n2 = _bmm(nmat, nmat)
    m = eye + nmat + n2 + _bmm(nmat, n2)
    return _bmm(m, inv)


def _pack_kernel(shift, *refs):
    if shift == 0:
        a_ref, o_ref = refs
        o_ref[...] = a_ref[...].astype(BF16)
    else:
        a_ref, b_ref, o_ref = refs
        lane = _iota(a_ref.shape, 1)
        moved_a = pltpu.roll(a_ref[...], LANES - shift, 1)
        moved_b = pltpu.roll(b_ref[...], LANES - shift, 1)
        o_ref[...] = jnp.where(lane < LANES - shift, moved_a, moved_b).astype(BF16)


def _pack_cols(w3, layer, start, width):
    _, d, n = w3.shape
    q, shift = divmod(start, LANES)
    last_blk = pl.cdiv(n, LANES) - 1
    src = lambda off: pl.BlockSpec((None, d, LANES), lambda t: (layer, 0, jnp.minimum(q + t + off, last_blk)))
    ins = [w3] if shift == 0 else [w3, w3]
    return pl.pallas_call(
        functools.partial(_pack_kernel, shift),
        out_shape=jax.ShapeDtypeStruct((d, width), BF16),
        grid=(width // LANES,),
        in_specs=[src(0)] if shift == 0 else [src(0), src(1)],
        out_specs=pl.BlockSpec((d, LANES), lambda t: (0, t)),
        compiler_params=_cparams(("arbitrary",)),
        name="pack_cols",
    )(*ins)


INPROJ_TN = 512


def _inproj_kernel(starts, x_ref, g_ref, w_ref, *rest):
    outs, h_scr = rest[:-1], rest[-1]
    j = pl.program_id(1)

    @pl.when(j == 0)
    def _():
        x = x_ref[...]
        y = x * lax.rsqrt(jnp.mean(x * x, axis=-1, keepdims=True) + NORM_EPS) * g_ref[...]
        h_scr[...] = y.astype(BF16)

    for k, o_ref in enumerate(outs):
        @pl.when((j >= starts[k]) & (j < starts[k + 1]))
        def _(o_ref=o_ref):
            o_ref[...] = jnp.dot(h_scr[...], w_ref[...], preferred_element_type=F32)


def _pick_tile(n, prefs):
    for t in prefs:
        if n % t == 0:
            return t
    return n


def _inproj(x2, gain, w_bf16, widths):
    m, d = x2.shape
    tn = INPROJ_TN
    assert sum(widths) == w_bf16.shape[1] and all(w % tn == 0 for w in widths)
    tm = _pick_tile(m, (1024, 512, 256, 128, 64, 32, 16, 8))
    starts = tuple(int(s) for s in np.cumsum((0,) + tuple(w // tn for w in widths)))
    out_spec = lambda k: pl.BlockSpec(
        (tm, tn), lambda i, j: (i, jnp.clip(j - starts[k], 0, starts[k + 1] - starts[k] - 1)))
    return pl.pallas_call(
        functools.partial(_inproj_kernel, starts),
        out_shape=tuple(jax.ShapeDtypeStruct((m, w), F32) for w in widths),
        grid=(m // tm, starts[-1]),
        in_specs=[
            pl.BlockSpec((tm, d), lambda i, j: (i, 0)),
            pl.BlockSpec((1, d), lambda i, j: (0, 0)),
            pl.BlockSpec((d, tn), lambda i, j: (0, j)),
        ],
        out_specs=tuple(out_spec(k) for k in range(len(widths))),
        scratch_shapes=[pltpu.VMEM((tm, d), BF16)],
        compiler_params=_cparams(("parallel", "arbitrary")),
        name="inproj",
    )(x2, gain.reshape(1, d), w_bf16)


def _rwkv_kernel(has_vres, tb, *refs):
    scr = refs[-14:]
    (carry, state, pl_scr, at_scr, rt_scr, bt_scr, kt_scr, v_scr, y_scr,
     t_scr, rb_scr, akv_scr, rkv_scr, btk_scr) = scr
    if has_vres:
        (p_ref, vf_ref, mu_ref, w0_ref, a0_ref, wwa_ref, gup_ref, kk_ref, ka_ref, rk_ref, lnw_ref, lnb_ref,
         hones_ref, v0_ref, vup_ref, o_ref) = refs[:-14]
    else:
        (p_ref, mu_ref, w0_ref, a0_ref, wwa_ref, gup_ref, kk_ref, ka_ref, rk_ref, lnw_ref, lnb_ref,
         hones_ref, o_ref, vout_ref) = refs[:-14]
    W = RW_WIDTH
    tstep = pl.program_id(1)

    @pl.when(tstep == 0)
    def _():
        carry[...] = jnp.zeros_like(carry)
        state[...] = jnp.zeros_like(state)

    row0 = _iota((tb, LANES), 0) == 0
    for c0 in range(0, RW_GROUP, LANES):
        x = p_ref[:, c0:c0 + LANES]
        prev = pltpu.roll(x, 1, 0)
        prev = jnp.where(row0, carry[SUBLANES - 1:SUBLANES, c0:c0 + LANES], prev)
        pl_scr[:, c0:c0 + LANES] = x + (prev - x) * mu_ref[:, c0:c0 + LANES]
    carry[...] = p_ref[tb - SUBLANES:tb, :]

    head_sum = lambda t: _mm(t, hones_ref[...])

    c_wd = 3 * W
    x128 = pl_scr[:, c_wd:c_wd + LANES]
    lane = _iota((tb, LANES), 1)
    xin = jnp.where(lane < RW_DECAY_RANK, jnp.tanh(x128), x128)
    wa = jnp.dot(xin.astype(BF16), wwa_ref[...], preferred_element_type=F32)
    w_log = -jax.nn.softplus(-(w0_ref[...] + wa[:, :W])) - 0.5
    lw = -jnp.exp(w_log)
    asig = _sigmoid(a0_ref[...] + wa[:, W:])
    gd = pl_scr[:, c_wd + LANES:c_wd + 2 * LANES]
    g = jnp.dot(_sigmoid(gd).astype(BF16), gup_ref[...], preferred_element_type=F32)

    v = pl_scr[:, 2 * W:3 * W]
    if has_vres:
        vd = pl_scr[:, RW_COLS:RW_COLS + LANES]
        vl = jnp.dot(vd.astype(BF16), vup_ref[...], preferred_element_type=F32)
        v = v + (vf_ref[...] - v) * _sigmoid(v0_ref[...] + vl)
    else:
        vout_ref[...] = v
    v_scr[...] = v

    k = pl_scr[:, W:2 * W]
    kkraw = k * kk_ref[...]
    ssq = head_sum(kkraw * kkraw)
    kk = kkraw * lax.rsqrt(ssq + 1e-12)
    kfin = k * (1.0 + (asig - 1.0) * ka_ref[...])
    r = pl_scr[:, 0:W]
    bonus = head_sum(r * kfin * rk_ref[...]) * v

    C = CHUNK
    P2 = 2 * C
    NC, NP = tb // C, RW_HEADS // 2
    tril_b = (_iota((NC, C, C), 1) >= _iota((NC, C, C), 2)).astype(BF16)
    cum = sum(_bmm(tril_b, t.reshape(NC, C, W)) for t in _split3(lw)).reshape(tb, W)
    e_pos = jnp.exp(cum)
    e_neg = jnp.exp(-cum)
    rt_scr[...] = r * e_pos
    kt_scr[...] = kfin * e_neg
    bt_scr[...] = (kk * asig) * e_neg
    at_scr[...] = -kk * jnp.exp(cum - lw)

    def to_b(ref):
        return jnp.concatenate([ref[c * C:(c + 1) * C, p * LANES:(p + 1) * LANES][None]
                                for c in range(NC) for p in range(NP)], axis=0)

    row, col = _iota((1, P2, P2), 1), _iota((1, P2, P2), 2)
    same_head = (row // C) == (col // C)
    strict = same_head & ((row % C) > (col % C))
    incl = same_head & ((row % C) >= (col % C))
    left = _iota((1, C, LANES), 2) < RW_HEAD_DIM
    leftf = left.astype(F32)
    rightf = 1.0 - leftf

    def sel(x3):
        return jnp.where(left, x3[:, :C], x3[:, C:])

    at_b, rt_b, bt_b, kt_b, vp_b = to_b(at_scr), to_b(rt_scr), to_b(bt_scr), to_b(kt_scr), to_b(v_scr)
    lhs = jnp.concatenate([at_b * leftf, rt_b * leftf, at_b * rightf, rt_b * rightf], axis=1)
    out_a = _bmm_nt(lhs, jnp.concatenate([bt_b, kt_b], axis=1))
    out_b = _bmm_nt(lhs, jnp.concatenate([kt_b, bt_b], axis=1))
    ab = jnp.where(strict, jnp.concatenate([out_a[:, 0:C], out_b[:, 2 * C:3 * C]], axis=1), 0.0)
    ak = jnp.where(strict, jnp.concatenate([out_b[:, 0:C], out_a[:, 2 * C:3 * C]], axis=1), 0.0)
    rb = jnp.where(incl, jnp.concatenate([out_a[:, C:2 * C], out_b[:, 3 * C:4 * C]], axis=1), 0.0)
    rk = jnp.where(incl, jnp.concatenate([out_b[:, C:2 * C], out_a[:, 3 * C:4 * C]], axis=1), 0.0)
    t_scr[...] = _neumann_inverse(ab)
    rb_scr[...] = rb
    vv = jnp.concatenate([vp_b, vp_b], axis=1)
    akv_scr[...] = sel(_bmm(ak, vv))
    rkv_scr[...] = sel(_bmm(rk, vv))
    pc_rows = jnp.concatenate([e_pos[(c + 1) * C - 1:(c + 1) * C, p * LANES:(p + 1) * LANES][None]
                               for c in range(NC) for p in range(NP)], axis=0)
    btk_scr[...] = jnp.concatenate([bt_b * pc_rows, kt_b * pc_rows], axis=1)

    for c in range(NC):
        items = slice(c * NP, (c + 1) * NP)
        s = state[...]
        a_c, r_c, v_c = at_b[items], rt_b[items], vp_b[items]
        ars = _bmm_nt(jnp.concatenate([a_c, r_c], axis=1), s)
        rhs = ars[:, :C] + akv_scr[items]
        u = sel(_bmm(t_scr[items], jnp.concatenate([rhs, rhs], axis=1)))
        y = ars[:, C:] + sel(_bmm(rb_scr[items], jnp.concatenate([u, u], axis=1))) + rkv_scr[items]
        s_new = s * pc_rows[items] + _bmm_tn(jnp.concatenate([u, v_c], axis=1), btk_scr[items])
        state[...] = jnp.where(same_head, s_new, 0.0)
        for p in range(NP):
            y_scr[c * C:(c + 1) * C, p * LANES:(p + 1) * LANES] = y[p]

    y = y_scr[...]
    inv_n = 1.0 / RW_HEAD_DIM
    y_hi = y.astype(BF16).astype(F32)
    mean = (head_sum(y_hi) + head_sum(y - y_hi)) * inv_n
    yc = y - mean
    var = head_sum(yc * yc) * inv_n
    yn = yc * lax.rsqrt(var + RW_GN_EPS) * lnw_ref[...] + lnb_ref[...]
    o_ref[...] = ((yn + bonus) * g).astype(o_ref.dtype)


def _rwkv_mixer(p_rw, v_first, prm, bsz, seq, has_vres):
    m = p_rw.shape[0]
    tb = _pick_tile(seq, (256, 128, 64))
    nt = seq // tb
    nb = (tb // CHUNK) * (RW_HEADS // 2)
    W = RW_WIDTH
    row_spec = lambda width: pl.BlockSpec((tb, width), lambda b, t: (b * nt + t, 0))
    full = lambda a: pl.BlockSpec(a.shape, lambda b, t: (0,) * a.ndim)
    names = (["mu", "w0", "a0", "wwa", "gup", "kk", "ka", "rk", "lnw", "lnb", "hones"]
             + (["v0", "vup"] if has_vres else []))
    params = [prm[n] for n in names]
    ins = [p_rw] + ([v_first] if has_vres else []) + params
    in_specs = [row_spec(RW_GROUP)] + ([row_spec(W)] if has_vres else []) + [full(a) for a in params]
    if has_vres:
        out_shape = jax.ShapeDtypeStruct((m, W), BF16)
        out_specs = row_spec(W)
    else:
        out_shape = (jax.ShapeDtypeStruct((m, W), BF16), jax.ShapeDtypeStruct((m, W), F32))
        out_specs = (row_spec(W), row_spec(W))
    scratch = [
        pltpu.VMEM((SUBLANES, RW_GROUP), F32),
        pltpu.VMEM((RW_HEADS // 2, LANES, LANES), F32),
        pltpu.VMEM((tb, RW_GROUP), F32),
    ] + [pltpu.VMEM((tb, W), F32) for _ in range(6)] + [
        pltpu.VMEM((nb, 2 * CHUNK, 2 * CHUNK), F32),
        pltpu.VMEM((nb, 2 * CHUNK, 2 * CHUNK), F32),
        pltpu.VMEM((nb, CHUNK, LANES), F32),
        pltpu.VMEM((nb, CHUNK, LANES), F32),
        pltpu.VMEM((nb, 2 * CHUNK, LANES), F32),
    ]
    res = pl.pallas_call(
        functools.partial(_rwkv_kernel, has_vres, tb),
        out_shape=out_shape,
        grid=(bsz, nt),
        in_specs=in_specs,
        out_specs=out_specs,
        scratch_shapes=scratch,
        compiler_params=_cparams(("parallel", "arbitrary")),
        name="rwkv7_vres" if has_vres else "rwkv7",
    )(*ins)
    if has_vres:
        return res, None
    return res


def _gdn_kernel(tb, pg_ref, slab_ref, cw_ref, alog_ref, dtb_ref, onorm_ref, o_ref,
                carry, state, ext, q_scr, k_scr, v_scr, sig_scr, gcol_scr, grow_scr, o_scr,
                intra_scr, u_scr, wq_scr, kt_scr):
    H, Dh, Wd = GDN_HEADS, GDN_HEAD_DIM, GDN_WIDTH
    QKV = 3 * Wd
    tstep = pl.program_id(1)

    @pl.when(tstep == 0)
    def _():
        carry[...] = jnp.zeros_like(carry)
        state[...] = jnp.zeros_like(state)

    ext[0:SUBLANES, :] = carry[...]
    ext[SUBLANES:, :] = pg_ref[:, 0:QKV]
    carry[...] = pg_ref[tb - SUBLANES:tb, 0:QKV]
    for j in range(QKV // Dh):
        ln = slice(j * Dh, (j + 1) * Dh)
        acc = jnp.zeros((tb, Dh), F32)
        for kk in range(GDN_CONV):
            off = SUBLANES - (GDN_CONV - 1) + kk
            acc = acc + ext[pl.ds(off, tb), ln] * cw_ref[kk:kk + 1, ln]
        act = acc * _sigmoid(acc)
        which, h = divmod(j, H)
        hl = slice(h * Dh, (h + 1) * Dh)
        if which == 0:
            nrm = lax.rsqrt(jnp.sum(act * act, axis=-1, keepdims=True) + 1e-6)
            q_scr[:, hl] = act * nrm * (Dh ** -0.5)
        elif which == 1:
            nrm = lax.rsqrt(jnp.sum(act * act, axis=-1, keepdims=True) + 1e-6)
            k_scr[:, hl] = act * nrm
        else:
            v_scr[:, hl] = act

    slab = slab_ref[...]
    sig_scr[...] = _sigmoid(slab)
    gsl = -jnp.exp(alog_ref[...]) * jax.nn.softplus(slab + dtb_ref[...])
    lane = _iota((tb, LANES), 1)
    gsl = jnp.where((lane >= SLAB_ALPHA0) & (lane < SLAB_ALPHA0 + H), gsl, 0.0)

    C = CHUNK
    NC = tb // C
    trow, tcol = _iota((tb, tb), 0), _iota((tb, tb), 1)
    same_chunk = (trow // C) == (tcol // C)
    gcol_scr[...] = _mm_exact_lhs((same_chunk & (trow >= tcol)).astype(BF16), gsl)
    grow_scr[...] = _mm_tn_exact_rhs(gsl, (same_chunk & (trow <= tcol)).astype(BF16))

    def items(fn):
        return jnp.concatenate([fn(c, h)[None] for c in range(NC) for h in range(H)], axis=0)

    rows = lambda c: slice(c * C, (c + 1) * C)
    head = lambda h: slice(h * Dh, (h + 1) * Dh)
    gc = items(lambda c, h: gcol_scr[rows(c), SLAB_ALPHA0 + h:SLAB_ALPHA0 + h + 1])
    gr = items(lambda c, h: grow_scr[SLAB_ALPHA0 + h:SLAB_ALPHA0 + h + 1, rows(c)])
    beta = items(lambda c, h: sig_scr[rows(c), SLAB_BETA0 + h:SLAB_BETA0 + h + 1])
    q_b = items(lambda c, h: q_scr[rows(c), head(h)])
    k_b = items(lambda c, h: k_scr[rows(c), head(h)])
    v_b = items(lambda c, h: v_scr[rows(c), head(h)])
    row, col = _iota((1, C, C), 1), _iota((1, C, C), 2)
    dec = jnp.exp(jnp.where(row >= col, gc - gr, -jnp.inf))
    kb = k_b * beta
    prod = _bmm_nt(jnp.concatenate([kb, q_b], axis=1), k_b)
    lmat = jnp.where(row > col, prod[:, :C] * dec, 0.0)
    intra_scr[...] = prod[:, C:] * dec
    nb = NC * H
    l2 = (-lmat).reshape(nb // 2, 2 * C, C)
    prow, pcol = _iota((1, 2 * C, 2 * C), 1), _iota((1, 2 * C, 2 * C), 2)
    lp = jnp.where((prow // C) == (pcol // C), jnp.concatenate([l2, l2], axis=2), 0.0)
    tp = _neumann_inverse(lp)
    eg = jnp.exp(gc)
    rhs = jnp.concatenate([v_b * beta, kb * eg], axis=2).reshape(nb // 2, 2 * C, 2 * Dh)
    sol = _bmm(tp, rhs).reshape(nb, C, 2 * Dh)
    u_scr[...] = sol[:, :, :Dh]
    wq_scr[...] = jnp.concatenate([sol[:, :, Dh:], q_b * eg], axis=1)
    g_last = gc[:, C - 1:C, :]
    kt_scr[...] = k_b * jnp.exp(g_last - gc)
    eg_last = jnp.exp(g_last)

    for c in range(NC):
        it = slice(c * H, (c + 1) * H)
        s = state[...]
        ws = _bmm(wq_scr[it], s)
        v_new = u_scr[it] - ws[:, :C]
        o = ws[:, C:] + _bmm(intra_scr[it], v_new)
        state[...] = s * eg_last[it] + _bmm_tn(kt_scr[it], v_new)
        for h in range(H):
            o_scr[rows(c), head(h)] = o[h]

    for h in range(H):
        hl = slice(h * Dh, (h + 1) * Dh)
        o = o_scr[:, hl]
        o = o * lax.rsqrt(jnp.mean(o * o, axis=-1, keepdims=True) + NORM_EPS) * onorm_ref[...]
        z = pg_ref[:, QKV + h * Dh:QKV + (h + 1) * Dh]
        o_ref[:, hl] = (o * (z * _sigmoid(z))).astype(o_ref.dtype)


def _gdn_mixer(p_gdn, p_rw, prm, bsz, seq):
    m = p_gdn.shape[0]
    tb = _pick_tile(seq, (256, 128))
    nt = seq // tb
    nb = (tb // CHUNK) * GDN_HEADS
    Wd = GDN_WIDTH
    params = [prm["cw"], prm["alog"], prm["dtb"], prm["onorm"]]
    full = lambda a: pl.BlockSpec(a.shape, lambda b, t: (0,) * a.ndim)
    slab_blk = RW_COLS // LANES
    return pl.pallas_call(
        functools.partial(_gdn_kernel, tb),
        out_shape=jax.ShapeDtypeStruct((m, Wd), BF16),
        grid=(bsz, nt),
        in_specs=[pl.BlockSpec((tb, 4 * Wd), lambda b, t: (b * nt + t, 0)),
                  pl.BlockSpec((tb, LANES), lambda b, t: (b * nt + t, slab_blk))] + [full(a) for a in params],
        out_specs=pl.BlockSpec((tb, Wd), lambda b, t: (b * nt + t, 0)),
        scratch_shapes=[
            pltpu.VMEM((SUBLANES, 3 * Wd), F32),
            pltpu.VMEM((GDN_HEADS, GDN_HEAD_DIM, GDN_HEAD_DIM), F32),
            pltpu.VMEM((tb + SUBLANES, 3 * Wd), F32),
            pltpu.VMEM((tb, Wd), F32), pltpu.VMEM((tb, Wd), F32), pltpu.VMEM((tb, Wd), F32),
            pltpu.VMEM((tb, LANES), F32), pltpu.VMEM((tb, LANES), F32), pltpu.VMEM((LANES, tb), F32),
            pltpu.VMEM((tb, Wd), F32),
            pltpu.VMEM((nb, CHUNK, CHUNK), F32),
            pltpu.VMEM((nb, CHUNK, GDN_HEAD_DIM), F32),
            pltpu.VMEM((nb, 2 * CHUNK, GDN_HEAD_DIM), F32),
            pltpu.VMEM((nb, CHUNK, GDN_HEAD_DIM), F32),
        ],
        compiler_params=_cparams(("parallel", "arbitrary")),
        name="gdn",
    )(p_gdn, p_rw, *params)


def _swa_kernel(ps_ref, pos_ref, freq_ref, sgn_lo_ref, sgn_hi_ref, qn_ref, kn_ref, sink_ref, ones_ref, o_ref,
                kprev, vprev):
    Wn = SWA_WINDOW
    hd = SWA_HEAD_DIM
    G = SWA_Q_HEADS // SWA_KV_HEADS
    QW, KW = SWA_WIDTH, SWA_KV_WIDTH
    n = pl.program_id(1)

    @pl.when(n == 0)
    def _():
        kprev[...] = jnp.zeros_like(kprev)
        vprev[...] = jnp.zeros_like(vprev)

    half = ROPE_DIM // 2
    ang = pos_ref[...] * freq_ref[...]
    cs1, sn1 = jnp.cos(ang), jnp.sin(ang)
    lo1, hi1 = sn1 * sgn_lo_ref[...], sn1 * sgn_hi_ref[...]

    def norm_rope(x, gain_row, width):
        rep = lambda t: jnp.concatenate([t] * (width // LANES), axis=1)
        ms = _mm(x * x, ones_ref[0:width, 0:width]) * (1.0 / hd)
        y = x * lax.rsqrt(ms + NORM_EPS) * gain_row
        up = pltpu.roll(y, width - half, 1)
        dn = pltpu.roll(y, half, 1)
        return y * rep(cs1) + up * rep(lo1) + dn * rep(hi1)

    q = norm_rope(ps_ref[:, 0:QW], qn_ref[...], QW) * (hd ** -0.5)
    k = norm_rope(ps_ref[:, QW:QW + KW], kn_ref[...], KW)
    v = ps_ref[:, QW + KW:QW + 2 * KW]

    qi = _iota((G * Wn, 2 * Wn), 0) % Wn
    kj = _iota((G * Wn, 2 * Wn), 1)
    rel = qi + Wn - kj
    allowed = (rel >= 0) & (rel < SWA_WINDOW) & ((kj >= Wn) | (n > 0))
    grp = _iota((G * Wn, 1), 0) // Wn
    sinks = sink_ref[...]
    first_half = _iota((G * Wn, LANES), 1) < hd
    side_mask = (first_half.astype(F32), 1.0 - first_half.astype(F32))
    ones_kv = jnp.ones((2 * Wn, LANES), BF16)

    for pair in range(SWA_KV_HEADS // 2):
        kl = slice(pair * LANES, (pair + 1) * LANES)
        kband = jnp.concatenate([kprev[:, kl], k[:, kl]], axis=0).astype(BF16)
        vband = jnp.concatenate([vprev[:, kl], v[:, kl]], axis=0).astype(BF16)
        qs = jnp.concatenate([q[:, (pair * G + t) * LANES:(pair * G + t + 1) * LANES] for t in range(G)], axis=0)
        outs = []
        for side in range(2):
            s = jnp.where(allowed, _mm_nt(qs * side_mask[side], kband), -jnp.inf)
            sink = jnp.zeros((G * Wn, 1), F32)
            for t in range(G):
                h = SWA_Q_ORDER[(pair * G + t) * 2 + side]
                sink = jnp.where(grp == t, sinks[:, h:h + 1], sink)
            mx = jnp.maximum(jnp.max(s, axis=-1, keepdims=True), sink)
            p = jnp.exp(s - mx).astype(BF16)
            den = jnp.dot(p, ones_kv, preferred_element_type=F32) + jnp.exp(sink - mx)
            outs.append(jnp.dot(p, vband, preferred_element_type=F32) / den)
        o = jnp.where(first_half, outs[0], outs[1])
        for t in range(G):
            o_ref[:, (pair * G + t) * LANES:(pair * G + t + 1) * LANES] = o[t * Wn:(t + 1) * Wn].astype(o_ref.dtype)

    kprev[...] = k
    vprev[...] = v


def _swa_mixer(p_swa, pos_col, prm, bsz, seq):
    m = p_swa.shape[0]
    Wn = SWA_WINDOW
    nb = seq // Wn
    head_ones = _head_ones(SWA_WIDTH, SWA_HEAD_DIM)
    params = [prm["freq"], prm["sgn_lo"], prm["sgn_hi"], prm["qn"], prm["kn"], prm["sinks"], head_ones]
    full = lambda a: pl.BlockSpec(a.shape, lambda b, t: (0,) * a.ndim)
    return pl.pallas_call(
        _swa_kernel,
        out_shape=jax.ShapeDtypeStruct((m, SWA_WIDTH), BF16),
        grid=(bsz, nb),
        in_specs=[pl.BlockSpec((Wn, SWA_COLS), lambda b, t: (b * nb + t, 0)),
                  pl.BlockSpec((Wn, 1), lambda b, t: (b * nb + t, 0))] + [full(a) for a in params],
        out_specs=pl.BlockSpec((Wn, SWA_WIDTH), lambda b, t: (b * nb + t, 0)),
        scratch_shapes=[pltpu.VMEM((Wn, SWA_KV_WIDTH), F32), pltpu.VMEM((Wn, SWA_KV_WIDTH), F32)],
        compiler_params=_cparams(("parallel", "arbitrary")),
        name="swa",
    )(p_swa, pos_col, *params)


def _merge_kernel(x_ref, gain_ref, orw_ref, ogdn_ref, oswa_ref, wg0_ref, wg1_ref, wg2_ref,
                  wb0_ref, wb1_ref, wb2_ref, wout_ref, o_ref, h_scr, acc):
    j = pl.program_id(1)

    @pl.when(j == 0)
    def _():
        x = x_ref[...]
        y = x * lax.rsqrt(jnp.mean(x * x, axis=-1, keepdims=True) + NORM_EPS) * gain_ref[...]
        h_scr[...] = y.astype(BF16)
        acc[...] = jnp.zeros_like(acc)

    h = h_scr[...]
    branch = lambda wg_ref, a_ref, wb_ref: (
        _sigmoid(jnp.dot(h, wg_ref[...], preferred_element_type=F32))
        * jnp.dot(a_ref[...], wb_ref[...], preferred_element_type=F32))
    merged = (branch(wg0_ref, orw_ref, wb0_ref) + branch(wg1_ref, ogdn_ref, wb1_ref)
              + branch(wg2_ref, oswa_ref, wb2_ref))
    acc[...] += jnp.dot(merged.astype(BF16), wout_ref[...], preferred_element_type=F32)

    @pl.when(j == pl.num_programs(1) - 1)
    def _():
        o_ref[...] = x_ref[...] + acc[...]


def _merge(x2, gain, o_rw, o_gdn, o_swa, w_gate, wb_rw, wb_gdn, wb_swa, w_out):
    m, d = x2.shape
    tm = _pick_tile(m, (512, 256, 128, 64, 32, 16, 8))
    tn = 512
    nj = d // tn
    rows = lambda width: pl.BlockSpec((tm, width), lambda i, j: (i, 0))
    gate = lambda b: pl.BlockSpec((d, tn), lambda i, j, b=b: (0, b * nj + j))
    wcol = lambda kdim: pl.BlockSpec((kdim, tn), lambda i, j: (0, j))
    return pl.pallas_call(
        _merge_kernel,
        out_shape=jax.ShapeDtypeStruct((m, d), F32),
        grid=(m // tm, nj),
        in_specs=[rows(d), pl.BlockSpec((1, d), lambda i, j: (0, 0)),
                  rows(RW_WIDTH), rows(GDN_WIDTH), rows(SWA_WIDTH), gate(0), gate(1), gate(2),
                  wcol(RW_WIDTH), wcol(GDN_WIDTH), wcol(SWA_WIDTH),
                  pl.BlockSpec((tn, d), lambda i, j: (j, 0))],
        out_specs=rows(d),
        scratch_shapes=[pltpu.VMEM((tm, d), BF16), pltpu.VMEM((tm, d), F32)],
        compiler_params=_cparams(("parallel", "arbitrary")),
        name="merge",
    )(x2, gain.reshape(1, d), o_rw, o_gdn, o_swa, w_gate, w_gate, w_gate, wb_rw, wb_gdn, wb_swa, w_out)


def _ffn_kernel(tm, seq, x_ref, halo_ref, gain_ref, wg_ref, wu_ref, cg_ref, cu_ref, wd_ref, o_ref,
                h_scr, u_scr, acc):
    i = pl.program_id(0)
    j = pl.program_id(1)
    tn = wg_ref.shape[1]

    @pl.when(j == 0)
    def _():
        def norm(x):
            return (x * lax.rsqrt(jnp.mean(x * x, axis=-1, keepdims=True) + NORM_EPS) * gain_ref[...]).astype(BF16)
        seq_start = (i * tm) % seq == 0
        halo = jnp.where(seq_start, 0.0, halo_ref[...])
        h_scr[0:2 * SUBLANES, :] = norm(jnp.concatenate([jnp.zeros_like(halo), halo], axis=0))
        h_scr[2 * SUBLANES:, :] = norm(x_ref[...])
        acc[...] = jnp.zeros_like(acc)

    h = h_scr[...]
    u_scr[:, 0:tn] = jnp.dot(h, wg_ref[...], preferred_element_type=F32)
    u_scr[:, tn:2 * tn] = jnp.dot(h, wu_ref[...], preferred_element_type=F32)

    def conv(c0, cw_ref):
        out = jnp.zeros((tm, tn), F32)
        for kk in range(FFN_CONV):
            off = 2 * SUBLANES - (FFN_CONV - 1) + kk
            out = out + u_scr[pl.ds(off, tm), c0:c0 + tn] * cw_ref[kk:kk + 1, :]
        return out

    gate = conv(0, cg_ref)
    up = conv(tn, cu_ref)
    act = gate * _sigmoid(gate) * up
    acc[...] += jnp.dot(act.astype(BF16), wd_ref[...], preferred_element_type=F32)

    @pl.when(j == pl.num_programs(1) - 1)
    def _():
        o_ref[...] = x_ref[...] + acc[...]


def _conv_ffn(x2, gain, w_gate, w_up, c_gate, c_up, w_down, seq):
    m, d = x2.shape
    dff = w_gate.shape[1]
    tm = _pick_tile(seq, (512, 256, 128, 64, 32, 16))
    tn = _pick_tile(dff, (512, 256, 128))
    nj = dff // tn
    hb = tm // SUBLANES
    return pl.pallas_call(
        functools.partial(_ffn_kernel, tm, seq),
        out_shape=jax.ShapeDtypeStruct((m, d), F32),
        grid=(m // tm, nj),
        in_specs=[
            pl.BlockSpec((tm, d), lambda i, j: (i, 0)),
            pl.BlockSpec((SUBLANES, d), lambda i, j: (jnp.maximum(i * hb - 1, 0), 0)),
            pl.BlockSpec((1, d), lambda i, j: (0, 0)),
            pl.BlockSpec((d, tn), lambda i, j: (0, j)),
            pl.BlockSpec((d, tn), lambda i, j: (0, j)),
            pl.BlockSpec((FFN_CONV, tn), lambda i, j: (0, j)),
            pl.BlockSpec((FFN_CONV, tn), lambda i, j: (0, j)),
            pl.BlockSpec((tn, d), lambda i, j: (j, 0)),
        ],
        out_specs=pl.BlockSpec((tm, d), lambda i, j: (i, 0)),
        scratch_shapes=[pltpu.VMEM((tm + 2 * SUBLANES, d), BF16),
                        pltpu.VMEM((tm + 2 * SUBLANES, 2 * tn), F32),
                        pltpu.VMEM((tm, d), F32)],
        compiler_params=_cparams(("parallel", "arbitrary")),
        name="conv_ffn",
    )(x2, x2, gain.reshape(1, d), w_gate, w_up, c_gate, c_up, w_down)


def _row(a):
    return a.reshape(1, -1).astype(F32)


def _head_ones(width, head_dim):
    idx = np.arange(width) // head_dim
    return jnp.asarray(idx[:, None] == idx[None, :], BF16)


def _swa_head_order(a, axis):
    hd = SWA_HEAD_DIM
    take = lambda h: lax.slice_in_dim(a, h * hd, (h + 1) * hd, axis=axis)
    return jnp.concatenate([take(h) for h in SWA_Q_ORDER], axis=axis)


def _pad_cols(a, width):
    return jnp.pad(a, ((0, 0), (0, width - a.shape[1])))


def _layer_params(l, w_in, rw_mu, rw_w0, rw_w_up, rw_a0, rw_a_up, rw_g_up, rw_k_k, rw_k_a, rw_r_k, rw_ln_w,
                  rw_ln_b, vres_down, vres_mu, vres_v0, vres_v_up, gdn_conv, gdn_a_log, gdn_dt_bias, gdn_o_norm,
                  swa_q_norm, swa_k_norm, swa_sinks):
    d = w_in.shape[1]
    c_gdn = RW_COLS
    c_ba = c_gdn + 4 * GDN_WIDTH
    c_swa = c_ba + 2 * GDN_HEADS
    c_gate = c_swa + SWA_COLS
    has_vres = l > 0
    vres_w = vres_down[l - 1] if has_vres else jnp.zeros((d, RW_VRES_RANK), F32)
    assert c_ba % LANES == 0
    ba_w = _pack_cols(w_in, l, c_ba, LANES)[:, :2 * GDN_HEADS]
    slab_w = _pad_cols(jnp.concatenate([vres_w.astype(BF16), ba_w], axis=1), LANES)
    w_rw = jnp.concatenate([_pack_cols(w_in, l, 0, RW_COLS), slab_w], axis=1)
    w_gdn = _pack_cols(w_in, l, c_gdn, 4 * GDN_WIDTH)
    w_swa = _pack_cols(w_in, l, c_swa, SWA_COLS)
    w_swa = jnp.concatenate([_swa_head_order(w_swa[:, :SWA_WIDTH], axis=1), w_swa[:, SWA_WIDTH:]], axis=1)
    w_gate = _pack_cols(w_in, l, c_gate, w_in.shape[2] - c_gate)

    W = RW_WIDTH
    mu = jnp.concatenate([rw_mu[l], vres_mu[l - 1] if has_vres else jnp.zeros((RW_VRES_RANK,), F32),
                          jnp.zeros((LANES - RW_VRES_RANK,), F32)])
    wwa = jnp.zeros((LANES, 2 * W), F32)
    wwa = wwa.at[:RW_DECAY_RANK, :W].set(rw_w_up[l]).at[RW_DECAY_RANK:, W:].set(rw_a_up[l])
    rw = dict(mu=_row(mu), w0=_row(rw_w0[l]), a0=_row(rw_a0[l]), wwa=wwa.astype(BF16),
              gup=rw_g_up[l].astype(BF16), kk=_row(rw_k_k[l]), ka=_row(rw_k_a[l]), rk=_row(rw_r_k[l]),
              lnw=_row(rw_ln_w[l]), lnb=_row(rw_ln_b[l]), hones=_head_ones(W, RW_HEAD_DIM))
    if has_vres:
        vup = jnp.zeros((LANES, W), F32).at[:RW_VRES_RANK].set(vres_v_up[l - 1])
        rw.update(v0=_row(vres_v0[l - 1]), vup=vup.astype(BF16))

    lanes_pad = lambda a, off: jnp.zeros((1, LANES), F32).at[0, off:off + a.shape[0]].set(a)
    gdn = dict(cw=gdn_conv[l].astype(F32), alog=lanes_pad(gdn_a_log[l], SLAB_ALPHA0),
               dtb=lanes_pad(gdn_dt_bias[l], SLAB_ALPHA0), onorm=_row(gdn_o_norm[l]))

    half = ROPE_DIM // 2
    lane = np.arange(LANES) % SWA_HEAD_DIM
    inv_freq = ROPE_THETA ** (-(lane % half).astype(np.float32) * 2.0 / ROPE_DIM)
    freq = np.where(lane < ROPE_DIM, inv_freq, 0.0).astype(np.float32)
    sgn_lo = np.where(lane < half, -1.0, 0.0).astype(np.float32)
    sgn_hi = np.where((lane >= half) & (lane < ROPE_DIM), 1.0, 0.0).astype(np.float32)
    swa = dict(freq=jnp.asarray(freq).reshape(1, -1), sgn_lo=jnp.asarray(sgn_lo).reshape(1, -1),
               sgn_hi=jnp.asarray(sgn_hi).reshape(1, -1),
               qn=_row(jnp.tile(swa_q_norm[l], SWA_Q_HEADS)), kn=_row(jnp.tile(swa_k_norm[l], SWA_KV_HEADS)),
               sinks=lanes_pad(swa_sinks[l], 0))
    groups = (w_rw, w_gdn, w_swa)
    mix_widths = tuple(pl.cdiv(g.shape[1], INPROJ_TN) * INPROJ_TN for g in groups)
    w_mix = jnp.concatenate([_pad_cols(g, wd) for g, wd in zip(groups, mix_widths)], axis=1)
    return dict(w_mix=w_mix, mix_widths=mix_widths, w_gate=w_gate, rw=rw, gdn=gdn, swa=swa, has_vres=has_vres)


def kernel(x, positions, norm_mix, w_in, rw_mu, rw_w0, rw_w_up, rw_a0, rw_a_up, rw_g_up, rw_k_k, rw_k_a, rw_r_k, rw_ln_w, rw_ln_b, vres_down, vres_mu, vres_v0, vres_v_up, gdn_conv, gdn_a_log, gdn_dt_bias, gdn_o_norm, swa_q_norm, swa_k_norm, swa_sinks, w_branch, w_out, norm_ffn, ffn_up, ffn_conv, ffn_down):
    bsz, seq, d = x.shape
    depth = w_in.shape[0]
    x2 = x.reshape(bsz * seq, d)
    pos_col = positions.astype(F32).reshape(bsz * seq, 1)
    dff = ffn_down.shape[1]
    v_first = None
    for l in range(depth):
        lp = _layer_params(l, w_in, rw_mu, rw_w0, rw_w_up, rw_a0, rw_a_up, rw_g_up, rw_k_k, rw_k_a, rw_r_k,
                           rw_ln_w, rw_ln_b, vres_down, vres_mu, vres_v0, vres_v_up, gdn_conv, gdn_a_log,
                           gdn_dt_bias, gdn_o_norm, swa_q_norm, swa_k_norm, swa_sinks)
        p_rw, p_gdn, p_swa = _inproj(x2, norm_mix[l], lp["w_mix"], lp["mix_widths"])
        if lp["has_vres"]:
            o_rw, _ = _rwkv_mixer(p_rw, v_first, lp["rw"], bsz, seq, True)
        else:
            o_rw, v_first = _rwkv_mixer(p_rw, None, lp["rw"], bsz, seq, False)
        o_gdn = _gdn_mixer(p_gdn, p_rw, lp["gdn"], bsz, seq)
        o_swa = _swa_mixer(p_swa, pos_col, lp["swa"], bsz, seq)
        wb = w_branch[l].astype(BF16)
        x2 = _merge(x2, norm_mix[l], o_rw, o_gdn, o_swa, lp["w_gate"], wb[:RW_WIDTH], wb[RW_WIDTH:RW_WIDTH + GDN_WIDTH],
                    _swa_head_order(wb[RW_WIDTH + GDN_WIDTH:], axis=0), w_out[l].astype(BF16))
        up = ffn_up[l].astype(BF16)
        x2 = _conv_ffn(x2, norm_ffn[l], up[:, :dff], up[:, dff:], ffn_conv[l][:, :dff], ffn_conv[l][:, dff:],
                       ffn_down[l].astype(BF16), seq)
    return x2.reshape(bsz, seq, d)
```

```python
import functools

import jax
import jax.numpy as jnp
import numpy as np
from jax import lax
from jax.experimental import pallas as pl
from jax.experimental.pallas import tpu as pltpu

F32 = jnp.float32
BF16 = jnp.bfloat16

D_MODEL = 2048
RW_HEADS, RW_HEAD_DIM = 8, 64
RW_WIDTH = RW_HEADS * RW_HEAD_DIM
RW_DECAY_RANK, RW_ICLR_RANK, RW_GATE_RANK, RW_VRES_RANK = 64, 64, 128, 32
RW_GN_EPS = 64e-5
RW_COLS = 3 * RW_WIDTH + RW_DECAY_RANK + RW_ICLR_RANK + RW_GATE_RANK
GDN_HEADS, GDN_HEAD_DIM = 6, 128
GDN_WIDTH = GDN_HEADS * GDN_HEAD_DIM
GDN_CONV = 4
SWA_Q_HEADS, SWA_KV_HEADS, SWA_HEAD_DIM = 12, 4, 64
SWA_WIDTH = SWA_Q_HEADS * SWA_HEAD_DIM
SWA_KV_WIDTH = SWA_KV_HEADS * SWA_HEAD_DIM
SWA_WINDOW = 128
SWA_COLS = SWA_WIDTH + 2 * SWA_KV_WIDTH
ROPE_DIM = SWA_HEAD_DIM // 4
ROPE_THETA = 500000.0
D_FF = 5632
FFN_CONV = 3
NORM_EPS = 1e-6

LANES = 128
SUBLANES = 8
VMEM_LIMIT_BYTES = 56 * 1024 * 1024

CHUNK = 64
INV_BLOCK = 16
SLAB_VRES0 = 0
SLAB_BETA0 = RW_VRES_RANK
SLAB_ALPHA0 = RW_VRES_RANK + GDN_HEADS
RW_GROUP = RW_COLS + LANES
_G = SWA_Q_HEADS // SWA_KV_HEADS
SWA_Q_ORDER = tuple((2 * p + side) * _G + t for p in range(SWA_KV_HEADS // 2) for t in range(_G) for side in range(2))

def _cparams(sem):
    return pltpu.CompilerParams(dimension_semantics=sem, vmem_limit_bytes=VMEM_LIMIT_BYTES)


def _dg(a, b, dims):
    return lax.dot_general(a.astype(BF16), b.astype(BF16), dims, preferred_element_type=F32)


def _mm(a, b):
    return _dg(a, b, (((1,), (0,)), ((), ())))


def _mm_nt(a, b):
    return _dg(a, b, (((1,), (1,)), ((), ())))


def _mm_tn(a, b):
    return _dg(a, b, (((0,), (0,)), ((), ())))


def _bmm(a, b):
    return _dg(a, b, (((2,), (1,)), ((0,), (0,))))


def _bmm_nt(a, b):
    return _dg(a, b, (((2,), (2,)), ((0,), (0,))))


def _bmm_tn(a, b):
    return _dg(a, b, (((1,), (1,)), ((0,), (0,))))


def _split3(a):
    hi = a.astype(BF16)
    r1 = a - hi.astype(F32)
    mid = r1.astype(BF16)
    lo = (r1 - mid.astype(F32)).astype(BF16)
    return hi, mid, lo


def _mm_exact_lhs(a_bf16, b):
    hi, mid, lo = _split3(b)
    d = lambda t: jnp.dot(a_bf16, t, preferred_element_type=F32)
    return d(hi) + d(mid) + d(lo)


def _mm_tn_exact_rhs(a, b_bf16):
    hi, mid, lo = _split3(a)
    d = lambda t: _mm_tn(t, b_bf16)
    return d(hi) + d(mid) + d(lo)


def _iota(shape, dim):
    return lax.broadcasted_iota(jnp.int32, shape, dim)


def _sigmoid(x):
    return 0.5 * jnp.tanh(0.5 * x) + 0.5


def _neumann_inverse(lmat):
    assert CHUNK // INV_BLOCK == 4
    n = lmat.shape[-1]
    row, col = _iota((1, n, n), 1), _iota((1, n, n), 2)
    eye = (row == col).astype(F32)
    in_blk = (row // INV_BLOCK) == (col // INV_BLOCK)
    lb = jnp.where(in_blk, lmat, 0.0)
    e = lmat - lb
    inv = eye + lb
    p = _bmm(lb, lb)
    steps = INV_BLOCK.bit_length() - 2
    for s in range(steps):
        inv = inv + _bmm(inv, p)
        if s + 1 < steps:
            p = _bmm(p, p)
    nmat = _bmm(inv, e)
    n2 = _bmm(nmat, nmat)
    m = eye + nmat + n2 + _bmm(nmat, n2)
    return _bmm(m, inv)


def _pack_kernel(shift, *refs):
    if shift == 0:
        a_ref, o_ref = refs
        o_ref[...] = a_ref[...].astype(BF16)
    else:
        a_ref, b_ref, o_ref = refs
        lane = _iota(a_ref.shape, 1)
        moved_a = pltpu.roll(a_ref[...], LANES - shift, 1)
        moved_b = pltpu.roll(b_ref[...], LANES - shift, 1)
        o_ref[...] = jnp.where(lane < LANES - shift, moved_a, moved_b).astype(BF16)


def _pack_cols(w3, layer, start, width):
    _, d, n = w3.shape
    q, shift = divmod(start, LANES)
    last_blk = pl.cdiv(n, LANES) - 1
    src = lambda off: pl.BlockSpec((None, d, LANES), lambda t: (layer, 0, jnp.minimum(q + t + off, last_blk)))
    ins = [w3] if shift == 0 else [w3, w3]
    return pl.pallas_call(
        functools.partial(_pack_kernel, shift),
        out_shape=jax.ShapeDtypeStruct((d, width), BF16),
        grid=(width // LANES,),
        in_specs=[src(0)] if shift == 0 else [src(0), src(1)],
        out_specs=pl.BlockSpec((d, LANES), lambda t: (0, t)),
        compiler_params=_cparams(("arbitrary",)),
        name="pack_cols",
    )(*ins)


INPROJ_TN = 512


def _inproj_kernel(starts, x_ref, g_ref, w_ref, *rest):
    outs, h_scr = rest[:-1], rest[-1]
    j = pl.program_id(1)

    @pl.when(j == 0)
    def _():
        x = x_ref[...]
        y = x * lax.rsqrt(jnp.mean(x * x, axis=-1, keepdims=True) + NORM_EPS) * g_ref[...]
        h_scr[...] = y.astype(BF16)

    for k, o_ref in enumerate(outs):
        @pl.when((j >= starts[k]) & (j < starts[k + 1]))
        def _(o_ref=o_ref):
            o_ref[...] = jnp.dot(h_scr[...], w_ref[...], preferred_element_type=F32)


def _pick_tile(n, prefs):
    for t in prefs:
        if n % t == 0:
            return t
    return n


def _inproj(x2, gain, w_bf16, widths):
    m, d = x2.shape
    tn = INPROJ_TN
    assert sum(widths) == w_bf16.shape[1] and all(w % tn == 0 for w in widths)
    tm = _pick_tile(m, (1024, 512, 256, 128, 64, 32, 16, 8))
    starts = tuple(int(s) for s in np.cumsum((0,) + tuple(w // tn for w in widths)))
    out_spec = lambda k: pl.BlockSpec(
        (tm, tn), lambda i, j: (i, jnp.clip(j - starts[k], 0, starts[k + 1] - starts[k] - 1)))
    return pl.pallas_call(
        functools.partial(_inproj_kernel, starts),
        out_shape=tuple(jax.ShapeDtypeStruct((m, w), F32) for w in widths),
        grid=(m // tm, starts[-1]),
        in_specs=[
            pl.BlockSpec((tm, d), lambda i, j: (i, 0)),
            pl.BlockSpec((1, d), lambda i, j: (0, 0)),
            pl.BlockSpec((d, tn), lambda i, j: (0, j)),
        ],
        out_specs=tuple(out_spec(k) for k in range(len(widths))),
        scratch_shapes=[pltpu.VMEM((tm, d), BF16)],
        compiler_params=_cparams(("parallel", "arbitrary")),
        name="inproj",
    )(x2, gain.reshape(1, d), w_bf16)


def _rwkv_kernel(has_vres, tb, *refs):
    scr = refs[-14:]
    (carry, state, pl_scr, at_scr, rt_scr, bt_scr, kt_scr, v_scr, y_scr,
     t_scr, rb_scr, akv_scr, rkv_scr, btk_scr) = scr
    if has_vres:
        (p_ref, vf_ref, mu_ref, w0_ref, a0_ref, wwa_ref, gup_ref, kk_ref, ka_ref, rk_ref, lnw_ref, lnb_ref,
         hones_ref, v0_ref, vup_ref, o_ref) = refs[:-14]
    else:
        (p_ref, mu_ref, w0_ref, a0_ref, wwa_ref, gup_ref, kk_ref, ka_ref, rk_ref, lnw_ref, lnb_ref,
         hones_ref, o_ref, vout_ref) = refs[:-14]
    W = RW_WIDTH
    tstep = pl.program_id(1)

    @pl.when(tstep == 0)
    def _():
        carry[...] = jnp.zeros_like(carry)
        state[...] = jnp.zeros_like(state)

    row0 = _iota((tb, LANES), 0) == 0
    for c0 in range(0, RW_GROUP, LANES):
        x = p_ref[:, c0:c0 + LANES]
        prev = pltpu.roll(x, 1, 0)
        prev = jnp.where(row0, carry[SUBLANES - 1:SUBLANES, c0:c0 + LANES], prev)
        pl_scr[:, c0:c0 + LANES] = x + (prev - x) * mu_ref[:, c0:c0 + LANES]
    carry[...] = p_ref[tb - SUBLANES:tb, :]

    head_sum = lambda t: _mm(t, hones_ref[...])

    c_wd = 3 * W
    x128 = pl_scr[:, c_wd:c_wd + LANES]
    lane = _iota((tb, LANES), 1)
    xin = jnp.where(lane < RW_DECAY_RANK, jnp.tanh(x128), x128)
    wa = jnp.dot(xin.astype(BF16), wwa_ref[...], preferred_element_type=F32)
    w_log = -jax.nn.softplus(-(w0_ref[...] + wa[:, :W])) - 0.5
    lw = -jnp.exp(w_log)
    asig = _sigmoid(a0_ref[...] + wa[:, W:])
    gd = pl_scr[:, c_wd + LANES:c_wd + 2 * LANES]
    g = jnp.dot(_sigmoid(gd).astype(BF16), gup_ref[...], preferred_element_type=F32)

    v = pl_scr[:, 2 * W:3 * W]
    if has_vres:
        vd = pl_scr[:, RW_COLS:RW_COLS + LANES]
        vl = jnp.dot(vd.astype(BF16), vup_ref[...], preferred_element_type=F32)
        v = v + (vf_ref[...] - v) * _sigmoid(v0_ref[...] + vl)
    else:
        vout_ref[...] = v
    v_scr[...] = v

    k = pl_scr[:, W:2 * W]
    kkraw = k * kk_ref[...]
    ssq = head_sum(kkraw * kkraw)
    kk = kkraw * lax.rsqrt(ssq + 1e-12)
    kfin = k * (1.0 + (asig - 1.0) * ka_ref[...])
    r = pl_scr[:, 0:W]
    bonus = head_sum(r * kfin * rk_ref[...]) * v

    C = CHUNK
    P2 = 2 * C
    NC, NP = tb // C, RW_HEADS // 2
    tril_b = (_iota((NC, C, C), 1) >= _iota((NC, C, C), 2)).astype(BF16)
    cum = sum(_bmm(tril_b, t.reshape(NC, C, W)) for t in _split3(lw)).reshape(tb, W)
    e_pos = jnp.exp(cum)
    e_neg = jnp.exp(-cum)
    rt_scr[...] = r * e_pos
    kt_scr[...] = kfin * e_neg
    bt_scr[...] = (kk * asig) * e_neg
    at_scr[...] = -kk * jnp.exp(cum - lw)

    def to_b(ref):
        return jnp.concatenate([ref[c * C:(c + 1) * C, p * LANES:(p + 1) * LANES][None]
                                for c in range(NC) for p in range(NP)], axis=0)

    row, col = _iota((1, P2, P2), 1), _iota((1, P2, P2), 2)
    same_head = (row // C) == (col // C)
    strict = same_head & ((row % C) > (col % C))
    incl = same_head & ((row % C) >= (col % C))
    left = _iota((1, C, LANES), 2) < RW_HEAD_DIM
    leftf = left.astype(F32)
    rightf = 1.0 - leftf

    def sel(x3):
        return jnp.where(left, x3[:, :C], x3[:, C:])

    at_b, rt_b, bt_b, kt_b, vp_b = to_b(at_scr), to_b(rt_scr), to_b(bt_scr), to_b(kt_scr), to_b(v_scr)
    lhs = jnp.concatenate([at_b * leftf, rt_b * leftf, at_b * rightf, rt_b * rightf], axis=1)
    out_a = _bmm_nt(lhs, jnp.concatenate([bt_b, kt_b], axis=1))
    out_b = _bmm_nt(lhs, jnp.concatenate([kt_b, bt_b], axis=1))
    ab = jnp.where(strict, jnp.concatenate([out_a[:, 0:C], out_b[:, 2 * C:3 * C]], axis=1), 0.0)
    ak = jnp.where(strict, jnp.concatenate([out_b[:, 0:C], out_a[:, 2 * C:3 * C]], axis=1), 0.0)
    rb = jnp.where(incl, jnp.concatenate([out_a[:, C:2 * C], out_b[:, 3 * C:4 * C]], axis=1), 0.0)
    rk = jnp.where(incl, jnp.concatenate([out_b[:, C:2 * C], out_a[:, 3 * C:4 * C]], axis=1), 0.0)
    t_scr[...] = _neumann_inverse(ab)
    rb_scr[...] = rb
    vv = jnp.concatenate([vp_b, vp_b], axis=1)
    akv_scr[...] = sel(_bmm(ak, vv))
    rkv_scr[...] = sel(_bmm(rk, vv))
    pc_rows = jnp.concatenate([e_pos[(c + 1) * C - 1:(c + 1) * C, p * LANES:(p + 1) * LANES][None]
                               for c in range(NC) for p in range(NP)], axis=0)
    btk_scr[...] = jnp.concatenate([bt_b * pc_rows, kt_b * pc_rows], axis=1)

    for c in range(NC):
        items = slice(c * NP, (c + 1) * NP)
        s = state[...]
        a_c, r_c, v_c = at_b[items], rt_b[items], vp_b[items]
        ars = _bmm_nt(jnp.concatenate([a_c, r_c], axis=1), s)
        rhs = ars[:, :C] + akv_scr[items]
        u = sel(_bmm(t_scr[items], jnp.concatenate([rhs, rhs], axis=1)))
        y = ars[:, C:] + sel(_bmm(rb_scr[items], jnp.concatenate([u, u], axis=1))) + rkv_scr[items]
        s_new = s * pc_rows[items] + _bmm_tn(jnp.concatenate([u, v_c], axis=1), btk_scr[items])
        state[...] = jnp.where(same_head, s_new, 0.0)
        for p in range(NP):
            y_scr[c * C:(c + 1) * C, p * LANES:(p + 1) * LANES] = y[p]

    y = y_scr[...]
    inv_n = 1.0 / RW_HEAD_DIM
    y_hi = y.astype(BF16).astype(F32)
    mean = (head_sum(y_hi) + head_sum(y - y_hi)) * inv_n
    yc = y - mean
    var = head_sum(yc * yc) * inv_n
    yn = yc * lax.rsqrt(var + RW_GN_EPS) * lnw_ref[...] + lnb_ref[...]
    o_ref[...] = ((yn + bonus) * g).astype(o_ref.dtype)


def _rwkv_mixer(p_rw, v_first, prm, bsz, seq, has_vres):
    m = p_rw.shape[0]
    tb = _pick_tile(seq, (256, 128, 64))
    nt = seq // tb
    nb = (tb // CHUNK) * (RW_HEADS // 2)
    W = RW_WIDTH
    row_spec = lambda width: pl.BlockSpec((tb, width), lambda b, t: (b * nt + t, 0))
    full = lambda a: pl.BlockSpec(a.shape, lambda b, t: (0,) * a.ndim)
    names = (["mu", "w0", "a0", "wwa", "gup", "kk", "ka", "rk", "lnw", "lnb", "hones"]
             + (["v0", "vup"] if has_vres else []))
    params = [prm[n] for n in names]
    ins = [p_rw] + ([v_first] if has_vres else []) + params
    in_specs = [row_spec(RW_GROUP)] + ([row_spec(W)] if has_vres else []) + [full(a) for a in params]
    if has_vres:
        out_shape = jax.ShapeDtypeStruct((m, W), BF16)
        out_specs = row_spec(W)
    else:
        out_shape = (jax.ShapeDtypeStruct((m, W), BF16), jax.ShapeDtypeStruct((m, W), F32))
        out_specs = (row_spec(W), row_spec(W))
    scratch = [
        pltpu.VMEM((SUBLANES, RW_GROUP), F32),
        pltpu.VMEM((RW_HEADS // 2, LANES, LANES), F32),
        pltpu.VMEM((tb, RW_GROUP), F32),
    ] + [pltpu.VMEM((tb, W), F32) for _ in range(6)] + [
        pltpu.VMEM((nb, 2 * CHUNK, 2 * CHUNK), F32),
        pltpu.VMEM((nb, 2 * CHUNK, 2 * CHUNK), F32),
        pltpu.VMEM((nb, CHUNK, LANES), F32),
        pltpu.VMEM((nb, CHUNK, LANES), F32),
        pltpu.VMEM((nb, 2 * CHUNK, LANES), F32),
    ]
    res = pl.pallas_call(
        functools.partial(_rwkv_kernel, has_vres, tb),
        out_shape=out_shape,
        grid=(bsz, nt),
        in_specs=in_specs,
        out_specs=out_specs,
        scratch_shapes=scratch,
        compiler_params=_cparams(("parallel", "arbitrary")),
        name="rwkv7_vres" if has_vres else "rwkv7",
    )(*ins)
    if has_vres:
        return res, None
    return res


def _gdn_kernel(tb, pg_ref, slab_ref, cw_ref, alog_ref, dtb_ref, onorm_ref, eb_ref, ea_ref, o_ref,
                carry, state, ext, q_scr, k_scr, v_scr, grow_scr, o_scr,
                gcf_scr, kb_scr, vb_scr, kbeg_scr, qeg_scr, kt2_scr,
                intra_scr, u_scr, wq_scr, kt_scr):
    H, Dh, Wd = GDN_HEADS, GDN_HEAD_DIM, GDN_WIDTH
    QKV = 3 * Wd
    tstep = pl.program_id(1)

    @pl.when(tstep == 0)
    def _():
        carry[...] = jnp.zeros_like(carry)
        state[...] = jnp.zeros_like(state)

    ext[0:SUBLANES, :] = carry[...]
    ext[SUBLANES:, :] = pg_ref[:, 0:QKV]
    carry[...] = pg_ref[tb - SUBLANES:tb, 0:QKV]
    for j in range(QKV // Dh):
        ln = slice(j * Dh, (j + 1) * Dh)
        acc = jnp.zeros((tb, Dh), F32)
        for kk in range(GDN_CONV):
            off = SUBLANES - (GDN_CONV - 1) + kk
            acc = acc + ext[pl.ds(off, tb), ln] * cw_ref[kk:kk + 1, ln]
        act = acc * _sigmoid(acc)
        which, h = divmod(j, H)
        hl = slice(h * Dh, (h + 1) * Dh)
        if which == 0:
            nrm = lax.rsqrt(jnp.sum(act * act, axis=-1, keepdims=True) + 1e-6)
            q_scr[:, hl] = act * nrm * (Dh ** -0.5)
        elif which == 1:
            nrm = lax.rsqrt(jnp.sum(act * act, axis=-1, keepdims=True) + 1e-6)
            k_scr[:, hl] = act * nrm
        else:
            v_scr[:, hl] = act

    slab = slab_ref[...]
    gsl = -jnp.exp(alog_ref[...]) * jax.nn.softplus(slab + dtb_ref[...])
    lane = _iota((tb, LANES), 1)
    gsl = jnp.where((lane >= SLAB_ALPHA0) & (lane < SLAB_ALPHA0 + H), gsl, 0.0)

    C = CHUNK
    NC = tb // C
    trow, tcol = _iota((tb, tb), 0), _iota((tb, tb), 1)
    same_chunk = (trow // C) == (tcol // C)
    gcol = _mm_exact_lhs((same_chunk & (trow >= tcol)).astype(BF16), gsl)
    grow_scr[...] = _mm_tn_exact_rhs(gsl, (same_chunk & (trow <= tcol)).astype(BF16))

    def spread(x, e_ref):
        return sum(_mm(t, e_ref[...]) for t in _split3(x))

    beta_f = spread(_sigmoid(slab), eb_ref)
    gc_f = spread(gcol, ea_ref)
    eg_f = jnp.exp(gc_f)
    k2 = k_scr[...]
    kb2 = k2 * beta_f
    gc3 = gc_f.reshape(NC, C, Wd)
    gl3 = gc3[:, C - 1:C, :]
    gcf_scr[...] = gc_f
    kb_scr[...] = kb2
    vb_scr[...] = v_scr[...] * beta_f
    kbeg_scr[...] = kb2 * eg_f
    qeg_scr[...] = q_scr[...] * eg_f
    kt2_scr[...] = (k2.reshape(NC, C, Wd) * jnp.exp(gl3 - gc3)).reshape(tb, Wd)
    egl3 = jnp.exp(gl3)

    def items(fn):
        return jnp.concatenate([fn(c, h)[None] for c in range(NC) for h in range(H)], axis=0)

    rows = lambda c: slice(c * C, (c + 1) * C)
    head = lambda h: slice(h * Dh, (h + 1) * Dh)
    tile = lambda ref: items(lambda c, h: ref[rows(c), head(h)])
    gci = items(lambda c, h: gcf_scr[rows(c), h * Dh:h * Dh + C])
    gr = items(lambda c, h: grow_scr[SLAB_ALPHA0 + h:SLAB_ALPHA0 + h + 1, rows(c)])
    eg_last = jnp.concatenate([egl3[c:c + 1, :, head(h)] for c in range(NC) for h in range(H)], axis=0)
    k_b = tile(k_scr)
    row, col = _iota((1, C, C), 1), _iota((1, C, C), 2)
    dec = jnp.exp(jnp.where(row >= col, gci - gr, -jnp.inf))
    prod = _bmm_nt(jnp.concatenate([tile(kb_scr), tile(q_scr)], axis=1), k_b)
    lmat = jnp.where(row > col, prod[:, :C] * dec, 0.0)
    intra_scr[...] = prod[:, C:] * dec
    nb = NC * H
    l2 = (-lmat).reshape(nb // 2, 2 * C, C)
    prow, pcol = _iota((1, 2 * C, 2 * C), 1), _iota((1, 2 * C, 2 * C), 2)
    lp = jnp.where((prow // C) == (pcol // C), jnp.concatenate([l2, l2], axis=2), 0.0)
    tp = _neumann_inverse(lp)
    rhs = jnp.concatenate([tile(vb_scr), tile(kbeg_scr)], axis=2).reshape(nb // 2, 2 * C, 2 * Dh)
    sol = _bmm(tp, rhs).reshape(nb, C, 2 * Dh)
    u_scr[...] = sol[:, :, :Dh]
    wq_scr[...] = jnp.concatenate([sol[:, :, Dh:], tile(qeg_scr)], axis=1)
    kt_scr[...] = tile(kt2_scr)

    for c in range(NC):
        it = slice(c * H, (c + 1) * H)
        s = state[...]
        ws = _bmm(wq_scr[it], s)
        v_new = u_scr[it] - ws[:, :C]
        o = ws[:, C:] + _bmm(intra_scr[it], v_new)
        state[...] = s * eg_last[it] + _bmm_tn(kt_scr[it], v_new)
        for h in range(H):
            o_scr[rows(c), head(h)] = o[h]

    for h in range(H):
        hl = slice(h * Dh, (h + 1) * Dh)
        o = o_scr[:, hl]
        o = o * lax.rsqrt(jnp.mean(o * o, axis=-1, keepdims=True) + NORM_EPS) * onorm_ref[...]
        z = pg_ref[:, QKV + h * Dh:QKV + (h + 1) * Dh]
        o_ref[:, hl] = (o * (z * _sigmoid(z))).astype(o_ref.dtype)


def _gdn_mixer(p_gdn, p_rw, prm, bsz, seq):
    m = p_gdn.shape[0]
    tb = _pick_tile(seq, (256, 128))
    nt = seq // tb
    nb = (tb // CHUNK) * GDN_HEADS
    Wd = GDN_WIDTH
    lane_head = np.arange(Wd)[None, :] // GDN_HEAD_DIM
    spread = lambda off: jnp.asarray(np.arange(LANES)[:, None] - off == lane_head, BF16)
    params = [prm["cw"], prm["alog"], prm["dtb"], prm["onorm"], spread(SLAB_BETA0), spread(SLAB_ALPHA0)]
    full = lambda a: pl.BlockSpec(a.shape, lambda b, t: (0,) * a.ndim)
    slab_blk = RW_COLS // LANES
    return pl.pallas_call(
        functools.partial(_gdn_kernel, tb),
        out_shape=jax.ShapeDtypeStruct((m, Wd), BF16),
        grid=(bsz, nt),
        in_specs=[pl.BlockSpec((tb, 4 * Wd), lambda b, t: (b * nt + t, 0)),
                  pl.BlockSpec((tb, LANES), lambda b, t: (b * nt + t, slab_blk))] + [full(a) for a in params],
        out_specs=pl.BlockSpec((tb, Wd), lambda b, t: (b * nt + t, 0)),
        scratch_shapes=[
            pltpu.VMEM((SUBLANES, 3 * Wd), F32),
            pltpu.VMEM((GDN_HEADS, GDN_HEAD_DIM, GDN_HEAD_DIM), F32),
            pltpu.VMEM((tb + SUBLANES, 3 * Wd), F32),
            pltpu.VMEM((tb, Wd), F32), pltpu.VMEM((tb, Wd), F32), pltpu.VMEM((tb, Wd), F32),
            pltpu.VMEM((LANES, tb), F32),
            pltpu.VMEM((tb, Wd), F32),
        ] + [pltpu.VMEM((tb, Wd), F32) for _ in range(6)] + [
            pltpu.VMEM((nb, CHUNK, CHUNK), F32),
            pltpu.VMEM((nb, CHUNK, GDN_HEAD_DIM), F32),
            pltpu.VMEM((nb, 2 * CHUNK, GDN_HEAD_DIM), F32),
            pltpu.VMEM((nb, CHUNK, GDN_HEAD_DIM), F32),
        ],
        compiler_params=_cparams(("parallel", "arbitrary")),
        name="gdn",
    )(p_gdn, p_rw, *params)


def _swa_kernel(ps_ref, pos_ref, freq_ref, sgn_lo_ref, sgn_hi_ref, qn_ref, kn_ref, sink_ref, ones_ref, o_ref,
                kprev, vprev):
    Wn = SWA_WINDOW
    hd = SWA_HEAD_DIM
    G = SWA_Q_HEADS // SWA_KV_HEADS
    QW, KW = SWA_WIDTH, SWA_KV_WIDTH
    n = pl.program_id(1)

    @pl.when(n == 0)
    def _():
        kprev[...] = jnp.zeros_like(kprev)
        vprev[...] = jnp.zeros_like(vprev)

    half = ROPE_DIM // 2
    ang = pos_ref[...] * freq_ref[...]
    cs1, sn1 = jnp.cos(ang), jnp.sin(ang)
    lo1, hi1 = sn1 * sgn_lo_ref[...], sn1 * sgn_hi_ref[...]

    def norm_rope(x, gain_row, width):
        rep = lambda t: jnp.concatenate([t] * (width // LANES), axis=1)
        ms = _mm(x * x, ones_ref[0:width, 0:width]) * (1.0 / hd)
        y = x * lax.rsqrt(ms + NORM_EPS) * gain_row
        up = pltpu.roll(y, width - half, 1)
        dn = pltpu.roll(y, half, 1)
        return y * rep(cs1) + up * rep(lo1) + dn * rep(hi1)

    q = norm_rope(ps_ref[:, 0:QW], qn_ref[...], QW) * (hd ** -0.5)
    k = norm_rope(ps_ref[:, QW:QW + KW], kn_ref[...], KW)
    v = ps_ref[:, QW + KW:QW + 2 * KW]

    qi = _iota((G * Wn, 2 * Wn), 0) % Wn
    kj = _iota((G * Wn, 2 * Wn), 1)
    rel = qi + Wn - kj
    allowed = (rel >= 0) & (rel < SWA_WINDOW) & ((kj >= Wn) | (n > 0))
    grp = _iota((G * Wn, 1), 0) // Wn
    sinks = sink_ref[...]
    first_half = _iota((G * Wn, LANES), 1) < hd
    side_mask = (first_half.astype(F32), 1.0 - first_half.astype(F32))
    ones_kv = jnp.ones((2 * Wn, LANES), BF16)

    for pair in range(SWA_KV_HEADS // 2):
        kl = slice(pair * LANES, (pair + 1) * LANES)
        kband = jnp.concatenate([kprev[:, kl], k[:, kl]], axis=0).astype(BF16)
        vband = jnp.concatenate([vprev[:, kl], v[:, kl]], axis=0).astype(BF16)
        qs = jnp.concatenate([q[:, (pair * G + t) * LANES:(pair * G + t + 1) * LANES] for t in range(G)], axis=0)
        outs = []
        for side in range(2):
            s = jnp.where(allowed, _mm_nt(qs * side_mask[side], kband), -jnp.inf)
            sink = jnp.zeros((G * Wn, 1), F32)
            for t in range(G):
                h = SWA_Q_ORDER[(pair * G + t) * 2 + side]
                sink = jnp.where(grp == t, sinks[:, h:h + 1], sink)
            mx = jnp.maximum(jnp.max(s, axis=-1, keepdims=True), sink)
            p = jnp.exp(s - mx).astype(BF16)
            den = jnp.dot(p, ones_kv, preferred_element_type=F32) + jnp.exp(sink - mx)
            outs.append(jnp.dot(p, vband, preferred_element_type=F32) / den)
        o = jnp.where(first_half, outs[0], outs[1])
        for t in range(G):
            o_ref[:, (pair * G + t) * LANES:(pair * G + t + 1) * LANES] = o[t * Wn:(t + 1) * Wn].astype(o_ref.dtype)

    kprev[...] = k
    vprev[...] = v


def _swa_mixer(p_swa, pos_col, prm, bsz, seq):
    m = p_swa.shape[0]
    Wn = SWA_WINDOW
    nb = seq // Wn
    head_ones = _head_ones(SWA_WIDTH, SWA_HEAD_DIM)
    params = [prm["freq"], prm["sgn_lo"], prm["sgn_hi"], prm["qn"], prm["kn"], prm["sinks"], head_ones]
    full = lambda a: pl.BlockSpec(a.shape, lambda b, t: (0,) * a.ndim)
    return pl.pallas_call(
        _swa_kernel,
        out_shape=jax.ShapeDtypeStruct((m, SWA_WIDTH), BF16),
        grid=(bsz, nb),
        in_specs=[pl.BlockSpec((Wn, SWA_COLS), lambda b, t: (b * nb + t, 0)),
                  pl.BlockSpec((Wn, 1), lambda b, t: (b * nb + t, 0))] + [full(a) for a in params],
        out_specs=pl.BlockSpec((Wn, SWA_WIDTH), lambda b, t: (b * nb + t, 0)),
        scratch_shapes=[pltpu.VMEM((Wn, SWA_KV_WIDTH), F32), pltpu.VMEM((Wn, SWA_KV_WIDTH), F32)],
        compiler_params=_cparams(("parallel", "arbitrary")),
        name="swa",
    )(p_swa, pos_col, *params)


def _merge_kernel(x_ref, gain_ref, orw_ref, ogdn_ref, oswa_ref, wg0_ref, wg1_ref, wg2_ref,
                  wb0_ref, wb1_ref, wb2_ref, wout_ref, o_ref, h_scr, acc):
    j = pl.program_id(1)

    @pl.when(j == 0)
    def _():
        x = x_ref[...]
        y = x * lax.rsqrt(jnp.mean(x * x, axis=-1, keepdims=True) + NORM_EPS) * gain_ref[...]
        h_scr[...] = y.astype(BF16)
        acc[...] = jnp.zeros_like(acc)

    h = h_scr[...]
    branch = lambda wg_ref, a_ref, wb_ref: (
        _sigmoid(jnp.dot(h, wg_ref[...], preferred_element_type=F32))
        * jnp.dot(a_ref[...], wb_ref[...], preferred_element_type=F32))
    merged = (branch(wg0_ref, orw_ref, wb0_ref) + branch(wg1_ref, ogdn_ref, wb1_ref)
              + branch(wg2_ref, oswa_ref, wb2_ref))
    acc[...] += jnp.dot(merged.astype(BF16), wout_ref[...], preferred_element_type=F32)

    @pl.when(j == pl.num_programs(1) - 1)
    def _():
        o_ref[...] = x_ref[...] + acc[...]


def _merge(x2, gain, o_rw, o_gdn, o_swa, w_gate, wb_rw, wb_gdn, wb_swa, w_out):
    m, d = x2.shape
    tm = _pick_tile(m, (512, 256, 128, 64, 32, 16, 8))
    tn = 512
    nj = d // tn
    rows = lambda width: pl.BlockSpec((tm, width), lambda i, j: (i, 0))
    gate = lambda b: pl.BlockSpec((d, tn), lambda i, j, b=b: (0, b * nj + j))
    wcol = lambda kdim: pl.BlockSpec((kdim, tn), lambda i, j: (0, j))
    return pl.pallas_call(
        _merge_kernel,
        out_shape=jax.ShapeDtypeStruct((m, d), F32),
        grid=(m // tm, nj),
        in_specs=[rows(d), pl.BlockSpec((1, d), lambda i, j: (0, 0)),
                  rows(RW_WIDTH), rows(GDN_WIDTH), rows(SWA_WIDTH), gate(0), gate(1), gate(2),
                  wcol(RW_WIDTH), wcol(GDN_WIDTH), wcol(SWA_WIDTH),
                  pl.BlockSpec((tn, d), lambda i, j: (j, 0))],
        out_specs=rows(d),
        scratch_shapes=[pltpu.VMEM((tm, d), BF16), pltpu.VMEM((tm, d), F32)],
        compiler_params=_cparams(("parallel", "arbitrary")),
        name="merge",
    )(x2, gain.reshape(1, d), o_rw, o_gdn, o_swa, w_gate, w_gate, w_gate, wb_rw, wb_gdn, wb_swa, w_out)


def _ffn_up_kernel(tm, seq, x_ref, halo_ref, gain_ref, wg_ref, wu_ref, cg_ref, cu_ref, act_ref, h_scr, u_scr):
    i = pl.program_id(0)
    j = pl.program_id(1)
    tn = wg_ref.shape[1]

    @pl.when(j == 0)
    def _():
        def norm(x):
            return (x * lax.rsqrt(jnp.mean(x * x, axis=-1, keepdims=True) + NORM_EPS) * gain_ref[...]).astype(BF16)
        seq_start = (i * tm) % seq == 0
        halo = jnp.where(seq_start, 0.0, halo_ref[...])
        h_scr[0:2 * SUBLANES, :] = norm(jnp.concatenate([jnp.zeros_like(halo), halo], axis=0))
        h_scr[2 * SUBLANES:, :] = norm(x_ref[...])

    h = h_scr[...]
    u_scr[:, 0:tn] = jnp.dot(h, wg_ref[...], preferred_element_type=F32)
    u_scr[:, tn:2 * tn] = jnp.dot(h, wu_ref[...], preferred_element_type=F32)

    def conv(c0, cw_ref):
        out = jnp.zeros((tm, tn), F32)
        for kk in range(FFN_CONV):
            off = 2 * SUBLANES - (FFN_CONV - 1) + kk
            out = out + u_scr[pl.ds(off, tm), c0:c0 + tn] * cw_ref[kk:kk + 1, :]
        return out

    gate = conv(0, cg_ref)
    up = conv(tn, cu_ref)
    act_ref[...] = (gate * _sigmoid(gate) * up).astype(BF16)


def _ffn_down_kernel(act_ref, wd_ref, x_ref, o_ref):
    o_ref[...] = x_ref[...] + jnp.dot(act_ref[...], wd_ref[...], preferred_element_type=F32)


def _conv_ffn(x2, gain, w_gate, w_up, c_gate, c_up, w_down, seq):
    m, d = x2.shape
    dff = w_gate.shape[1]
    tm = _pick_tile(seq, (1024, 512, 256, 128, 64, 32, 16))
    tn = _pick_tile(dff, (512, 256, 128))
    hb = tm // SUBLANES
    act = pl.pallas_call(
        functools.partial(_ffn_up_kernel, tm, seq),
        out_shape=jax.ShapeDtypeStruct((m, dff), BF16),
        grid=(m // tm, dff // tn),
        in_specs=[
            pl.BlockSpec((tm, d), lambda i, j: (i, 0)),
            pl.BlockSpec((SUBLANES, d), lambda i, j: (jnp.maximum(i * hb - 1, 0), 0)),
            pl.BlockSpec((1, d), lambda i, j: (0, 0)),
            pl.BlockSpec((d, tn), lambda i, j: (0, j)),
            pl.BlockSpec((d, tn), lambda i, j: (0, j)),
            pl.BlockSpec((FFN_CONV, tn), lambda i, j: (0, j)),
            pl.BlockSpec((FFN_CONV, tn), lambda i, j: (0, j)),
        ],
        out_specs=pl.BlockSpec((tm, tn), lambda i, j: (i, j)),
        scratch_shapes=[pltpu.VMEM((tm + 2 * SUBLANES, d), BF16),
                        pltpu.VMEM((tm + 2 * SUBLANES, 2 * tn), F32)],
        compiler_params=_cparams(("parallel", "arbitrary")),
        name="ffn_up",
    )(x2, x2, gain.reshape(1, d), w_gate, w_up, c_gate, c_up)
    tm2 = _pick_tile(m, (1024, 512, 256, 128, 64, 32, 16, 8))
    tn2 = _pick_tile(d, (512, 256, 128))
    return pl.pallas_call(
        _ffn_down_kernel,
        out_shape=jax.ShapeDtypeStruct((m, d), F32),
        grid=(m // tm2, d // tn2),
        in_specs=[
            pl.BlockSpec((tm2, dff), lambda i, j: (i, 0)),
            pl.BlockSpec((dff, tn2), lambda i, j: (0, j)),
            pl.BlockSpec((tm2, tn2), lambda i, j: (i, j)),
        ],
        out_specs=pl.BlockSpec((tm2, tn2), lambda i, j: (i, j)),
        compiler_params=_cparams(("parallel", "arbitrary")),
        name="ffn_down",
    )(act, w_down, x2)


def _row(a):
    return a.reshape(1, -1).astype(F32)


def _head_ones(width, head_dim):
    idx = np.arange(width) // head_dim
    return jnp.asarray(idx[:, None] == idx[None, :], BF16)


def _swa_head_order(a, axis):
    hd = SWA_HEAD_DIM
    take = lambda h: lax.slice_in_dim(a, h * hd, (h + 1) * hd, axis=axis)
    return jnp.concatenate([take(h) for h in SWA_Q_ORDER], axis=axis)


def _pad_cols(a, width):
    return jnp.pad(a, ((0, 0), (0, width - a.shape[1])))


def _layer_params(l, w_in, rw_mu, rw_w0, rw_w_up, rw_a0, rw_a_up, rw_g_up, rw_k_k, rw_k_a, rw_r_k, rw_ln_w,
                  rw_ln_b, vres_down, vres_mu, vres_v0, vres_v_up, gdn_conv, gdn_a_log, gdn_dt_bias, gdn_o_norm,
                  swa_q_norm, swa_k_norm, swa_sinks):
    d = w_in.shape[1]
    c_gdn = RW_COLS
    c_ba = c_gdn + 4 * GDN_WIDTH
    c_swa = c_ba + 2 * GDN_HEADS
    c_gate = c_swa + SWA_COLS
    has_vres = l > 0
    vres_w = vres_down[l - 1] if has_vres else jnp.zeros((d, RW_VRES_RANK), F32)
    assert c_ba % LANES == 0
    ba_w = _pack_cols(w_in, l, c_ba, LANES)[:, :2 * GDN_HEADS]
    slab_w = _pad_cols(jnp.concatenate([vres_w.astype(BF16), ba_w], axis=1), LANES)
    w_rw = jnp.concatenate([_pack_cols(w_in, l, 0, RW_COLS), slab_w], axis=1)
    w_gdn = _pack_cols(w_in, l, c_gdn, 4 * GDN_WIDTH)
    w_swa = _pack_cols(w_in, l, c_swa, SWA_COLS)
    w_swa = jnp.concatenate([_swa_head_order(w_swa[:, :SWA_WIDTH], axis=1), w_swa[:, SWA_WIDTH:]], axis=1)
    w_gate = _pack_cols(w_in, l, c_gate, w_in.shape[2] - c_gate)

    W = RW_WIDTH
    mu = jnp.concatenate([rw_mu[l], vres_mu[l - 1] if has_vres else jnp.zeros((RW_VRES_RANK,), F32),
                          jnp.zeros((LANES - RW_VRES_RANK,), F32)])
    wwa = jnp.zeros((LANES, 2 * W), F32)
    wwa = wwa.at[:RW_DECAY_RANK, :W].set(rw_w_up[l]).at[RW_DECAY_RANK:, W:].set(rw_a_up[l])
    rw = dict(mu=_row(mu), w0=_row(rw_w0[l]), a0=_row(rw_a0[l]), wwa=wwa.astype(BF16),
              gup=rw_g_up[l].astype(BF16), kk=_row(rw_k_k[l]), ka=_row(rw_k_a[l]), rk=_row(rw_r_k[l]),
              lnw=_row(rw_ln_w[l]), lnb=_row(rw_ln_b[l]), hones=_head_ones(W, RW_HEAD_DIM))
    if has_vres:
        vup = jnp.zeros((LANES, W), F32).at[:RW_VRES_RANK].set(vres_v_up[l - 1])
        rw.update(v0=_row(vres_v0[l - 1]), vup=vup.astype(BF16))

    lanes_pad = lambda a, off: jnp.zeros((1, LANES), F32).at[0, off:off + a.shape[0]].set(a)
    gdn = dict(cw=gdn_conv[l].astype(F32), alog=lanes_pad(gdn_a_log[l], SLAB_ALPHA0),
               dtb=lanes_pad(gdn_dt_bias[l], SLAB_ALPHA0), onorm=_row(gdn_o_norm[l]))

    half = ROPE_DIM // 2
    lane = np.arange(LANES) % SWA_HEAD_DIM
    inv_freq = ROPE_THETA ** (-(lane % half).astype(np.float32) * 2.0 / ROPE_DIM)
    freq = np.where(lane < ROPE_DIM, inv_freq, 0.0).astype(np.float32)
    sgn_lo = np.where(lane < half, -1.0, 0.0).astype(np.float32)
    sgn_hi = np.where((lane >= half) & (lane < ROPE_DIM), 1.0, 0.0).astype(np.float32)
    swa = dict(freq=jnp.asarray(freq).reshape(1, -1), sgn_lo=jnp.asarray(sgn_lo).reshape(1, -1),
               sgn_hi=jnp.asarray(sgn_hi).reshape(1, -1),
               qn=_row(jnp.tile(swa_q_norm[l], SWA_Q_HEADS)), kn=_row(jnp.tile(swa_k_norm[l], SWA_KV_HEADS)),
               sinks=lanes_pad(swa_sinks[l], 0))
    groups = (w_rw, w_gdn, w_swa)
    mix_widths = tuple(pl.cdiv(g.shape[1], INPROJ_TN) * INPROJ_TN for g in groups)
    w_mix = jnp.concatenate([_pad_cols(g, wd) for g, wd in zip(groups, mix_widths)], axis=1)
    return dict(w_mix=w_mix, mix_widths=mix_widths, w_gate=w_gate, rw=rw, gdn=gdn, swa=swa, has_vres=has_vres)


def kernel(x, positions, norm_mix, w_in, rw_mu, rw_w0, rw_w_up, rw_a0, rw_a_up, rw_g_up, rw_k_k, rw_k_a, rw_r_k, rw_ln_w, rw_ln_b, vres_down, vres_mu, vres_v0, vres_v_up, gdn_conv, gdn_a_log, gdn_dt_bias, gdn_o_norm, swa_q_norm, swa_k_norm, swa_sinks, w_branch, w_out, norm_ffn, ffn_up, ffn_conv, ffn_down):
    bsz, seq, d = x.shape
    depth = w_in.shape[0]
    x2 = x.reshape(bsz * seq, d)
    pos_col = positions.astype(F32).reshape(bsz * seq, 1)
    dff = ffn_down.shape[1]
    v_first = None
    for l in range(depth):
        lp = _layer_params(l, w_in, rw_mu, rw_w0, rw_w_up, rw_a0, rw_a_up, rw_g_up, rw_k_k, rw_k_a, rw_r_k,
                           rw_ln_w, rw_ln_b, vres_down, vres_mu, vres_v0, vres_v_up, gdn_conv, gdn_a_log,
                           gdn_dt_bias, gdn_o_norm, swa_q_norm, swa_k_norm, swa_sinks)
        p_rw, p_gdn, p_swa = _inproj(x2, norm_mix[l], lp["w_mix"], lp["mix_widths"])
        if lp["has_vres"]:
            o_rw, _ = _rwkv_mixer(p_rw, v_first, lp["rw"], bsz, seq, True)
        else:
            o_rw, v_first = _rwkv_mixer(p_rw, None, lp["rw"], bsz, seq, False)
        o_gdn = _gdn_mixer(p_gdn, p_rw, lp["gdn"], bsz, seq)
        o_swa = _swa_mixer(p_swa, pos_col, lp["swa"], bsz, seq)
        wb = w_branch[l].astype(BF16)
        x2 = _merge(x2, norm_mix[l], o_rw, o_gdn, o_swa, lp["w_gate"], wb[:RW_WIDTH], wb[RW_WIDTH:RW_WIDTH + GDN_WIDTH],
                    _swa_head_order(wb[RW_WIDTH + GDN_WIDTH:], axis=0), w_out[l].astype(BF16))
        up = ffn_up[l].astype(BF16)
        x2 = _conv_ffn(x2, norm_ffn[l], up[:, :dff], up[:, dff:], ffn_conv[l][:, :dff], ffn_conv[l][:, dff:],
                       ffn_down[l].astype(BF16), seq)
    return x2.reshape(bsz, seq, d)
```

```python
import functools

import jax
import jax.numpy as jnp
import numpy as np
from jax import lax
from jax.experimental import pallas as pl
from jax.experimental.pallas import tpu as pltpu

F32 = jnp.float32
BF16 = jnp.bfloat16

D_MODEL = 2048
RW_HEADS, RW_HEAD_DIM = 8, 64
RW_WIDTH = RW_HEADS * RW_HEAD_DIM
RW_DECAY_RANK, RW_ICLR_RANK, RW_GATE_RANK, RW_VRES_RANK = 64, 64, 128, 32
RW_GN_EPS = 64e-5
RW_COLS = 3 * RW_WIDTH + RW_DECAY_RANK + RW_ICLR_RANK + RW_GATE_RANK
GDN_HEADS, GDN_HEAD_DIM = 6, 128
GDN_WIDTH = GDN_HEADS * GDN_HEAD_DIM
GDN_CONV = 4
SWA_Q_HEADS, SWA_KV_HEADS, SWA_HEAD_DIM = 12, 4, 64
SWA_WIDTH = SWA_Q_HEADS * SWA_HEAD_DIM
SWA_KV_WIDTH = SWA_KV_HEADS * SWA_HEAD_DIM
SWA_WINDOW = 128
SWA_COLS = SWA_WIDTH + 2 * SWA_KV_WIDTH
ROPE_DIM = SWA_HEAD_DIM // 4
ROPE_THETA = 500000.0
D_FF = 5632
FFN_CONV = 3
NORM_EPS = 1e-6

LANES = 128
SUBLANES = 8
PACKED_ROWS = 16
VMEM_LIMIT_BYTES = 56 * 1024 * 1024

CHUNK = 64
INV_BLOCK = 16
FFN_SUBTILE = 512
SLAB_VRES0 = 0
SLAB_BETA0 = RW_VRES_RANK
SLAB_ALPHA0 = RW_VRES_RANK + GDN_HEADS
RW_GROUP = RW_COLS + LANES
_G = SWA_Q_HEADS // SWA_KV_HEADS
SWA_Q_ORDER = tuple((2 * p + side) * _G + t for p in range(SWA_KV_HEADS // 2) for t in range(_G) for side in range(2))

def _cparams(sem):
    return pltpu.CompilerParams(dimension_semantics=sem, vmem_limit_bytes=VMEM_LIMIT_BYTES)


def _dg(a, b, dims):
    return lax.dot_general(a.astype(BF16), b.astype(BF16), dims, preferred_element_type=F32)


def _mm(a, b):
    return _dg(a, b, (((1,), (0,)), ((), ())))


def _mm_nt(a, b):
    return _dg(a, b, (((1,), (1,)), ((), ())))


def _mm_tn(a, b):
    return _dg(a, b, (((0,), (0,)), ((), ())))


def _bmm(a, b):
    return _dg(a, b, (((2,), (1,)), ((0,), (0,))))


def _bmm_nt(a, b):
    return _dg(a, b, (((2,), (2,)), ((0,), (0,))))


def _bmm_tn(a, b):
    return _dg(a, b, (((1,), (1,)), ((0,), (0,))))


def _split3(a):
    hi = a.astype(BF16)
    r1 = a - hi.astype(F32)
    mid = r1.astype(BF16)
    lo = (r1 - mid.astype(F32)).astype(BF16)
    return hi, mid, lo


def _mm_exact_lhs(a_bf16, b):
    hi, mid, lo = _split3(b)
    d = lambda t: jnp.dot(a_bf16, t, preferred_element_type=F32)
    return d(hi) + d(mid) + d(lo)


def _mm_tn_exact_rhs(a, b_bf16):
    hi, mid, lo = _split3(a)
    d = lambda t: _mm_tn(t, b_bf16)
    return d(hi) + d(mid) + d(lo)


def _iota(shape, dim):
    return lax.broadcasted_iota(jnp.int32, shape, dim)


def _sigmoid(x):
    return 0.5 * jnp.tanh(0.5 * x) + 0.5


def _neumann_inverse(lmat):
    assert CHUNK // INV_BLOCK == 4
    n = lmat.shape[-1]
    row, col = _iota((1, n, n), 1), _iota((1, n, n), 2)
    eye = (row == col).astype(F32)
    in_blk = (row // INV_BLOCK) == (col // INV_BLOCK)
    lb = jnp.where(in_blk, lmat, 0.0)
    e = lmat - lb
    inv = eye + lb
    p = _bmm(lb, lb)
    steps = INV_BLOCK.bit_length() - 2
    for s in range(steps):
        if s + 1 < steps:
            both = _bmm(jnp.concatenate([inv, p], axis=1), p)
            inv, p = inv + both[:, :n], both[:, n:]
        else:
            inv = inv + _bmm(inv, p)
    nmat = _bmm(inv, e)
    n2 = _bmm(nmat, nmat)
    m = eye + nmat + n2 + _bmm(nmat, n2)
    return _bmm(m, inv)


def _pack_kernel(shift, *refs):
    if shift == 0:
        a_ref, o_ref = refs
        o_ref[...] = a_ref[...].astype(BF16)
    else:
        a_ref, b_ref, o_ref = refs
        lane = _iota(a_ref.shape, 1)
        moved_a = pltpu.roll(a_ref[...], LANES - shift, 1)
        moved_b = pltpu.roll(b_ref[...], LANES - shift, 1)
        o_ref[...] = jnp.where(lane < LANES - shift, moved_a, moved_b).astype(BF16)


def _pack_cols(w3, layer, start, width):
    _, d, n = w3.shape
    q, shift = divmod(start, LANES)
    last_blk = pl.cdiv(n, LANES) - 1
    src = lambda off: pl.BlockSpec((None, d, LANES), lambda t: (layer, 0, jnp.minimum(q + t + off, last_blk)))
    ins = [w3] if shift == 0 else [w3, w3]
    return pl.pallas_call(
        functools.partial(_pack_kernel, shift),
        out_shape=jax.ShapeDtypeStruct((d, width), BF16),
        grid=(width // LANES,),
        in_specs=[src(0)] if shift == 0 else [src(0), src(1)],
        out_specs=pl.BlockSpec((d, LANES), lambda t: (0, t)),
        compiler_params=_cparams(("arbitrary",)),
        name="pack_cols",
    )(*ins)


INPROJ_TN = 512


def _inproj_kernel(starts, x_ref, g_ref, w_ref, *rest):
    outs, h_scr = rest[:-1], rest[-1]
    j = pl.program_id(1)

    @pl.when(j == 0)
    def _():
        x = x_ref[...]
        y = x * lax.rsqrt(jnp.mean(x * x, axis=-1, keepdims=True) + NORM_EPS) * g_ref[...]
        h_scr[...] = y.astype(BF16)

    for k, o_ref in enumerate(outs):
        @pl.when((j >= starts[k]) & (j < starts[k + 1]))
        def _(o_ref=o_ref):
            o_ref[...] = jnp.dot(h_scr[...], w_ref[...], preferred_element_type=F32).astype(o_ref.dtype)


def _pick_tile(n, prefs):
    for t in prefs:
        if n % t == 0:
            return t
    return n


def _inproj(x2, gain, w_bf16, widths):
    m, d = x2.shape
    tn = INPROJ_TN
    assert sum(widths) == w_bf16.shape[1] and all(w % tn == 0 for w in widths)
    tm = _pick_tile(m, (1024, 512, 256, 128, 64, 32, 16, 8))
    starts = tuple(int(s) for s in np.cumsum((0,) + tuple(w // tn for w in widths)))
    out_spec = lambda k: pl.BlockSpec(
        (tm, tn), lambda i, j: (i, jnp.clip(j - starts[k], 0, starts[k + 1] - starts[k] - 1)))
    return pl.pallas_call(
        functools.partial(_inproj_kernel, starts),
        out_shape=tuple(jax.ShapeDtypeStruct((m, w), BF16) for w in widths),
        grid=(m // tm, starts[-1]),
        in_specs=[
            pl.BlockSpec((tm, d), lambda i, j: (i, 0)),
            pl.BlockSpec((1, d), lambda i, j: (0, 0)),
            pl.BlockSpec((d, tn), lambda i, j: (0, j)),
        ],
        out_specs=tuple(out_spec(k) for k in range(len(widths))),
        scratch_shapes=[pltpu.VMEM((tm, d), BF16)],
        compiler_params=_cparams(("parallel", "arbitrary")),
        name="inproj",
    )(x2, gain.reshape(1, d), w_bf16)


def _rwkv_kernel(has_vres, tb, *refs):
    scr = refs[-14:]
    (carry, state, pl_scr, at_scr, rt_scr, bt_scr, kt_scr, v_scr, y_scr,
     t_scr, rb_scr, akv_scr, rkv_scr, btk_scr) = scr
    if has_vres:
        (p_ref, vf_ref, mu_ref, w0_ref, a0_ref, wwa_ref, gup_ref, kk_ref, ka_ref, rk_ref, lnw_ref, lnb_ref,
         hones_ref, v0_ref, vup_ref, o_ref) = refs[:-14]
    else:
        (p_ref, mu_ref, w0_ref, a0_ref, wwa_ref, gup_ref, kk_ref, ka_ref, rk_ref, lnw_ref, lnb_ref,
         hones_ref, o_ref, vout_ref) = refs[:-14]
    W = RW_WIDTH
    tstep = pl.program_id(1)

    @pl.when(tstep == 0)
    def _():
        carry[...] = jnp.zeros_like(carry)
        state[...] = jnp.zeros_like(state)

    row0 = _iota((tb, LANES), 0) == 0
    for c0 in range(0, RW_GROUP, LANES):
        x = p_ref[:, c0:c0 + LANES].astype(F32)
        prev = pltpu.roll(x, 1, 0)
        prev = jnp.where(row0, carry[PACKED_ROWS - 1:PACKED_ROWS, c0:c0 + LANES], prev)
        pl_scr[:, c0:c0 + LANES] = x + (prev - x) * mu_ref[:, c0:c0 + LANES]
    carry[...] = p_ref[tb - PACKED_ROWS:tb, :].astype(F32)

    head_sum = lambda t: _mm(t, hones_ref[...])

    c_wd = 3 * W
    x128 = pl_scr[:, c_wd:c_wd + LANES]
    lane = _iota((tb, LANES), 1)
    xin = jnp.where(lane < RW_DECAY_RANK, jnp.tanh(x128), x128)
    wa = jnp.dot(xin.astype(BF16), wwa_ref[...], preferred_element_type=F32)
    w_log = -jax.nn.softplus(-(w0_ref[...] + wa[:, :W])) - 0.5
    lw = -jnp.exp(w_log)
    asig = _sigmoid(a0_ref[...] + wa[:, W:])
    gd = pl_scr[:, c_wd + LANES:c_wd + 2 * LANES]
    g = jnp.dot(_sigmoid(gd).astype(BF16), gup_ref[...], preferred_element_type=F32)

    v = pl_scr[:, 2 * W:3 * W]
    if has_vres:
        vd = pl_scr[:, RW_COLS:RW_COLS + LANES]
        vl = jnp.dot(vd.astype(BF16), vup_ref[...], preferred_element_type=F32)
        v = v + (vf_ref[...] - v) * _sigmoid(v0_ref[...] + vl)
    else:
        vout_ref[...] = v
    v_scr[...] = v

    k = pl_scr[:, W:2 * W]
    kkraw = k * kk_ref[...]
    ssq = head_sum(kkraw * kkraw)
    kk = kkraw * lax.rsqrt(ssq + 1e-12)
    kfin = k * (1.0 + (asig - 1.0) * ka_ref[...])
    r = pl_scr[:, 0:W]
    bonus = head_sum(r * kfin * rk_ref[...]) * v

    C = CHUNK
    P2 = 2 * C
    NC, NP = tb // C, RW_HEADS // 2
    tril_b = (_iota((NC, C, C), 1) >= _iota((NC, C, C), 2)).astype(BF16)
    cum = sum(_bmm(tril_b, t.reshape(NC, C, W)) for t in _split3(lw)).reshape(tb, W)
    e_pos = jnp.exp(cum)
    e_neg = jnp.exp(-cum)
    rt_scr[...] = r * e_pos
    kt_scr[...] = kfin * e_neg
    bt_scr[...] = (kk * asig) * e_neg
    at_scr[...] = -kk * jnp.exp(cum - lw)

    def to_b(ref):
        return jnp.concatenate([ref[c * C:(c + 1) * C, p * LANES:(p + 1) * LANES][None]
                                for c in range(NC) for p in range(NP)], axis=0)

    row, col = _iota((1, P2, P2), 1), _iota((1, P2, P2), 2)
    same_head = (row // C) == (col // C)
    strict = same_head & ((row % C) > (col % C))
    incl = same_head & ((row % C) >= (col % C))
    left = _iota((1, C, LANES), 2) < RW_HEAD_DIM
    leftf = left.astype(F32)
    rightf = 1.0 - leftf

    def sel(x3):
        return jnp.where(left, x3[:, :C], x3[:, C:])

    at_b, rt_b, bt_b, kt_b, vp_b = to_b(at_scr), to_b(rt_scr), to_b(bt_scr), to_b(kt_scr), to_b(v_scr)
    lhs = jnp.concatenate([at_b * leftf, rt_b * leftf, at_b * rightf, rt_b * rightf], axis=1)
    out_a = _bmm_nt(lhs, jnp.concatenate([bt_b, kt_b], axis=1))
    out_b = _bmm_nt(lhs, jnp.concatenate([kt_b, bt_b], axis=1))
    ab = jnp.where(strict, jnp.concatenate([out_a[:, 0:C], out_b[:, 2 * C:3 * C]], axis=1), 0.0)
    ak = jnp.where(strict, jnp.concatenate([out_b[:, 0:C], out_a[:, 2 * C:3 * C]], axis=1), 0.0)
    rb = jnp.where(incl, jnp.concatenate([out_a[:, C:2 * C], out_b[:, 3 * C:4 * C]], axis=1), 0.0)
    rk = jnp.where(incl, jnp.concatenate([out_b[:, C:2 * C], out_a[:, 3 * C:4 * C]], axis=1), 0.0)
    t_scr[...] = _neumann_inverse(ab)
    rb_scr[...] = rb
    vv = jnp.concatenate([vp_b, vp_b], axis=1)
    akv_scr[...] = sel(_bmm(ak, vv))
    rkv_scr[...] = sel(_bmm(rk, vv))
    pc_rows = jnp.concatenate([e_pos[(c + 1) * C - 1:(c + 1) * C, p * LANES:(p + 1) * LANES][None]
                               for c in range(NC) for p in range(NP)], axis=0)
    btk_scr[...] = jnp.concatenate([bt_b * pc_rows, kt_b * pc_rows], axis=1)

    for c in range(NC):
        items = slice(c * NP, (c + 1) * NP)
        s = state[...]
        a_c, r_c, v_c = at_b[items], rt_b[items], vp_b[items]
        ars = _bmm_nt(jnp.concatenate([a_c, r_c], axis=1), s)
        rhs = ars[:, :C] + akv_scr[items]
        u = sel(_bmm(t_scr[items], jnp.concatenate([rhs, rhs], axis=1)))
        y = ars[:, C:] + sel(_bmm(rb_scr[items], jnp.concatenate([u, u], axis=1))) + rkv_scr[items]
        s_new = s * pc_rows[items] + _bmm_tn(jnp.concatenate([u, v_c], axis=1), btk_scr[items])
        state[...] = jnp.where(same_head, s_new, 0.0)
        for p in range(NP):
            y_scr[c * C:(c + 1) * C, p * LANES:(p + 1) * LANES] = y[p]

    y = y_scr[...]
    inv_n = 1.0 / RW_HEAD_DIM
    y_hi = y.astype(BF16).astype(F32)
    mean = (head_sum(y_hi) + head_sum(y - y_hi)) * inv_n
    yc = y - mean
    var = head_sum(yc * yc) * inv_n
    yn = yc * lax.rsqrt(var + RW_GN_EPS) * lnw_ref[...] + lnb_ref[...]
    o_ref[...] = ((yn + bonus) * g).astype(o_ref.dtype)


def _rwkv_mixer(p_rw, v_first, prm, bsz, seq, has_vres):
    m = p_rw.shape[0]
    tb = _pick_tile(seq, (256, 128, 64))
    nt = seq // tb
    nb = (tb // CHUNK) * (RW_HEADS // 2)
    W = RW_WIDTH
    row_spec = lambda width: pl.BlockSpec((tb, width), lambda b, t: (b * nt + t, 0))
    full = lambda a: pl.BlockSpec(a.shape, lambda b, t: (0,) * a.ndim)
    names = (["mu", "w0", "a0", "wwa", "gup", "kk", "ka", "rk", "lnw", "lnb", "hones"]
             + (["v0", "vup"] if has_vres else []))
    params = [prm[n] for n in names]
    ins = [p_rw] + ([v_first] if has_vres else []) + params
    in_specs = [row_spec(RW_GROUP)] + ([row_spec(W)] if has_vres else []) + [full(a) for a in params]
    if has_vres:
        out_shape = jax.ShapeDtypeStruct((m, W), BF16)
        out_specs = row_spec(W)
    else:
        out_shape = (jax.ShapeDtypeStruct((m, W), BF16), jax.ShapeDtypeStruct((m, W), F32))
        out_specs = (row_spec(W), row_spec(W))
    scratch = [
        pltpu.VMEM((PACKED_ROWS, RW_GROUP), F32),
        pltpu.VMEM((RW_HEADS // 2, LANES, LANES), F32),
        pltpu.VMEM((tb, RW_GROUP), F32),
    ] + [pltpu.VMEM((tb, W), F32) for _ in range(6)] + [
        pltpu.VMEM((nb, 2 * CHUNK, 2 * CHUNK), F32),
        pltpu.VMEM((nb, 2 * CHUNK, 2 * CHUNK), F32),
        pltpu.VMEM((nb, CHUNK, LANES), F32),
        pltpu.VMEM((nb, CHUNK, LANES), F32),
        pltpu.VMEM((nb, 2 * CHUNK, LANES), F32),
    ]
    res = pl.pallas_call(
        functools.partial(_rwkv_kernel, has_vres, tb),
        out_shape=out_shape,
        grid=(bsz, nt),
        in_specs=in_specs,
        out_specs=out_specs,
        scratch_shapes=scratch,
        compiler_params=_cparams(("parallel", "arbitrary")),
        name="rwkv7_vres" if has_vres else "rwkv7",
    )(*ins)
    if has_vres:
        return res, None
    return res


def _gdn_kernel(tb, pg_ref, slab_ref, cw_ref, alog_ref, dtb_ref, onorm_ref, eb_ref, ea_ref, o_ref,
                carry, state, ext, q_scr, k_scr, v_scr, grow_scr, o_scr,
                gcf_scr, kb_scr, vb_scr, kbeg_scr, qeg_scr, kt2_scr,
                intra_scr, u_scr, wq_scr, kt_scr):
    H, Dh, Wd = GDN_HEADS, GDN_HEAD_DIM, GDN_WIDTH
    QKV = 3 * Wd
    tstep = pl.program_id(1)

    @pl.when(tstep == 0)
    def _():
        carry[...] = jnp.zeros_like(carry)
        state[...] = jnp.zeros_like(state)

    ext[0:PACKED_ROWS, :] = carry[...]
    ext[PACKED_ROWS:, :] = pg_ref[:, 0:QKV].astype(F32)
    carry[...] = pg_ref[tb - PACKED_ROWS:tb, 0:QKV].astype(F32)
    for j in range(QKV // Dh):
        ln = slice(j * Dh, (j + 1) * Dh)
        acc = jnp.zeros((tb, Dh), F32)
        for kk in range(GDN_CONV):
            off = PACKED_ROWS - (GDN_CONV - 1) + kk
            acc = acc + ext[pl.ds(off, tb), ln] * cw_ref[kk:kk + 1, ln]
        act = acc * _sigmoid(acc)
        which, h = divmod(j, H)
        hl = slice(h * Dh, (h + 1) * Dh)
        if which == 0:
            nrm = lax.rsqrt(jnp.sum(act * act, axis=-1, keepdims=True) + 1e-6)
            q_scr[:, hl] = act * nrm * (Dh ** -0.5)
        elif which == 1:
            nrm = lax.rsqrt(jnp.sum(act * act, axis=-1, keepdims=True) + 1e-6)
            k_scr[:, hl] = act * nrm
        else:
            v_scr[:, hl] = act

    slab = slab_ref[...].astype(F32)
    gsl =-jnp.exp(alog_ref[...]) * jax.nn.softplus(slab + dtb_ref[...])
    lane = _iota((tb, LANES), 1)
    gsl = jnp.where((lane >= SLAB_ALPHA0) & (lane < SLAB_ALPHA0 + H), gsl, 0.0)

    C = CHUNK
    NC = tb // C
    trow, tcol = _iota((tb, tb), 0), _iota((tb, tb), 1)
    same_chunk = (trow // C) == (tcol // C)
    gcol = _mm_exact_lhs((same_chunk & (trow >= tcol)).astype(BF16), gsl)
    grow_scr[...] = _mm_tn_exact_rhs(gsl, (same_chunk & (trow <= tcol)).astype(BF16))

    def spread(x, e_ref):
        return sum(_mm(t, e_ref[...]) for t in _split3(x))

    beta_f = spread(_sigmoid(slab), eb_ref)
    gc_f = spread(gcol, ea_ref)
    eg_f = jnp.exp(gc_f)
    k2 = k_scr[...]
    kb2 = k2 * beta_f
    gc3 = gc_f.reshape(NC, C, Wd)
    gl3 = gc3[:, C - 1:C, :]
    gcf_scr[...] = gc_f
    kb_scr[...] = kb2
    vb_scr[...] = v_scr[...] * beta_f
    kbeg_scr[...] = kb2 * eg_f
    qeg_scr[...] = q_scr[...] * eg_f
    kt2_scr[...] = (k2.reshape(NC, C, Wd) * jnp.exp(gl3 - gc3)).reshape(tb, Wd)
    egl3 = jnp.exp(gl3)

    def items(fn):
        return jnp.concatenate([fn(c, h)[None] for c in range(NC) for h in range(H)], axis=0)

    rows = lambda c: slice(c * C, (c + 1) * C)
    head = lambda h: slice(h * Dh, (h + 1) * Dh)
    tile = lambda ref: items(lambda c, h: ref[rows(c), head(h)])
    gci = items(lambda c, h: gcf_scr[rows(c), h * Dh:h * Dh + C])
    gr = items(lambda c, h: grow_scr[SLAB_ALPHA0 + h:SLAB_ALPHA0 + h + 1, rows(c)])
    eg_last = jnp.concatenate([egl3[c:c + 1, :, head(h)] for c in range(NC) for h in range(H)], axis=0)
    k_b = tile(k_scr)
    row, col = _iota((1, C, C), 1), _iota((1, C, C), 2)
    dec = jnp.exp(jnp.where(row >= col, gci - gr, -jnp.inf))
    prod = _bmm_nt(jnp.concatenate([tile(kb_scr), tile(q_scr)], axis=1), k_b)
    lmat = jnp.where(row > col, prod[:, :C] * dec, 0.0)
    intra_scr[...] = prod[:, C:] * dec
    nb = NC * H
    l2 = (-lmat).reshape(nb // 2, 2 * C, C)
    prow, pcol = _iota((1, 2 * C, 2 * C), 1), _iota((1, 2 * C, 2 * C), 2)
    lp = jnp.where((prow // C) == (pcol // C), jnp.concatenate([l2, l2], axis=2), 0.0)
    tp = _neumann_inverse(lp)
    rhs = jnp.concatenate([tile(vb_scr), tile(kbeg_scr)], axis=2).reshape(nb // 2, 2 * C, 2 * Dh)
    sol = _bmm(tp, rhs).reshape(nb, C, 2 * Dh)
    u_scr[...] = sol[:, :, :Dh]
    wq_scr[...] = jnp.concatenate([sol[:, :, Dh:], tile(qeg_scr)], axis=1)
    kt_scr[...] = tile(kt2_scr)

    for c in range(NC):
        it = slice(c * H, (c + 1) * H)
        s = state[...]
        ws = _bmm(wq_scr[it], s)
        v_new = u_scr[it] - ws[:, :C]
        o = ws[:, C:] + _bmm(intra_scr[it], v_new)
        state[...] = s * eg_last[it] + _bmm_tn(kt_scr[it], v_new)
        for h in range(H):
            o_scr[rows(c), head(h)] = o[h]

    for h in range(H):
        hl = slice(h * Dh, (h + 1) * Dh)
        o = o_scr[:, hl]
        o = o * lax.rsqrt(jnp.mean(o * o, axis=-1, keepdims=True) + NORM_EPS) * onorm_ref[...]
        z = pg_ref[:, QKV + h * Dh:QKV + (h + 1) * Dh].astype(F32)
        o_ref[:, hl] = (o * (z * _sigmoid(z))).astype(o_ref.dtype)


def _gdn_mixer(p_gdn, p_rw, prm, bsz, seq):
    m = p_gdn.shape[0]
    tb = _pick_tile(seq, (256, 128))
    nt = seq // tb
    nb = (tb // CHUNK) * GDN_HEADS
    Wd = GDN_WIDTH
    lane_head = np.arange(Wd)[None, :] // GDN_HEAD_DIM
    spread = lambda off: jnp.asarray(np.arange(LANES)[:, None] - off == lane_head, BF16)
    params = [prm["cw"], prm["alog"], prm["dtb"], prm["onorm"], spread(SLAB_BETA0), spread(SLAB_ALPHA0)]
    full = lambda a: pl.BlockSpec(a.shape, lambda b, t: (0,) * a.ndim)
    slab_blk = RW_COLS // LANES
    return pl.pallas_call(
        functools.partial(_gdn_kernel, tb),
        out_shape=jax.ShapeDtypeStruct((m, Wd), BF16),
        grid=(bsz, nt),
        in_specs=[pl.BlockSpec((tb, 4 * Wd), lambda b, t: (b * nt + t, 0)),
                  pl.BlockSpec((tb, LANES), lambda b, t: (b * nt + t, slab_blk))] + [full(a) for a in params],
        out_specs=pl.BlockSpec((tb, Wd), lambda b, t: (b * nt + t, 0)),
        scratch_shapes=[
            pltpu.VMEM((PACKED_ROWS, 3 * Wd), F32),
            pltpu.VMEM((GDN_HEADS, GDN_HEAD_DIM, GDN_HEAD_DIM), F32),
            pltpu.VMEM((tb + PACKED_ROWS, 3 * Wd), F32),
            pltpu.VMEM((tb, Wd), F32), pltpu.VMEM((tb, Wd), F32), pltpu.VMEM((tb, Wd), F32),
            pltpu.VMEM((LANES, tb), F32),
            pltpu.VMEM((tb, Wd), F32),
        ] + [pltpu.VMEM((tb, Wd), F32) for _ in range(6)] + [
            pltpu.VMEM((nb, CHUNK, CHUNK), F32),
            pltpu.VMEM((nb, CHUNK, GDN_HEAD_DIM), F32),
            pltpu.VMEM((nb, 2 * CHUNK, GDN_HEAD_DIM), F32),
            pltpu.VMEM((nb, CHUNK, GDN_HEAD_DIM), F32),
        ],
        compiler_params=_cparams(("parallel", "arbitrary")),
        name="gdn",
    )(p_gdn, p_rw, *params)


def _swa_kernel(ps_ref, pos_ref, freq_ref, sgn_lo_ref, sgn_hi_ref, qn_ref, kn_ref, sink_ref, ones_ref, o_ref,
                kprev, vprev):
    Wn = SWA_WINDOW
    hd = SWA_HEAD_DIM
    G = SWA_Q_HEADS // SWA_KV_HEADS
    QW, KW = SWA_WIDTH, SWA_KV_WIDTH
    n = pl.program_id(1)

    @pl.when(n == 0)
    def _():
        kprev[...] = jnp.zeros_like(kprev)
        vprev[...] = jnp.zeros_like(vprev)

    half = ROPE_DIM // 2
    ang = pos_ref[...] * freq_ref[...]
    cs1, sn1 = jnp.cos(ang), jnp.sin(ang)
    lo1, hi1 = sn1 * sgn_lo_ref[...], sn1 * sgn_hi_ref[...]

    def norm_rope(x, gain_row, width):
        rep = lambda t: jnp.concatenate([t] * (width // LANES), axis=1)
        ms = _mm(x * x, ones_ref[0:width, 0:width]) * (1.0 / hd)
        y = x * lax.rsqrt(ms + NORM_EPS) * gain_row
        up = pltpu.roll(y, width - half, 1)
        dn = pltpu.roll(y, half, 1)
        return y * rep(cs1) + up * rep(lo1) + dn * rep(hi1)

    q = norm_rope(ps_ref[:, 0:QW].astype(F32), qn_ref[...], QW) * (hd ** -0.5)
    k = norm_rope(ps_ref[:, QW:QW + KW].astype(F32), kn_ref[...], KW)
    v = ps_ref[:, QW + KW:QW + 2 * KW].astype(F32)

    qi = _iota((G * Wn, 2 * Wn), 0) % Wn
    kj = _iota((G * Wn, 2 * Wn), 1)
    rel = qi + Wn - kj
    allowed = (rel >= 0) & (rel < SWA_WINDOW) & ((kj >= Wn) | (n > 0))
    grp = _iota((G * Wn, 1), 0) // Wn
    sinks = sink_ref[...]
    first_half = _iota((G * Wn, LANES), 1) < hd
    side_mask = (first_half.astype(F32), 1.0 - first_half.astype(F32))
    ones_kv = jnp.ones((2 * Wn, LANES), BF16)

    for pair in range(SWA_KV_HEADS // 2):
        kl = slice(pair * LANES, (pair + 1) * LANES)
        kband = jnp.concatenate([kprev[:, kl], k[:, kl]], axis=0).astype(BF16)
        vband = jnp.concatenate([vprev[:, kl], v[:, kl]], axis=0).astype(BF16)
        qs = jnp.concatenate([q[:, (pair * G + t) * LANES:(pair * G + t + 1) * LANES] for t in range(G)], axis=0)
        outs = []
        for side in range(2):
            s = jnp.where(allowed, _mm_nt(qs * side_mask[side], kband), -jnp.inf)
            sink = jnp.zeros((G * Wn, 1), F32)
            for t in range(G):
                h = SWA_Q_ORDER[(pair * G + t) * 2 + side]
                sink = jnp.where(grp == t, sinks[:, h:h + 1], sink)
            mx = jnp.maximum(jnp.max(s, axis=-1, keepdims=True), sink)
            p = jnp.exp(s - mx).astype(BF16)
            den = jnp.dot(p, ones_kv, preferred_element_type=F32) + jnp.exp(sink - mx)
            outs.append(jnp.dot(p, vband, preferred_element_type=F32) / den)
        o = jnp.where(first_half, outs[0], outs[1])
        for t in range(G):
            o_ref[:, (pair * G + t) * LANES:(pair * G + t + 1) * LANES] = o[t * Wn:(t + 1) * Wn].astype(o_ref.dtype)

    kprev[...] = k
    vprev[...] = v


def _swa_mixer(p_swa, pos_col, prm, bsz, seq):
    m = p_swa.shape[0]
    Wn = SWA_WINDOW
    nb = seq // Wn
    head_ones = _head_ones(SWA_WIDTH, SWA_HEAD_DIM)
    params = [prm["freq"], prm["sgn_lo"], prm["sgn_hi"], prm["qn"], prm["kn"], prm["sinks"], head_ones]
    full = lambda a: pl.BlockSpec(a.shape, lambda b, t: (0,) * a.ndim)
    return pl.pallas_call(
        _swa_kernel,
        out_shape=jax.ShapeDtypeStruct((m, SWA_WIDTH), BF16),
        grid=(bsz, nb),
        in_specs=[pl.BlockSpec((Wn, SWA_COLS), lambda b, t: (b * nb + t, 0)),
                  pl.BlockSpec((Wn, 1), lambda b, t: (b * nb + t, 0))] + [full(a) for a in params],
        out_specs=pl.BlockSpec((Wn, SWA_WIDTH), lambda b, t: (b * nb + t, 0)),
        scratch_shapes=[pltpu.VMEM((Wn, SWA_KV_WIDTH), F32), pltpu.VMEM((Wn, SWA_KV_WIDTH), F32)],
        compiler_params=_cparams(("parallel", "arbitrary")),
        name="swa",
    )(p_swa, pos_col, *params)


def _merge_kernel(x_ref, gain_ref, orw_ref, ogdn_ref, oswa_ref, wg0_ref, wg1_ref, wg2_ref,
                  wb0_ref, wb1_ref, wb2_ref, wout_ref, o_ref, h_scr, acc):
    j = pl.program_id(1)

    @pl.when(j == 0)
    def _():
        x = x_ref[...]
        y = x * lax.rsqrt(jnp.mean(x * x, axis=-1, keepdims=True) + NORM_EPS) * gain_ref[...]
        h_scr[...] = y.astype(BF16)
        acc[...] = jnp.zeros_like(acc)

    h = h_scr[...]
    branch = lambda wg_ref, a_ref, wb_ref: (
        _sigmoid(jnp.dot(h, wg_ref[...], preferred_element_type=F32))
        * jnp.dot(a_ref[...], wb_ref[...], preferred_element_type=F32))
    merged = (branch(wg0_ref, orw_ref, wb0_ref) + branch(wg1_ref, ogdn_ref, wb1_ref)
              + branch(wg2_ref, oswa_ref, wb2_ref))
    acc[...] += jnp.dot(merged.astype(BF16), wout_ref[...], preferred_element_type=F32)

    @pl.when(j == pl.num_programs(1) - 1)
    def _():
        o_ref[...] = x_ref[...] + acc[...]


def _merge(x2, gain, o_rw, o_gdn, o_swa, w_gate, wb_rw, wb_gdn, wb_swa, w_out):
    m, d = x2.shape
    tm = _pick_tile(m, (512, 256, 128, 64, 32, 16, 8))
    tn = 512
    nj = d // tn
    rows = lambda width: pl.BlockSpec((tm, width), lambda i, j: (i, 0))
    gate = lambda b: pl.BlockSpec((d, tn), lambda i, j, b=b: (0, b * nj + j))
    wcol = lambda kdim: pl.BlockSpec((kdim, tn), lambda i, j: (0, j))
    return pl.pallas_call(
        _merge_kernel,
        out_shape=jax.ShapeDtypeStruct((m, d), F32),
        grid=(m // tm, nj),
        in_specs=[rows(d), pl.BlockSpec((1, d), lambda i, j: (0, 0)),
                  rows(RW_WIDTH), rows(GDN_WIDTH), rows(SWA_WIDTH), gate(0), gate(1), gate(2),
                  wcol(RW_WIDTH), wcol(GDN_WIDTH), wcol(SWA_WIDTH),
                  pl.BlockSpec((tn, d), lambda i, j: (j, 0))],
        out_specs=rows(d),
        scratch_shapes=[pltpu.VMEM((tm, d), BF16), pltpu.VMEM((tm, d), F32)],
        compiler_params=_cparams(("parallel", "arbitrary")),
        name="merge",
    )(x2, gain.reshape(1, d), o_rw, o_gdn, o_swa, w_gate, w_gate, w_gate, wb_rw, wb_gdn, wb_swa, w_out)


def _ffn_up_kernel(tm, seq, x_ref, halo_ref, gain_ref, wg_ref, wu_ref, cg_ref, cu_ref, act_ref, h_scr, u_scr):
    i = pl.program_id(0)
    j = pl.program_id(1)
    tn = wg_ref.shape[1]

    @pl.when(j == 0)
    def _():
        def norm(x):
            return (x * lax.rsqrt(jnp.mean(x * x, axis=-1, keepdims=True) + NORM_EPS) * gain_ref[...]).astype(BF16)
        seq_start = (i * tm) % seq == 0
        halo = jnp.where(seq_start, 0.0, halo_ref[...])
        h_scr[0:2 * SUBLANES, :] = norm(jnp.concatenate([jnp.zeros_like(halo), halo], axis=0))
        h_scr[2 * SUBLANES:, :] = norm(x_ref[...])

    h = h_scr[...]
    ts = FFN_SUBTILE
    for s in range(tn // ts):
        cols = slice(s * ts, (s + 1) * ts)

        def conv(w_ref, cw_ref, base):
            u_scr[:, base:base + ts] = jnp.dot(h, w_ref[:, cols], preferred_element_type=F32)
            out = jnp.zeros((tm, ts), F32)
            for kk in range(FFN_CONV):
                off = 2 * SUBLANES - (FFN_CONV - 1) + kk
                out = out + u_scr[pl.ds(off, tm), base:base + ts] * cw_ref[kk:kk + 1, cols]
            return out

        gate = conv(wg_ref, cg_ref, 2 * s * ts)
        up = conv(wu_ref, cu_ref, (2 * s + 1) * ts)
        act_ref[:, cols] = (gate * _sigmoid(gate) * up).astype(BF16)


def _ffn_down_kernel(act_ref, wd_ref, x_ref, o_ref):
    o_ref[...] = x_ref[...] + jnp.dot(act_ref[...], wd_ref[...], preferred_element_type=F32)


def _conv_ffn(x2, gain, w_up2, c_up2, w_down, seq):
    m, d = x2.shape
    dff = w_down.shape[0]
    tm = _pick_tile(seq, (1024, 512, 256, 128, 64, 32, 16))
    tn = _pick_tile(dff, (512, 256, 128))
    nj = dff // tn
    hb = tm // SUBLANES
    act = pl.pallas_call(
        functools.partial(_ffn_up_kernel, tm, seq),
        out_shape=jax.ShapeDtypeStruct((m, dff), BF16),
        grid=(m // tm, dff // tn),
        in_specs=[
            pl.BlockSpec((tm, d), lambda i, j: (i, 0)),
            pl.BlockSpec((SUBLANES, d), lambda i, j: (jnp.maximum(i * hb - 1, 0), 0)),
            pl.BlockSpec((1, d), lambda i, j: (0, 0)),
            pl.BlockSpec((d, tn), lambda i, j: (0, j)),
            pl.BlockSpec((d, tn), lambda i, j: (0, nj + j)),
            pl.BlockSpec((FFN_CONV, tn), lambda i, j: (0, j)),
            pl.BlockSpec((FFN_CONV, tn), lambda i, j: (0, nj + j)),
        ],
        out_specs=pl.BlockSpec((tm, tn), lambda i, j: (i, j)),
        scratch_shapes=[pltpu.VMEM((tm + 2 * SUBLANES, d), BF16),
                        pltpu.VMEM((tm + 2 * SUBLANES, 2 * tn), F32)],
        compiler_params=_cparams(("parallel", "arbitrary")),
        name="ffn_up",
    )(x2, x2, gain.reshape(1, d), w_up2, w_up2, c_up2, c_up2)
    tm2 = _pick_tile(m, (1024, 512, 256, 128, 64, 32, 16, 8))
    tn2 = _pick_tile(d, (512, 256, 128))
    return pl.pallas_call(
        _ffn_down_kernel,
        out_shape=jax.ShapeDtypeStruct((m, d), F32),
        grid=(m // tm2, d // tn2),
        in_specs=[
            pl.BlockSpec((tm2, dff), lambda i, j: (i, 0)),
            pl.BlockSpec((dff, tn2), lambda i, j: (0, j)),
            pl.BlockSpec((tm2, tn2), lambda i, j: (i, j)),
        ],
        out_specs=pl.BlockSpec((tm2, tn2), lambda i, j: (i, j)),
        compiler_params=_cparams(("parallel", "arbitrary")),
        name="ffn_down",
    )(act, w_down, x2)


def _row(a):
    return a.reshape(1, -1).astype(F32)


def _head_ones(width, head_dim):
    idx = np.arange(width) // head_dim
    return jnp.asarray(idx[:, None] == idx[None, :], BF16)


def _swa_head_order(a, axis):
    hd = SWA_HEAD_DIM
    take = lambda h: lax.slice_in_dim(a, h * hd, (h + 1) * hd, axis=axis)
    return jnp.concatenate([take(h) for h in SWA_Q_ORDER], axis=axis)


def _pad_cols(a, width):
    return jnp.pad(a, ((0, 0), (0, width - a.shape[1])))


def _layer_params(l, w_in, rw_mu, rw_w0, rw_w_up, rw_a0, rw_a_up, rw_g_up, rw_k_k, rw_k_a, rw_r_k, rw_ln_w,
                  rw_ln_b, vres_down, vres_mu, vres_v0, vres_v_up, gdn_conv, gdn_a_log, gdn_dt_bias, gdn_o_norm,
                  swa_q_norm, swa_k_norm, swa_sinks):
    d = w_in.shape[1]
    c_gdn = RW_COLS
    c_ba = c_gdn + 4 * GDN_WIDTH
    c_swa = c_ba + 2 * GDN_HEADS
    c_gate = c_swa + SWA_COLS
    has_vres = l > 0
    vres_w = vres_down[l - 1] if has_vres else jnp.zeros((d, RW_VRES_RANK), F32)
    assert c_ba % LANES == 0
    ba_w = _pack_cols(w_in, l, c_ba, LANES)[:, :2 * GDN_HEADS]
    slab_w = _pad_cols(jnp.concatenate([vres_w.astype(BF16), ba_w], axis=1), LANES)
    w_rw = jnp.concatenate([_pack_cols(w_in, l, 0, RW_COLS), slab_w], axis=1)
    w_gdn = _pack_cols(w_in, l, c_gdn, 4 * GDN_WIDTH)
    w_swa = _pack_cols(w_in, l, c_swa, SWA_COLS)
    w_swa = jnp.concatenate([_swa_head_order(w_swa[:, :SWA_WIDTH], axis=1), w_swa[:, SWA_WIDTH:]], axis=1)
    w_gate = _pack_cols(w_in, l, c_gate, w_in.shape[2] - c_gate)

    W = RW_WIDTH
    mu = jnp.concatenate([rw_mu[l], vres_mu[l - 1] if has_vres else jnp.zeros((RW_VRES_RANK,), F32),
                          jnp.zeros((LANES - RW_VRES_RANK,), F32)])
    wwa = jnp.zeros((LANES, 2 * W), F32)
    wwa = wwa.at[:RW_DECAY_RANK, :W].set(rw_w_up[l]).at[RW_DECAY_RANK:, W:].set(rw_a_up[l])
    rw = dict(mu=_row(mu), w0=_row(rw_w0[l]), a0=_row(rw_a0[l]), wwa=wwa.astype(BF16),
              gup=rw_g_up[l].astype(BF16), kk=_row(rw_k_k[l]), ka=_row(rw_k_a[l]), rk=_row(rw_r_k[l]),
              lnw=_row(rw_ln_w[l]), lnb=_row(rw_ln_b[l]), hones=_head_ones(W, RW_HEAD_DIM))
    if has_vres:
        vup = jnp.zeros((LANES, W), F32).at[:RW_VRES_RANK].set(vres_v_up[l - 1])
        rw.update(v0=_row(vres_v0[l - 1]), vup=vup.astype(BF16))

    lanes_pad = lambda a, off: jnp.zeros((1, LANES), F32).at[0, off:off + a.shape[0]].set(a)
    gdn = dict(cw=gdn_conv[l].astype(F32), alog=lanes_pad(gdn_a_log[l], SLAB_ALPHA0),
               dtb=lanes_pad(gdn_dt_bias[l], SLAB_ALPHA0), onorm=_row(gdn_o_norm[l]))

    half = ROPE_DIM // 2
    lane = np.arange(LANES) % SWA_HEAD_DIM
    inv_freq = ROPE_THETA ** (-(lane % half).astype(np.float32) * 2.0 / ROPE_DIM)
    freq = np.where(lane < ROPE_DIM, inv_freq, 0.0).astype(np.float32)
    sgn_lo = np.where(lane < half, -1.0, 0.0).astype(np.float32)
    sgn_hi = np.where((lane >= half) & (lane < ROPE_DIM), 1.0, 0.0).astype(np.float32)
    swa = dict(freq=jnp.asarray(freq).reshape(1, -1), sgn_lo=jnp.asarray(sgn_lo).reshape(1, -1),
               sgn_hi=jnp.asarray(sgn_hi).reshape(1, -1),
               qn=_row(jnp.tile(swa_q_norm[l], SWA_Q_HEADS)), kn=_row(jnp.tile(swa_k_norm[l], SWA_KV_HEADS)),
               sinks=lanes_pad(swa_sinks[l], 0))
    groups = (w_rw, w_gdn, w_swa)
    mix_widths = tuple(pl.cdiv(g.shape[1], INPROJ_TN) * INPROJ_TN for g in groups)
    w_mix = jnp.concatenate([_pad_cols(g, wd) for g, wd in zip(groups, mix_widths)], axis=1)
    return dict(w_mix=w_mix, mix_widths=mix_widths, w_gate=w_gate, rw=rw, gdn=gdn, swa=swa, has_vres=has_vres)


def kernel(x, positions, norm_mix, w_in, rw_mu, rw_w0, rw_w_up, rw_a0, rw_a_up, rw_g_up, rw_k_k, rw_k_a, rw_r_k, rw_ln_w, rw_ln_b, vres_down, vres_mu, vres_v0, vres_v_up, gdn_conv, gdn_a_log, gdn_dt_bias, gdn_o_norm, swa_q_norm, swa_k_norm, swa_sinks, w_branch, w_out, norm_ffn, ffn_up, ffn_conv, ffn_down):
    bsz, seq, d = x.shape
    depth = w_in.shape[0]
    x2 = x.reshape(bsz * seq, d)
    pos_col = positions.astype(F32).reshape(bsz * seq, 1)
    v_first = None
    for l in range(depth):
        lp = _layer_params(l, w_in, rw_mu, rw_w0, rw_w_up, rw_a0, rw_a_up, rw_g_up, rw_k_k, rw_k_a, rw_r_k,
                           rw_ln_w, rw_ln_b, vres_down, vres_mu, vres_v0, vres_v_up, gdn_conv, gdn_a_log,
                           gdn_dt_bias, gdn_o_norm, swa_q_norm, swa_k_norm, swa_sinks)
        p_rw, p_gdn, p_swa = _inproj(x2, norm_mix[l], lp["w_mix"], lp["mix_widths"])
        if lp["has_vres"]:
            o_rw, _ = _rwkv_mixer(p_rw, v_first, lp["rw"], bsz, seq, True)
        else:
            o_rw, v_first = _rwkv_mixer(p_rw, None, lp["rw"], bsz, seq, False)
        o_gdn = _gdn_mixer(p_gdn, p_rw, lp["gdn"], bsz, seq)
        o_swa = _swa_mixer(p_swa, pos_col, lp["swa"], bsz, seq)
        wb = w_branch[l].astype(BF16)
        x2 = _merge(x2, norm_mix[l], o_rw, o_gdn, o_swa, lp["w_gate"], wb[:RW_WIDTH], wb[RW_WIDTH:RW_WIDTH + GDN_WIDTH],
                    _swa_head_order(wb[RW_WIDTH + GDN_WIDTH:], axis=0), w_out[l].astype(BF16))
        x2 = _conv_ffn(x2, norm_ffn[l], ffn_up[l].astype(BF16), ffn_conv[l], ffn_down[l].astype(BF16), seq)
    return x2.reshape(bsz, seq, d)
```

```python
import functools

import jax
import jax.numpy as jnp
import numpy as np
from jax import lax
from jax.experimental import pallas as pl
from jax.experimental.pallas import tpu as pltpu

F32 = jnp.float32
BF16 = jnp.bfloat16

D_MODEL = 2048
RW_HEADS, RW_HEAD_DIM = 8, 64
RW_WIDTH = RW_HEADS * RW_HEAD_DIM
RW_DECAY_RANK, RW_ICLR_RANK, RW_GATE_RANK, RW_VRES_RANK = 64, 64, 128, 32
RW_GN_EPS = 64e-5
RW_COLS = 3 * RW_WIDTH + RW_DECAY_RANK + RW_ICLR_RANK + RW_GATE_RANK
GDN_HEADS, GDN_HEAD_DIM = 6, 128
GDN_WIDTH = GDN_HEADS * GDN_HEAD_DIM
GDN_CONV = 4
SWA_Q_HEADS, SWA_KV_HEADS, SWA_HEAD_DIM = 12, 4, 64
SWA_WIDTH = SWA_Q_HEADS * SWA_HEAD_DIM
SWA_KV_WIDTH = SWA_KV_HEADS * SWA_HEAD_DIM
SWA_WINDOW = 128
SWA_COLS = SWA_WIDTH + 2 * SWA_KV_WIDTH
ROPE_DIM = SWA_HEAD_DIM // 4
ROPE_THETA = 500000.0
D_FF = 5632
FFN_CONV = 3
NORM_EPS = 1e-6

LANES = 128
SUBLANES = 8
PACKED_ROWS = 16
VMEM_LIMIT_BYTES = 56 * 1024 * 1024

CHUNK = 64
INV_BLOCK = 16
FFN_SUBTILE = 512
SLAB_VRES0 = 0
SLAB_BETA0 = RW_VRES_RANK
SLAB_ALPHA0 = RW_VRES_RANK + GDN_HEADS
RW_GROUP = RW_COLS + LANES
_G = SWA_Q_HEADS // SWA_KV_HEADS
SWA_Q_ORDER = tuple((2 * p + side) * _G + t for p in range(SWA_KV_HEADS // 2) for t in range(_G) for side in range(2))

def _cparams(sem):
    return pltpu.CompilerParams(dimension_semantics=sem, vmem_limit_bytes=VMEM_LIMIT_BYTES)


def _dg(a, b, dims):
    return lax.dot_general(a.astype(BF16), b.astype(BF16), dims, preferred_element_type=F32)


def _mm(a, b):
    return _dg(a, b, (((1,), (0,)), ((), ())))


def _mm_nt(a, b):
    return _dg(a, b, (((1,), (1,)), ((), ())))


def _mm_tn(a, b):
    return _dg(a, b, (((0,), (0,)), ((), ())))


def _bmm(a, b):
    return _dg(a, b, (((2,), (1,)), ((0,), (0,))))


def _bmm_nt(a, b):
    return _dg(a, b, (((2,), (2,)), ((0,), (0,))))


def _bmm_tn(a, b):
    return _dg(a, b, (((1,), (1,)), ((0,), (0,))))


def _split3(a):
    hi = a.astype(BF16)
    r1 = a - hi.astype(F32)
    mid = r1.astype(BF16)
    lo = (r1 - mid.astype(F32)).astype(BF16)
    return hi, mid, lo


def _mm_exact_lhs(a_bf16, b):
    hi, mid, lo = _split3(b)
    d = lambda t: jnp.dot(a_bf16, t, preferred_element_type=F32)
    return d(hi) + d(mid) + d(lo)


def _mm_tn_exact_rhs(a, b_bf16):
    hi, mid, lo = _split3(a)
    d = lambda t: _mm_tn(t, b_bf16)
    return d(hi) + d(mid) + d(lo)


def _iota(shape, dim):
    return lax.broadcasted_iota(jnp.int32, shape, dim)


def _sigmoid(x):
    return 0.5 * jnp.tanh(0.5 * x) + 0.5


def _causal_taps(x, taps):
    k = taps.shape[0]
    acc = x * taps[k - 1:k, :]
    for s in range(1, k):
        acc = acc + pltpu.roll(x, s, 0) * taps[k - 1 - s:k - s, :]
    return acc[SUBLANES:, :]


def _neumann_inverse(lmat):
    assert CHUNK // INV_BLOCK == 4
    n = lmat.shape[-1]
    row, col = _iota((1, n, n), 1), _iota((1, n, n), 2)
    eye = (row == col).astype(F32)
    in_blk = (row // INV_BLOCK) == (col // INV_BLOCK)
    lb = jnp.where(in_blk, lmat, 0.0)
    e = lmat - lb
    inv = eye + lb
    p = _bmm(lb, lb)
    steps = INV_BLOCK.bit_length() - 2
    for s in range(steps):
        if s + 1 < steps:
            both = _bmm(jnp.concatenate([inv, p], axis=1), p)
            inv, p = inv + both[:, :n], both[:, n:]
        else:
            inv = inv + _bmm(inv, p)
    nmat = _bmm(inv, e)
    n2 = _bmm(nmat, nmat)
    m = eye + nmat + n2 + _bmm(nmat, n2)
    return _bmm(m, inv)


def _pack_kernel(shift, *refs):
    if shift == 0:
        a_ref, o_ref = refs
        o_ref[...] = a_ref[...].astype(BF16)
    else:
        a_ref, b_ref, o_ref = refs
        lane = _iota(a_ref.shape, 1)
        moved_a = pltpu.roll(a_ref[...], LANES - shift, 1)
        moved_b = pltpu.roll(b_ref[...], LANES - shift, 1)
        o_ref[...] = jnp.where(lane < LANES - shift, moved_a, moved_b).astype(BF16)


def _pack_cols(w3, layer, start, width):
    _, d, n = w3.shape
    q, shift = divmod(start, LANES)
    last_blk = pl.cdiv(n, LANES) - 1
    src = lambda off: pl.BlockSpec((None, d, LANES), lambda t: (layer, 0, jnp.minimum(q + t + off, last_blk)))
    ins = [w3] if shift == 0 else [w3, w3]
    return pl.pallas_call(
        functools.partial(_pack_kernel, shift),
        out_shape=jax.ShapeDtypeStruct((d, width), BF16),
        grid=(width // LANES,),
        in_specs=[src(0)] if shift == 0 else [src(0), src(1)],
        out_specs=pl.BlockSpec((d, LANES), lambda t: (0, t)),
        compiler_params=_cparams(("arbitrary",)),
        name="pack_cols",
    )(*ins)


INPROJ_TN = 512


def _inproj_kernel(starts, x_ref, g_ref, w_ref, *rest):
    outs, h_scr = rest[:-1], rest[-1]
    j = pl.program_id(1)

    @pl.when(j == 0)
    def _():
        x = x_ref[...]
        y = x * lax.rsqrt(jnp.mean(x * x, axis=-1, keepdims=True) + NORM_EPS) * g_ref[...]
        h_scr[...] = y.astype(BF16)

    for k, o_ref in enumerate(outs):
        @pl.when((j >= starts[k]) & (j < starts[k + 1]))
        def _(o_ref=o_ref):
            o_ref[...] = jnp.dot(h_scr[...], w_ref[...], preferred_element_type=F32).astype(o_ref.dtype)


def _pick_tile(n, prefs):
    for t in prefs:
        if n % t == 0:
            return t
    return n


def _inproj(x2, gain, w_bf16, widths):
    m, d = x2.shape
    tn = INPROJ_TN
    assert sum(widths) == w_bf16.shape[1] and all(w % tn == 0 for w in widths)
    tm = _pick_tile(m, (1024, 512, 256, 128, 64, 32, 16, 8))
    starts = tuple(int(s) for s in np.cumsum((0,) + tuple(w // tn for w in widths)))
    out_spec = lambda k: pl.BlockSpec(
        (tm, tn), lambda i, j: (i, jnp.clip(j - starts[k], 0, starts[k + 1] - starts[k] - 1)))
    return pl.pallas_call(
        functools.partial(_inproj_kernel, starts),
        out_shape=tuple(jax.ShapeDtypeStruct((m, w), BF16) for w in widths),
        grid=(m // tm, starts[-1]),
        in_specs=[
            pl.BlockSpec((tm, d), lambda i, j: (i, 0)),
            pl.BlockSpec((1, d), lambda i, j: (0, 0)),
            pl.BlockSpec((d, tn), lambda i, j: (0, j)),
        ],
        out_specs=tuple(out_spec(k) for k in range(len(widths))),
        scratch_shapes=[pltpu.VMEM((tm, d), BF16)],
        compiler_params=_cparams(("parallel", "arbitrary")),
        name="inproj",
    )(x2, gain.reshape(1, d), w_bf16)


def _rwkv_kernel(has_vres, tb, *refs):
    scr = refs[-14:]
    (carry, state, pl_scr, at_scr, rt_scr, bt_scr, kt_scr, v_scr, y_scr,
     t_scr, rb_scr, akv_scr, rkv_scr, btk_scr) = scr
    if has_vres:
        (p_ref, vf_ref, mu_ref, w0_ref, a0_ref, wwa_ref, gup_ref, kk_ref, ka_ref, rk_ref, lnw_ref, lnb_ref,
         hones_ref, v0_ref, vup_ref, o_ref) = refs[:-14]
    else:
        (p_ref, mu_ref, w0_ref, a0_ref, wwa_ref, gup_ref, kk_ref, ka_ref, rk_ref, lnw_ref, lnb_ref,
         hones_ref, o_ref, vout_ref) = refs[:-14]
    W = RW_WIDTH
    tstep = pl.program_id(1)

    @pl.when(tstep == 0)
    def _():
        carry[...] = jnp.zeros_like(carry)
        state[...] = jnp.zeros_like(state)

    row0 = _iota((tb, LANES), 0) == 0
    for c0 in range(0, RW_GROUP, LANES):
        x = p_ref[:, c0:c0 + LANES].astype(F32)
        prev = pltpu.roll(x, 1, 0)
        prev = jnp.where(row0, carry[PACKED_ROWS - 1:PACKED_ROWS, c0:c0 + LANES], prev)
        pl_scr[:, c0:c0 + LANES] = x + (prev - x) * mu_ref[:, c0:c0 + LANES]
    carry[...] = p_ref[tb - PACKED_ROWS:tb, :].astype(F32)

    head_sum = lambda t: _mm(t, hones_ref[...])

    c_wd = 3 * W
    x128 = pl_scr[:, c_wd:c_wd + LANES]
    lane = _iota((tb, LANES), 1)
    xin = jnp.where(lane < RW_DECAY_RANK, jnp.tanh(x128), x128)
    wa = jnp.dot(xin.astype(BF16), wwa_ref[...], preferred_element_type=F32)
    w_log = -jax.nn.softplus(-(w0_ref[...] + wa[:, :W])) - 0.5
    lw = -jnp.exp(w_log)
    asig = _sigmoid(a0_ref[...] + wa[:, W:])
    gd = pl_scr[:, c_wd + LANES:c_wd + 2 * LANES]
    g = jnp.dot(_sigmoid(gd).astype(BF16), gup_ref[...], preferred_element_type=F32)

    v = pl_scr[:, 2 * W:3 * W]
    if has_vres:
        vd = pl_scr[:, RW_COLS:RW_COLS + LANES]
        vl = jnp.dot(vd.astype(BF16), vup_ref[...], preferred_element_type=F32)
        v = v + (vf_ref[...] - v) * _sigmoid(v0_ref[...] + vl)
    else:
        vout_ref[...] = v
    v_scr[...] = v

    k = pl_scr[:, W:2 * W]
    kkraw = k * kk_ref[...]
    ssq = head_sum(kkraw * kkraw)
    kk = kkraw * lax.rsqrt(ssq + 1e-12)
    kfin = k * (1.0 + (asig - 1.0) * ka_ref[...])
    r = pl_scr[:, 0:W]
    bonus = head_sum(r * kfin * rk_ref[...]) * v

    C = CHUNK
    P2 = 2 * C
    NC, NP = tb // C, RW_HEADS // 2
    tril_b = (_iota((NC, C, C), 1) >= _iota((NC, C, C), 2)).astype(BF16)
    cum = sum(_bmm(tril_b, t.reshape(NC, C, W)) for t in _split3(lw)).reshape(tb, W)
    e_pos = jnp.exp(cum)
    e_neg = jnp.exp(-cum)
    rt_scr[...] = r * e_pos
    kt_scr[...] = kfin * e_neg
    bt_scr[...] = (kk * asig) * e_neg
    at_scr[...] = -kk * jnp.exp(cum - lw)

    def to_b(ref):
        return jnp.concatenate([ref[c * C:(c + 1) * C, p * LANES:(p + 1) * LANES][None]
                                for c in range(NC) for p in range(NP)], axis=0)

    row, col = _iota((1, P2, P2), 1), _iota((1, P2, P2), 2)
    same_head = (row // C) == (col // C)
    strict = same_head & ((row % C) > (col % C))
    incl = same_head & ((row % C) >= (col % C))
    left = _iota((1, C, LANES), 2) < RW_HEAD_DIM
    leftf = left.astype(F32)
    rightf = 1.0 - leftf

    def sel(x3):
        return jnp.where(left, x3[:, :C], x3[:, C:])

    at_b, rt_b, bt_b, kt_b, vp_b = to_b(at_scr), to_b(rt_scr), to_b(bt_scr), to_b(kt_scr), to_b(v_scr)
    lhs = jnp.concatenate([at_b * leftf, rt_b * leftf, at_b * rightf, rt_b * rightf], axis=1)
    out_ab = _bmm_nt(lhs, jnp.concatenate([bt_b, kt_b, kt_b, bt_b], axis=1))
    out_a, out_b = out_ab[:, :, :P2], out_ab[:, :, P2:]
    ab = jnp.where(strict, jnp.concatenate([out_a[:, 0:C], out_b[:, 2 * C:3 * C]], axis=1), 0.0)
    ak = jnp.where(strict, jnp.concatenate([out_b[:, 0:C], out_a[:, 2 * C:3 * C]], axis=1), 0.0)
    rb = jnp.where(incl, jnp.concatenate([out_a[:, C:2 * C], out_b[:, 3 * C:4 * C]], axis=1), 0.0)
    rk = jnp.where(incl, jnp.concatenate([out_b[:, C:2 * C], out_a[:, 3 * C:4 * C]], axis=1), 0.0)
    t_scr[...] = _neumann_inverse(ab)
    rb_scr[...] = rb
    vv = jnp.concatenate([vp_b, vp_b], axis=1)
    akv_scr[...] = sel(_bmm(ak, vv))
    rkv_scr[...] = sel(_bmm(rk, vv))
    pc_rows = jnp.concatenate([e_pos[(c + 1) * C - 1:(c + 1) * C, p * LANES:(p + 1) * LANES][None]
                               for c in range(NC) for p in range(NP)], axis=0)
    btk_scr[...] = jnp.concatenate([bt_b * pc_rows, kt_b * pc_rows], axis=1)

    for c in range(NC):
        items = slice(c * NP, (c + 1) * NP)
        s = state[...]
        a_c, r_c, v_c = at_b[items], rt_b[items], vp_b[items]
        ars = _bmm_nt(jnp.concatenate([a_c, r_c], axis=1), s)
        rhs = ars[:, :C] + akv_scr[items]
        u = sel(_bmm(t_scr[items], jnp.concatenate([rhs, rhs], axis=1)))
        y = ars[:, C:] + sel(_bmm(rb_scr[items], jnp.concatenate([u, u], axis=1))) + rkv_scr[items]
        s_new = s * pc_rows[items] + _bmm_tn(jnp.concatenate([u, v_c], axis=1), btk_scr[items])
        state[...] = jnp.where(same_head, s_new, 0.0)
        for p in range(NP):
            y_scr[c * C:(c + 1) * C, p * LANES:(p + 1) * LANES] = y[p]

    y = y_scr[...]
    inv_n = 1.0 / RW_HEAD_DIM
    y_hi = y.astype(BF16).astype(F32)
    mean = (head_sum(y_hi) + head_sum(y - y_hi)) * inv_n
    yc = y - mean
    var = head_sum(yc * yc) * inv_n
    yn = yc * lax.rsqrt(var + RW_GN_EPS) * lnw_ref[...] + lnb_ref[...]
    o_ref[...] = ((yn + bonus) * g).astype(o_ref.dtype)


def _rwkv_mixer(p_rw, v_first, prm, bsz, seq, has_vres):
    m = p_rw.shape[0]
    tb = _pick_tile(seq, (512, 256, 128, 64))
    nt = seq // tb
    nb = (tb // CHUNK) * (RW_HEADS // 2)
    W = RW_WIDTH
    row_spec = lambda width: pl.BlockSpec((tb, width), lambda b, t: (b * nt + t, 0))
    full = lambda a: pl.BlockSpec(a.shape, lambda b, t: (0,) * a.ndim)
    names = (["mu", "w0", "a0", "wwa", "gup", "kk", "ka", "rk", "lnw", "lnb", "hones"]
             + (["v0", "vup"] if has_vres else []))
    params = [prm[n] for n in names]
    ins = [p_rw] + ([v_first] if has_vres else []) + params
    in_specs = [row_spec(RW_GROUP)] + ([row_spec(W)] if has_vres else []) + [full(a) for a in params]
    if has_vres:
        out_shape = jax.ShapeDtypeStruct((m, W), BF16)
        out_specs = row_spec(W)
    else:
        out_shape = (jax.ShapeDtypeStruct((m, W), BF16), jax.ShapeDtypeStruct((m, W), F32))
        out_specs = (row_spec(W), row_spec(W))
    scratch = [
        pltpu.VMEM((PACKED_ROWS, RW_GROUP), F32),
        pltpu.VMEM((RW_HEADS // 2, LANES, LANES), F32),
        pltpu.VMEM((tb, RW_GROUP), F32),
    ] + [pltpu.VMEM((tb, W), F32) for _ in range(6)] + [
        pltpu.VMEM((nb, 2 * CHUNK, 2 * CHUNK), F32),
        pltpu.VMEM((nb, 2 * CHUNK, 2 * CHUNK), F32),
        pltpu.VMEM((nb, CHUNK, LANES), F32),
        pltpu.VMEM((nb, CHUNK, LANES), F32),
        pltpu.VMEM((nb, 2 * CHUNK, LANES), F32),
    ]
    res = pl.pallas_call(
        functools.partial(_rwkv_kernel, has_vres, tb),
        out_shape=out_shape,
        grid=(bsz, nt),
        in_specs=in_specs,
        out_specs=out_specs,
        scratch_shapes=scratch,
        compiler_params=_cparams(("parallel", "arbitrary")),
        name="rwkv7_vres" if has_vres else "rwkv7",
    )(*ins)
    if has_vres:
        return res, None
    return res


def _gdn_kernel(tb, pg_ref, slab_ref, cw_ref, alog_ref, dtb_ref, onorm_ref, eb_ref, ea_ref, o_ref,
                carry, state, ext, q_scr, k_scr, v_scr, grow_scr, o_scr,
                gcf_scr, kb_scr, vb_scr, kbeg_scr, qeg_scr, kt2_scr,
                intra_scr, u_scr, wq_scr, kt_scr):
    H, Dh, Wd = GDN_HEADS, GDN_HEAD_DIM, GDN_WIDTH
    QKV = 3 * Wd
    tstep = pl.program_id(1)

    @pl.when(tstep == 0)
    def _():
        carry[...] = jnp.zeros_like(carry)
        state[...] = jnp.zeros_like(state)

    ext[0:PACKED_ROWS, :] = carry[...]
    ext[PACKED_ROWS:, :] = pg_ref[:, 0:QKV].astype(F32)
    carry[...] = pg_ref[tb - PACKED_ROWS:tb, 0:QKV].astype(F32)
    for j in range(QKV // Dh):
        ln = slice(j * Dh, (j + 1) * Dh)
        acc = _causal_taps(ext[pl.ds(PACKED_ROWS - SUBLANES, tb + SUBLANES), ln], cw_ref[:, ln])
        act = acc * _sigmoid(acc)
        which, h = divmod(j, H)
        hl = slice(h * Dh, (h + 1) * Dh)
        if which == 0:
            nrm = lax.rsqrt(jnp.sum(act * act, axis=-1, keepdims=True) + 1e-6)
            q_scr[:, hl] = act * nrm * (Dh ** -0.5)
        elif which == 1:
            nrm = lax.rsqrt(jnp.sum(act * act, axis=-1, keepdims=True) + 1e-6)
            k_scr[:, hl] = act * nrm
        else:
            v_scr[:, hl] = act

    slab = slab_ref[...].astype(F32)
    gsl =-jnp.exp(alog_ref[...]) * jax.nn.softplus(slab + dtb_ref[...])
    lane = _iota((tb, LANES), 1)
    gsl = jnp.where((lane >= SLAB_ALPHA0) & (lane < SLAB_ALPHA0 + H), gsl, 0.0)

    C = CHUNK
    NC = tb // C
    trow, tcol = _iota((tb, tb), 0), _iota((tb, tb), 1)
    same_chunk = (trow // C) == (tcol // C)
    gcol = _mm_exact_lhs((same_chunk & (trow >= tcol)).astype(BF16), gsl)
    grow_scr[...] = _mm_tn_exact_rhs(gsl, (same_chunk & (trow <= tcol)).astype(BF16))

    def spread(x, e_ref):
        return sum(_mm(t, e_ref[...]) for t in _split3(x))

    beta_f = spread(_sigmoid(slab), eb_ref)
    gc_f = spread(gcol, ea_ref)
    eg_f = jnp.exp(gc_f)
    k2 = k_scr[...]
    kb2 = k2 * beta_f
    gc3 = gc_f.reshape(NC, C, Wd)
    gl3 = gc3[:, C - 1:C, :]
    gcf_scr[...] = gc_f
    kb_scr[...] = kb2
    vb_scr[...] = v_scr[...] * beta_f
    kbeg_scr[...] = kb2 * eg_f
    qeg_scr[...] = q_scr[...] * eg_f
    kt2_scr[...] = (k2.reshape(NC, C, Wd) * jnp.exp(gl3 - gc3)).reshape(tb, Wd)
    egl3 = jnp.exp(gl3)

    def items(fn):
        return jnp.concatenate([fn(c, h)[None] for c in range(NC) for h in range(H)], axis=0)

    rows = lambda c: slice(c * C, (c + 1) * C)
    head = lambda h: slice(h * Dh, (h + 1) * Dh)
    tile = lambda ref: items(lambda c, h: ref[rows(c), head(h)])
    gci = items(lambda c, h: gcf_scr[rows(c), h * Dh:h * Dh + C])
    gr = items(lambda c, h: grow_scr[SLAB_ALPHA0 + h:SLAB_ALPHA0 + h + 1, rows(c)])
    eg_last = jnp.concatenate([egl3[c:c + 1, :, head(h)] for c in range(NC) for h in range(H)], axis=0)
    k_b = tile(k_scr)
    row, col = _iota((1, C, C), 1), _iota((1, C, C), 2)
    dec = jnp.exp(jnp.where(row >= col, gci - gr, -jnp.inf))
    prod = _bmm_nt(jnp.concatenate([tile(kb_scr), tile(q_scr)], axis=1), k_b)
    lmat = jnp.where(row > col, prod[:, :C] * dec, 0.0)
    intra_scr[...] = prod[:, C:] * dec
    nb = NC * H
    l2 = (-lmat).reshape(nb // 2, 2 * C, C)
    prow, pcol = _iota((1, 2 * C, 2 * C), 1), _iota((1, 2 * C, 2 * C), 2)
    lp = jnp.where((prow // C) == (pcol // C), jnp.concatenate([l2, l2], axis=2), 0.0)
    tp = _neumann_inverse(lp)
    rhs = jnp.concatenate([tile(vb_scr), tile(kbeg_scr)], axis=2).reshape(nb // 2, 2 * C, 2 * Dh)
    sol = _bmm(tp, rhs).reshape(nb, C, 2 * Dh)
    u_scr[...] = sol[:, :, :Dh]
    wq_scr[...] = jnp.concatenate([sol[:, :, Dh:], tile(qeg_scr)], axis=1)
    kt_scr[...] = tile(kt2_scr)

    for c in range(NC):
        it = slice(c * H, (c + 1) * H)
        s = state[...]
        ws = _bmm(wq_scr[it], s)
        v_new = u_scr[it] - ws[:, :C]
        o = ws[:, C:] + _bmm(intra_scr[it], v_new)
        state[...] = s * eg_last[it] + _bmm_tn(kt_scr[it], v_new)
        for h in range(H):
            o_scr[rows(c), head(h)] = o[h]

    for h in range(H):
        hl = slice(h * Dh, (h + 1) * Dh)
        o = o_scr[:, hl]
        o = o * lax.rsqrt(jnp.mean(o * o, axis=-1, keepdims=True) + NORM_EPS) * onorm_ref[...]
        z = pg_ref[:, QKV + h * Dh:QKV + (h + 1) * Dh].astype(F32)
        o_ref[:, hl] = (o * (z * _sigmoid(z))).astype(o_ref.dtype)


def _gdn_mixer(p_gdn, p_rw, prm, bsz, seq):
    m = p_gdn.shape[0]
    tb = _pick_tile(seq, (256, 128))
    nt = seq // tb
    nb = (tb // CHUNK) * GDN_HEADS
    Wd = GDN_WIDTH
    lane_head = np.arange(Wd)[None, :] // GDN_HEAD_DIM
    spread = lambda off: jnp.asarray(np.arange(LANES)[:, None] - off == lane_head, BF16)
    params = [prm["cw"], prm["alog"], prm["dtb"], prm["onorm"], spread(SLAB_BETA0), spread(SLAB_ALPHA0)]
    full = lambda a: pl.BlockSpec(a.shape, lambda b, t: (0,) * a.ndim)
    slab_blk = RW_COLS // LANES
    return pl.pallas_call(
        functools.partial(_gdn_kernel, tb),
        out_shape=jax.ShapeDtypeStruct((m, Wd), BF16),
        grid=(bsz, nt),
        in_specs=[pl.BlockSpec((tb, 4 * Wd), lambda b, t: (b * nt + t, 0)),
                  pl.BlockSpec((tb, LANES), lambda b, t: (b * nt + t, slab_blk))] + [full(a) for a in params],
        out_specs=pl.BlockSpec((tb, Wd), lambda b, t: (b * nt + t, 0)),
        scratch_shapes=[
            pltpu.VMEM((PACKED_ROWS, 3 * Wd), F32),
            pltpu.VMEM((GDN_HEADS, GDN_HEAD_DIM, GDN_HEAD_DIM), F32),
            pltpu.VMEM((tb + PACKED_ROWS, 3 * Wd), F32),
            pltpu.VMEM((tb, Wd), F32), pltpu.VMEM((tb, Wd), F32), pltpu.VMEM((tb, Wd), F32),
            pltpu.VMEM((LANES, tb), F32),
            pltpu.VMEM((tb, Wd), F32),
        ] + [pltpu.VMEM((tb, Wd), F32) for _ in range(6)] + [
            pltpu.VMEM((nb, CHUNK, CHUNK), F32),
            pltpu.VMEM((nb, CHUNK, GDN_HEAD_DIM), F32),
            pltpu.VMEM((nb, 2 * CHUNK, GDN_HEAD_DIM), F32),
            pltpu.VMEM((nb, CHUNK, GDN_HEAD_DIM), F32),
        ],
        compiler_params=_cparams(("parallel", "arbitrary")),
        name="gdn",
    )(p_gdn, p_rw, *params)


def _swa_kernel(nblk, ps_ref, pos_ref, freq_ref, sgn_lo_ref, sgn_hi_ref, qn_ref, kn_ref, sink_ref, ones_ref, o_ref,
                kprev, vprev):
    Wn = SWA_WINDOW
    hd = SWA_HEAD_DIM
    G = SWA_Q_HEADS // SWA_KV_HEADS
    QW, KW = SWA_WIDTH, SWA_KV_WIDTH
    n = pl.program_id(1)

    @pl.when(n == 0)
    def _():
        kprev[...] = jnp.zeros_like(kprev)
        vprev[...] = jnp.zeros_like(vprev)

    half = ROPE_DIM // 2
    ang = pos_ref[...] * freq_ref[...]
    cs1, sn1 = jnp.cos(ang), jnp.sin(ang)
    lo1, hi1 = sn1 * sgn_lo_ref[...], sn1 * sgn_hi_ref[...]

    def norm_rope(x, gain_row, width):
        rep = lambda t: jnp.concatenate([t] * (width // LANES), axis=1)
        ms = _mm(x * x, ones_ref[0:width, 0:width]) * (1.0 / hd)
        y = x * lax.rsqrt(ms + NORM_EPS) * gain_row
        up = pltpu.roll(y, width - half, 1)
        dn = pltpu.roll(y, half, 1)
        return y * rep(cs1) + up * rep(lo1) + dn * rep(hi1)

    q = norm_rope(ps_ref[:, 0:QW].astype(F32), qn_ref[...], QW) * (hd ** -0.5)
    k = norm_rope(ps_ref[:, QW:QW + KW].astype(F32), kn_ref[...], KW)
    v = ps_ref[:, QW + KW:QW + 2 * KW].astype(F32)

    qi = _iota((G * Wn, 2 * Wn), 0) % Wn
    kj = _iota((G * Wn, 2 * Wn), 1)
    rel = qi + Wn - kj
    in_window = (rel >= 0) & (rel < SWA_WINDOW)
    grp = _iota((G * Wn, 1), 0) // Wn
    sinks = sink_ref[...]
    first_half = _iota((G * Wn, LANES), 1) < hd
    side_mask = (first_half.astype(F32), 1.0 - first_half.astype(F32))
    ones_kv = jnp.ones((2 * Wn, LANES), BF16)

    for blk in range(nblk):
        rows = slice(blk * Wn, (blk + 1) * Wn)
        before = slice((blk - 1) * Wn, blk * Wn)
        allowed = in_window & ((kj >= Wn) | (n > 0)) if blk == 0 else in_window
        for pair in range(SWA_KV_HEADS // 2):
            kl = slice(pair * LANES, (pair + 1) * LANES)
            k_before = kprev[:, kl] if blk == 0 else k[before, kl]
            v_before = vprev[:, kl] if blk == 0 else v[before, kl]
            kband = jnp.concatenate([k_before, k[rows, kl]], axis=0).astype(BF16)
            vband = jnp.concatenate([v_before, v[rows, kl]], axis=0).astype(BF16)
            qs = jnp.concatenate([q[rows, (pair * G + t) * LANES:(pair * G + t + 1) * LANES] for t in range(G)],
                                 axis=0)
            outs = []
            for side in range(2):
                s = jnp.where(allowed, _mm_nt(qs * side_mask[side], kband), -jnp.inf)
                sink = jnp.zeros((G * Wn, 1), F32)
                for t in range(G):
                    h = SWA_Q_ORDER[(pair * G + t) * 2 + side]
                    sink = jnp.where(grp == t, sinks[:, h:h + 1], sink)
                mx = jnp.maximum(jnp.max(s, axis=-1, keepdims=True), sink)
                p = jnp.exp(s - mx).astype(BF16)
                den = jnp.dot(p, ones_kv, preferred_element_type=F32) + jnp.exp(sink - mx)
                outs.append(jnp.dot(p, vband, preferred_element_type=F32) / den)
            o = jnp.where(first_half, outs[0], outs[1])
            for t in range(G):
                o_ref[rows, (pair * G + t) * LANES:(pair * G + t + 1) * LANES] = (
                    o[t * Wn:(t + 1) * Wn].astype(o_ref.dtype))

    last = slice((nblk - 1) * Wn, nblk * Wn)
    kprev[...] = k[last, :]
    vprev[...] = v[last, :]


def _swa_mixer(p_swa, pos_col, prm, bsz, seq):
    m = p_swa.shape[0]
    Wn = SWA_WINDOW
    nblk = 2 if seq % (2 * Wn) == 0 else 1
    rows = nblk * Wn
    nb = seq // rows
    head_ones = _head_ones(SWA_WIDTH, SWA_HEAD_DIM)
    params = [prm["freq"], prm["sgn_lo"], prm["sgn_hi"], prm["qn"], prm["kn"], prm["sinks"], head_ones]
    full = lambda a: pl.BlockSpec(a.shape, lambda b, t: (0,) * a.ndim)
    return pl.pallas_call(
        functools.partial(_swa_kernel, nblk),
        out_shape=jax.ShapeDtypeStruct((m, SWA_WIDTH), BF16),
        grid=(bsz, nb),
        in_specs=[pl.BlockSpec((rows, SWA_COLS), lambda b, t: (b * nb + t, 0)),
                  pl.BlockSpec((rows, 1), lambda b, t: (b * nb + t, 0))] + [full(a) for a in params],
        out_specs=pl.BlockSpec((rows, SWA_WIDTH), lambda b, t: (b * nb + t, 0)),
        scratch_shapes=[pltpu.VMEM((Wn, SWA_KV_WIDTH), F32), pltpu.VMEM((Wn, SWA_KV_WIDTH), F32)],
        compiler_params=_cparams(("parallel", "arbitrary")),
        name="swa",
    )(p_swa, pos_col, *params)


def _merge_kernel(x_ref, gain_ref, orw_ref, ogdn_ref, oswa_ref, wg0_ref, wg1_ref, wg2_ref,
                  wb0_ref, wb1_ref, wb2_ref, wout_ref, o_ref, h_scr, acc):
    j = pl.program_id(1)

    @pl.when(j == 0)
    def _():
        x = x_ref[...]
        y = x * lax.rsqrt(jnp.mean(x * x, axis=-1, keepdims=True) + NORM_EPS) * gain_ref[...]
        h_scr[...] = y.astype(BF16)
        acc[...] = jnp.zeros_like(acc)

    h = h_scr[...]
    branch = lambda wg_ref, a_ref, wb_ref: (
        _sigmoid(jnp.dot(h, wg_ref[...], preferred_element_type=F32))
        * jnp.dot(a_ref[...], wb_ref[...], preferred_element_type=F32))
    merged = (branch(wg0_ref, orw_ref, wb0_ref) + branch(wg1_ref, ogdn_ref, wb1_ref)
              + branch(wg2_ref, oswa_ref, wb2_ref))
    acc[...] += jnp.dot(merged.astype(BF16), wout_ref[...], preferred_element_type=F32)

    @pl.when(j == pl.num_programs(1) - 1)
    def _():
        o_ref[...] = x_ref[...] + acc[...]


def _merge(x2, gain, o_rw, o_gdn, o_swa, w_gate, wb_rw, wb_gdn, wb_swa, w_out):
    m, d = x2.shape
    tm = _pick_tile(m, (512, 256, 128, 64, 32, 16, 8))
    tn = 512
    nj = d // tn
    rows = lambda width: pl.BlockSpec((tm, width), lambda i, j: (i, 0))
    gate = lambda b: pl.BlockSpec((d, tn), lambda i, j, b=b: (0, b * nj + j))
    wcol = lambda kdim: pl.BlockSpec((kdim, tn), lambda i, j: (0, j))
    return pl.pallas_call(
        _merge_kernel,
        out_shape=jax.ShapeDtypeStruct((m, d), F32),
        grid=(m // tm, nj),
        in_specs=[rows(d), pl.BlockSpec((1, d), lambda i, j: (0, 0)),
                  rows(RW_WIDTH), rows(GDN_WIDTH), rows(SWA_WIDTH), gate(0), gate(1), gate(2),
                  wcol(RW_WIDTH), wcol(GDN_WIDTH), wcol(SWA_WIDTH),
                  pl.BlockSpec((tn, d), lambda i, j: (j, 0))],
        out_specs=rows(d),
        scratch_shapes=[pltpu.VMEM((tm, d), BF16), pltpu.VMEM((tm, d), F32)],
        compiler_params=_cparams(("parallel", "arbitrary")),
        name="merge",
    )(x2, gain.reshape(1, d), o_rw, o_gdn, o_swa, w_gate, w_gate, w_gate, wb_rw, wb_gdn, wb_swa, w_out)


def _ffn_up_kernel(tm, seq, x_ref, halo_ref, gain_ref, wg_ref, wu_ref, cg_ref, cu_ref, act_ref, h_scr, u_scr):
    i = pl.program_id(0)
    j = pl.program_id(1)
    tn = wg_ref.shape[1]

    @pl.when(j == 0)
    def _():
        def norm(x):
            return (x * lax.rsqrt(jnp.mean(x * x, axis=-1, keepdims=True) + NORM_EPS) * gain_ref[...]).astype(BF16)
        seq_start = (i * tm) % seq == 0
        halo = jnp.where(seq_start, 0.0, halo_ref[...])
        h_scr[0:2 * SUBLANES, :] = norm(jnp.concatenate([jnp.zeros_like(halo), halo], axis=0))
        h_scr[2 * SUBLANES:, :] = norm(x_ref[...])

    h = h_scr[...]
    ts = FFN_SUBTILE
    for s in range(tn // ts):
        cols = slice(s * ts, (s + 1) * ts)

        def conv(w_ref, cw_ref, base):
            u_scr[:, base:base + ts] = jnp.dot(h, w_ref[:, cols], preferred_element_type=F32)
            return _causal_taps(u_scr[pl.ds(SUBLANES, tm + SUBLANES), base:base + ts], cw_ref[:, cols])

        gate = conv(wg_ref, cg_ref, 2 * s * ts)
        up = conv(wu_ref, cu_ref, (2 * s + 1) * ts)
        act_ref[:, cols] = (gate * _sigmoid(gate) * up).astype(BF16)


def _ffn_down_kernel(act_ref, wd_ref, x_ref, o_ref):
    o_ref[...] = x_ref[...] + jnp.dot(act_ref[...], wd_ref[...], preferred_element_type=F32)


def _conv_ffn(x2, gain, w_up2, c_up2, w_down, seq):
    m, d = x2.shape
    dff = w_down.shape[0]
    tm = _pick_tile(seq, (1024, 512, 256, 128, 64, 32, 16))
    tn = _pick_tile(dff, (512, 256, 128))
    nj = dff // tn
    hb = tm // SUBLANES
    act = pl.pallas_call(
        functools.partial(_ffn_up_kernel, tm, seq),
        out_shape=jax.ShapeDtypeStruct((m, dff), BF16),
        grid=(m // tm, dff // tn),
        in_specs=[
            pl.BlockSpec((tm, d), lambda i, j: (i, 0)),
            pl.BlockSpec((SUBLANES, d), lambda i, j: (jnp.maximum(i * hb - 1, 0), 0)),
            pl.BlockSpec((1, d), lambda i, j: (0, 0)),
            pl.BlockSpec((d, tn), lambda i, j: (0, j)),
            pl.BlockSpec((d, tn), lambda i, j: (0, nj + j)),
            pl.BlockSpec((FFN_CONV, tn), lambda i, j: (0, j)),
            pl.BlockSpec((FFN_CONV, tn), lambda i, j: (0, nj + j)),
        ],
        out_specs=pl.BlockSpec((tm, tn), lambda i, j: (i, j)),
        scratch_shapes=[pltpu.VMEM((tm + 2 * SUBLANES, d), BF16),
                        pltpu.VMEM((tm + 2 * SUBLANES, 2 * tn), F32)],
        compiler_params=_cparams(("parallel", "arbitrary")),
        name="ffn_up",
    )(x2, x2, gain.reshape(1, d), w_up2, w_up2, c_up2, c_up2)
    tm2 = _pick_tile(m, (1024, 512, 256, 128, 64, 32, 16, 8))
    tn2 = _pick_tile(d, (512, 256, 128))
    return pl.pallas_call(
        _ffn_down_kernel,
        out_shape=jax.ShapeDtypeStruct((m, d), F32),
        grid=(m // tm2, d // tn2),
        in_specs=[
            pl.BlockSpec((tm2, dff), lambda i, j: (i, 0)),
            pl.BlockSpec((dff, tn2), lambda i, j: (0, j)),
            pl.BlockSpec((tm2, tn2), lambda i, j: (i, j)),
        ],
        out_specs=pl.BlockSpec((tm2, tn2), lambda i, j: (i, j)),
        compiler_params=_cparams(("parallel", "arbitrary")),
        name="ffn_down",
    )(act, w_down, x2)


def _row(a):
    return a.reshape(1, -1).astype(F32)


def _head_ones(width, head_dim):
    idx = np.arange(width) // head_dim
    return jnp.asarray(idx[:, None] == idx[None, :], BF16)


def _swa_head_order(a, axis):
    hd = SWA_HEAD_DIM
    take = lambda h: lax.slice_in_dim(a, h * hd, (h + 1) * hd, axis=axis)
    return jnp.concatenate([take(h) for h in SWA_Q_ORDER], axis=axis)


def _pad_cols(a, width):
    return jnp.pad(a, ((0, 0), (0, width - a.shape[1])))


def _layer_params(l, w_in, rw_mu, rw_w0, rw_w_up, rw_a0, rw_a_up, rw_g_up, rw_k_k, rw_k_a, rw_r_k, rw_ln_w,
                  rw_ln_b, vres_down, vres_mu, vres_v0, vres_v_up, gdn_conv, gdn_a_log, gdn_dt_bias, gdn_o_norm,
                  swa_q_norm, swa_k_norm, swa_sinks):
    d = w_in.shape[1]
    c_gdn = RW_COLS
    c_ba = c_gdn + 4 * GDN_WIDTH
    c_swa = c_ba + 2 * GDN_HEADS
    c_gate = c_swa + SWA_COLS
    has_vres = l > 0
    vres_w = vres_down[l - 1] if has_vres else jnp.zeros((d, RW_VRES_RANK), F32)
    assert c_ba % LANES == 0
    ba_w = _pack_cols(w_in, l, c_ba, LANES)[:, :2 * GDN_HEADS]
    slab_w = _pad_cols(jnp.concatenate([vres_w.astype(BF16), ba_w], axis=1), LANES)
    w_rw = jnp.concatenate([_pack_cols(w_in, l, 0, RW_COLS), slab_w], axis=1)
    w_gdn = _pack_cols(w_in, l, c_gdn, 4 * GDN_WIDTH)
    w_swa = _pack_cols(w_in, l, c_swa, SWA_COLS)
    w_swa = jnp.concatenate([_swa_head_order(w_swa[:, :SWA_WIDTH], axis=1), w_swa[:, SWA_WIDTH:]], axis=1)
    w_gate = _pack_cols(w_in, l, c_gate, w_in.shape[2] - c_gate)

    W = RW_WIDTH
    mu = jnp.concatenate([rw_mu[l], vres_mu[l - 1] if has_vres else jnp.zeros((RW_VRES_RANK,), F32),
                          jnp.zeros((LANES - RW_VRES_RANK,), F32)])
    wwa = jnp.zeros((LANES, 2 * W), F32)
    wwa = wwa.at[:RW_DECAY_RANK, :W].set(rw_w_up[l]).at[RW_DECAY_RANK:, W:].set(rw_a_up[l])
    rw = dict(mu=_row(mu), w0=_row(rw_w0[l]), a0=_row(rw_a0[l]), wwa=wwa.astype(BF16),
              gup=rw_g_up[l].astype(BF16), kk=_row(rw_k_k[l]), ka=_row(rw_k_a[l]), rk=_row(rw_r_k[l]),
              lnw=_row(rw_ln_w[l]), lnb=_row(rw_ln_b[l]), hones=_head_ones(W, RW_HEAD_DIM))
    if has_vres:
        vup = jnp.zeros((LANES, W), F32).at[:RW_VRES_RANK].set(vres_v_up[l - 1])
        rw.update(v0=_row(vres_v0[l - 1]), vup=vup.astype(BF16))

    lanes_pad = lambda a, off: jnp.zeros((1, LANES), F32).at[0, off:off + a.shape[0]].set(a)
    gdn = dict(cw=gdn_conv[l].astype(F32), alog=lanes_pad(gdn_a_log[l], SLAB_ALPHA0),
               dtb=lanes_pad(gdn_dt_bias[l], SLAB_ALPHA0), onorm=_row(gdn_o_norm[l]))

    half = ROPE_DIM // 2
    lane = np.arange(LANES) % SWA_HEAD_DIM
    inv_freq = ROPE_THETA ** (-(lane % half).astype(np.float32) * 2.0 / ROPE_DIM)
    freq = np.where(lane < ROPE_DIM, inv_freq, 0.0).astype(np.float32)
    sgn_lo = np.where(lane < half, -1.0, 0.0).astype(np.float32)
    sgn_hi = np.where((lane >= half) & (lane < ROPE_DIM), 1.0, 0.0).astype(np.float32)
    swa = dict(freq=jnp.asarray(freq).reshape(1, -1), sgn_lo=jnp.asarray(sgn_lo).reshape(1, -1),
               sgn_hi=jnp.asarray(sgn_hi).reshape(1, -1),
               qn=_row(jnp.tile(swa_q_norm[l], SWA_Q_HEADS)), kn=_row(jnp.tile(swa_k_norm[l], SWA_KV_HEADS)),
               sinks=lanes_pad(swa_sinks[l], 0))
    groups = (w_rw, w_gdn, w_swa)
    mix_widths = tuple(pl.cdiv(g.shape[1], INPROJ_TN) * INPROJ_TN for g in groups)
    w_mix = jnp.concatenate([_pad_cols(g, wd) for g, wd in zip(groups, mix_widths)], axis=1)
    return dict(w_mix=w_mix, mix_widths=mix_widths, w_gate=w_gate, rw=rw, gdn=gdn, swa=swa, has_vres=has_vres)


def kernel(x, positions, norm_mix, w_in, rw_mu, rw_w0, rw_w_up, rw_a0, rw_a_up, rw_g_up, rw_k_k, rw_k_a, rw_r_k, rw_ln_w, rw_ln_b, vres_down, vres_mu, vres_v0, vres_v_up, gdn_conv, gdn_a_log, gdn_dt_bias, gdn_o_norm, swa_q_norm, swa_k_norm, swa_sinks, w_branch, w_out, norm_ffn, ffn_up, ffn_conv, ffn_down):
    bsz, seq, d = x.shape
    depth = w_in.shape[0]
    x2 = x.reshape(bsz * seq, d)
    pos_col = positions.astype(F32).reshape(bsz * seq, 1)
    v_first = None
    for l in range(depth):
        lp = _layer_params(l, w_in, rw_mu, rw_w0, rw_w_up, rw_a0, rw_a_up, rw_g_up, rw_k_k, rw_k_a, rw_r_k,
                           rw_ln_w, rw_ln_b, vres_down, vres_mu, vres_v0, vres_v_up, gdn_conv, gdn_a_log,
                           gdn_dt_bias, gdn_o_norm, swa_q_norm, swa_k_norm, swa_sinks)
        p_rw, p_gdn, p_swa = _inproj(x2, norm_mix[l], lp["w_mix"], lp["mix_widths"])
        if lp["has_vres"]:
            o_rw, _ = _rwkv_mixer(p_rw, v_first, lp["rw"], bsz, seq, True)
        else:
            o_rw, v_first = _rwkv_mixer(p_rw, None, lp["rw"], bsz, seq, False)
        o_gdn = _gdn_mixer(p_gdn, p_rw, lp["gdn"], bsz, seq)
        o_swa = _swa_mixer(p_swa, pos_col, lp["swa"], bsz, seq)
        wb = w_branch[l].astype(BF16)
        x2 = _merge(x2, norm_mix[l], o_rw, o_gdn, o_swa, lp["w_gate"], wb[:RW_WIDTH], wb[RW_WIDTH:RW_WIDTH + GDN_WIDTH],
                    _swa_head_order(wb[RW_WIDTH + GDN_WIDTH:], axis=0), w_out[l].astype(BF16))
        x2 = _conv_ffn(x2, norm_ffn[l], ffn_up[l].astype(BF16), ffn_conv[l], ffn_down[l].astype(BF16), seq)
    return x2.reshape(bsz, seq, d)
```

```python
import functools

import jax
import jax.numpy as jnp
import numpy as np
from jax import lax
from jax.experimental import pallas as pl
from jax.experimental.pallas import tpu as pltpu

F32 = jnp.float32
BF16 = jnp.bfloat16

D_MODEL = 2048
RW_HEADS, RW_HEAD_DIM = 8, 64
RW_WIDTH = RW_HEADS * RW_HEAD_DIM
RW_DECAY_RANK, RW_ICLR_RANK, RW_GATE_RANK, RW_VRES_RANK = 64, 64, 128, 32
RW_GN_EPS = 64e-5
RW_COLS = 3 * RW_WIDTH + RW_DECAY_RANK + RW_ICLR_RANK + RW_GATE_RANK
GDN_HEADS, GDN_HEAD_DIM = 6, 128
GDN_WIDTH = GDN_HEADS * GDN_HEAD_DIM
GDN_CONV = 4
SWA_Q_HEADS, SWA_KV_HEADS, SWA_HEAD_DIM = 12, 4, 64
SWA_WIDTH = SWA_Q_HEADS * SWA_HEAD_DIM
SWA_KV_WIDTH = SWA_KV_HEADS * SWA_HEAD_DIM
SWA_WINDOW = 128
SWA_COLS = SWA_WIDTH + 2 * SWA_KV_WIDTH
ROPE_DIM = SWA_HEAD_DIM // 4
ROPE_THETA = 500000.0
D_FF = 5632
FFN_CONV = 3
NORM_EPS = 1e-6

LANES = 128
SUBLANES = 8
PACKED_ROWS = 16
VMEM_LIMIT_BYTES = 56 * 1024 * 1024

CHUNK = 64
INV_BLOCK = 16
FFN_SUBTILE = 512
SLAB_VRES0 = 0
SLAB_BETA0 = RW_VRES_RANK
SLAB_ALPHA0 = RW_VRES_RANK + GDN_HEADS
RW_GROUP = RW_COLS + LANES
_G = SWA_Q_HEADS // SWA_KV_HEADS
SWA_Q_ORDER = tuple((2 * p + side) * _G + t for p in range(SWA_KV_HEADS // 2) for t in range(_G) for side in range(2))

def _cparams(sem):
    return pltpu.CompilerParams(dimension_semantics=sem, vmem_limit_bytes=VMEM_LIMIT_BYTES)


def _dg(a, b, dims):
    return lax.dot_general(a.astype(BF16), b.astype(BF16), dims, preferred_element_type=F32)


def _mm(a, b):
    return _dg(a, b, (((1,), (0,)), ((), ())))


def _mm_nt(a, b):
    return _dg(a, b, (((1,), (1,)), ((), ())))


def _mm_tn(a, b):
    return _dg(a, b, (((0,), (0,)), ((), ())))


def _bmm(a, b):
    return _dg(a, b, (((2,), (1,)), ((0,), (0,))))


def _bmm_nt(a, b):
    return _dg(a, b, (((2,), (2,)), ((0,), (0,))))


def _bmm_tn(a, b):
    return _dg(a, b, (((1,), (1,)), ((0,), (0,))))


def _split3(a):
    hi = a.astype(BF16)
    r1 = a - hi.astype(F32)
    mid = r1.astype(BF16)
    lo = (r1 - mid.astype(F32)).astype(BF16)
    return hi, mid, lo


def _mm_exact_lhs(a_bf16, b):
    hi, mid, lo = _split3(b)
    d = lambda t: jnp.dot(a_bf16, t, preferred_element_type=F32)
    return d(hi) + d(mid) + d(lo)


def _mm_tn_exact_rhs(a, b_bf16):
    hi, mid, lo = _split3(a)
    d = lambda t: _mm_tn(t, b_bf16)
    return d(hi) + d(mid) + d(lo)


def _iota(shape, dim):
    return lax.broadcasted_iota(jnp.int32, shape, dim)


def _sigmoid(x):
    return 0.5 * jnp.tanh(0.5 * x) + 0.5


def _causal_taps(x, taps):
    k = taps.shape[0]
    acc = x * taps[k - 1:k, :]
    for s in range(1, k):
        acc = acc + pltpu.roll(x, s, 0) * taps[k - 1 - s:k - s, :]
    return acc[SUBLANES:, :]


def _neumann_inverse(lmat):
    assert CHUNK // INV_BLOCK == 4
    n = lmat.shape[-1]
    row, col = _iota((1, n, n), 1), _iota((1, n, n), 2)
    eye = (row == col).astype(F32)
    in_blk = (row // INV_BLOCK) == (col // INV_BLOCK)
    lb = jnp.where(in_blk, lmat, 0.0)
    e = lmat - lb
    inv = eye + lb
    p = _bmm(lb, lb)
    steps = INV_BLOCK.bit_length() - 2
    for s in range(steps):
        if s + 1 < steps:
            both = _bmm(jnp.concatenate([inv, p], axis=1), p)
            inv, p = inv + both[:, :n], both[:, n:]
        else:
            inv = inv + _bmm(inv, p)
    nmat = _bmm(inv, e)
    n2 = _bmm(nmat, nmat)
    m = eye + nmat + n2 + _bmm(nmat, n2)
    return _bmm(m, inv)


def _pack_kernel(shift, *refs):
    if shift == 0:
        a_ref, o_ref = refs
        o_ref[...] = a_ref[...].astype(BF16)
    else:
        a_ref, b_ref, o_ref = refs
        lane = _iota(a_ref.shape, 1)
        moved_a = pltpu.roll(a_ref[...], LANES - shift, 1)
        moved_b = pltpu.roll(b_ref[...], LANES - shift, 1)
        o_ref[...] = jnp.where(lane < LANES - shift, moved_a, moved_b).astype(BF16)


def _pack_cols(w3, layer, start, width):
    _, d, n = w3.shape
    q, shift = divmod(start, LANES)
    last_blk = pl.cdiv(n, LANES) - 1
    src = lambda off: pl.BlockSpec((None, d, LANES), lambda t: (layer, 0, jnp.minimum(q + t + off, last_blk)))
    ins = [w3] if shift == 0 else [w3, w3]
    return pl.pallas_call(
        functools.partial(_pack_kernel, shift),
        out_shape=jax.ShapeDtypeStruct((d, width), BF16),
        grid=(width // LANES,),
        in_specs=[src(0)] if shift == 0 else [src(0), src(1)],
        out_specs=pl.BlockSpec((d, LANES), lambda t: (0, t)),
        compiler_params=_cparams(("arbitrary",)),
        name="pack_cols",
    )(*ins)


INPROJ_TN = 512


def _inproj_kernel(starts, x_ref, g_ref, w_ref, *rest):
    outs, h_scr = rest[:-1], rest[-1]
    j = pl.program_id(1)

    @pl.when(j == 0)
    def _():
        x = x_ref[...]
        y = x * lax.rsqrt(jnp.mean(x * x, axis=-1, keepdims=True) + NORM_EPS) * g_ref[...]
        h_scr[...] = y.astype(BF16)

    for k, o_ref in enumerate(outs):
        @pl.when((j >= starts[k]) & (j < starts[k + 1]))
        def _(o_ref=o_ref):
            o_ref[...] = jnp.dot(h_scr[...], w_ref[...], preferred_element_type=F32).astype(o_ref.dtype)


def _pick_tile(n, prefs):
    for t in prefs:
        if n % t == 0:
            return t
    return n


def _inproj(x2, gain, w_bf16, widths):
    m, d = x2.shape
    tn = INPROJ_TN
    assert sum(widths) == w_bf16.shape[1] and all(w % tn == 0 for w in widths)
    tm = _pick_tile(m, (1024, 512, 256, 128, 64, 32, 16, 8))
    starts = tuple(int(s) for s in np.cumsum((0,) + tuple(w // tn for w in widths)))
    out_spec = lambda k: pl.BlockSpec(
        (tm, tn), lambda i, j: (i, jnp.clip(j - starts[k], 0, starts[k + 1] - starts[k] - 1)))
    return pl.pallas_call(
        functools.partial(_inproj_kernel, starts),
        out_shape=tuple(jax.ShapeDtypeStruct((m, w), BF16) for w in widths),
        grid=(m // tm, starts[-1]),
        in_specs=[
            pl.BlockSpec((tm, d), lambda i, j: (i, 0)),
            pl.BlockSpec((1, d), lambda i, j: (0, 0)),
            pl.BlockSpec((d, tn), lambda i, j: (0, j)),
        ],
        out_specs=tuple(out_spec(k) for k in range(len(widths))),
        scratch_shapes=[pltpu.VMEM((tm, d), BF16)],
        compiler_params=_cparams(("parallel", "arbitrary")),
        name="inproj",
    )(x2, gain.reshape(1, d), w_bf16)


def _rwkv_kernel(has_vres, tb, *refs):
    scr = refs[-14:]
    (carry, state, pl_scr, at_scr, rt_scr, bt_scr, kt_scr, v_scr, y_scr,
     t_scr, rb_scr, akv_scr, rkv_scr, btk_scr) = scr
    if has_vres:
        (p_ref, vf_ref, mu_ref, w0_ref, a0_ref, wwa_ref, gup_ref, kk_ref, ka_ref, rk_ref, lnw_ref, lnb_ref,
         hones_ref, v0_ref, vup_ref, o_ref) = refs[:-14]
    else:
        (p_ref, mu_ref, w0_ref, a0_ref, wwa_ref, gup_ref, kk_ref, ka_ref, rk_ref, lnw_ref, lnb_ref,
         hones_ref, o_ref, vout_ref) = refs[:-14]
    W = RW_WIDTH
    tstep = pl.program_id(1)

    @pl.when(tstep == 0)
    def _():
        carry[...] = jnp.zeros_like(carry)
        state[...] = jnp.zeros_like(state)

    row0 = _iota((tb, LANES), 0) == 0
    for c0 in range(0, RW_GROUP, LANES):
        x = p_ref[:, c0:c0 + LANES].astype(F32)
        prev = pltpu.roll(x, 1, 0)
        prev = jnp.where(row0, carry[PACKED_ROWS - 1:PACKED_ROWS, c0:c0 + LANES], prev)
        pl_scr[:, c0:c0 + LANES] = x + (prev - x) * mu_ref[:, c0:c0 + LANES]
    carry[...] = p_ref[tb - PACKED_ROWS:tb, :].astype(F32)

    head_sum = lambda t: _mm(t, hones_ref[...])

    c_wd = 3 * W
    x128 = pl_scr[:, c_wd:c_wd + LANES]
    lane = _iota((tb, LANES), 1)
    xin = jnp.where(lane < RW_DECAY_RANK, jnp.tanh(x128), x128)
    wa = jnp.dot(xin.astype(BF16), wwa_ref[...], preferred_element_type=F32)
    w_log = -jax.nn.softplus(-(w0_ref[...] + wa[:, :W])) - 0.5
    lw = -jnp.exp(w_log)
    asig = _sigmoid(a0_ref[...] + wa[:, W:])
    gd = pl_scr[:, c_wd + LANES:c_wd + 2 * LANES]
    g = jnp.dot(_sigmoid(gd).astype(BF16), gup_ref[...], preferred_element_type=F32)

    v = pl_scr[:, 2 * W:3 * W]
    if has_vres:
        vd = pl_scr[:, RW_COLS:RW_COLS + LANES]
        vl = jnp.dot(vd.astype(BF16), vup_ref[...], preferred_element_type=F32)
        v = v + (vf_ref[...] - v) * _sigmoid(v0_ref[...] + vl)
    else:
        vout_ref[...] = v
    v_scr[...] = v

    k = pl_scr[:, W:2 * W]
    kkraw = k * kk_ref[...]
    ssq = head_sum(kkraw * kkraw)
    kk = kkraw * lax.rsqrt(ssq + 1e-12)
    kfin = k * (1.0 + (asig - 1.0) * ka_ref[...])
    r = pl_scr[:, 0:W]
    bonus = head_sum(r * kfin * rk_ref[...]) * v

    C = CHUNK
    P2 = 2 * C
    NC, NP = tb // C, RW_HEADS // 2
    tril_b = (_iota((NC, C, C), 1) >= _iota((NC, C, C), 2)).astype(BF16)
    cum = sum(_bmm(tril_b, t.reshape(NC, C, W)) for t in _split3(lw)).reshape(tb, W)
    e_pos = jnp.exp(cum)
    e_neg = jnp.exp(-cum)
    rt_scr[...] = r * e_pos
    kt_scr[...] = kfin * e_neg
    bt_scr[...] = (kk * asig) * e_neg
    at_scr[...] = -kk * jnp.exp(cum - lw)

    def to_b(ref):
        return jnp.concatenate([ref[c * C:(c + 1) * C, p * LANES:(p + 1) * LANES][None]
                                for c in range(NC) for p in range(NP)], axis=0)

    row, col = _iota((1, P2, P2), 1), _iota((1, P2, P2), 2)
    same_head = (row // C) == (col // C)
    strict = same_head & ((row % C) > (col % C))
    incl = same_head & ((row % C) >= (col % C))
    left = _iota((1, C, LANES), 2) < RW_HEAD_DIM
    leftf = left.astype(F32)
    rightf = 1.0 - leftf

    def sel(x3):
        return jnp.where(left, x3[:, :C], x3[:, C:])

    at_b, rt_b, bt_b, kt_b, vp_b = to_b(at_scr), to_b(rt_scr), to_b(bt_scr), to_b(kt_scr), to_b(v_scr)
    lhs = jnp.concatenate([at_b * leftf, rt_b * leftf, at_b * rightf, rt_b * rightf], axis=1)
    out_ab = _bmm_nt(lhs, jnp.concatenate([bt_b, kt_b, kt_b, bt_b], axis=1))
    out_a, out_b = out_ab[:, :, :P2], out_ab[:, :, P2:]
    ab = jnp.where(strict, jnp.concatenate([out_a[:, 0:C], out_b[:, 2 * C:3 * C]], axis=1), 0.0)
    ak = jnp.where(strict, jnp.concatenate([out_b[:, 0:C], out_a[:, 2 * C:3 * C]], axis=1), 0.0)
    rb = jnp.where(incl, jnp.concatenate([out_a[:, C:2 * C], out_b[:, 3 * C:4 * C]], axis=1), 0.0)
    rk = jnp.where(incl, jnp.concatenate([out_b[:, C:2 * C], out_a[:, 3 * C:4 * C]], axis=1), 0.0)
    t_scr[...] = _neumann_inverse(ab)
    rb_scr[...] = rb
    vv = jnp.concatenate([vp_b, vp_b], axis=1)
    akv_scr[...] = sel(_bmm(ak, vv))
    rkv_scr[...] = sel(_bmm(rk, vv))
    pc_rows = jnp.concatenate([e_pos[(c + 1) * C - 1:(c + 1) * C, p * LANES:(p + 1) * LANES][None]
                               for c in range(NC) for p in range(NP)], axis=0)
    btk_scr[...] = jnp.concatenate([bt_b * pc_rows, kt_b * pc_rows], axis=1)

    for c in range(NC):
        items = slice(c * NP, (c + 1) * NP)
        s = state[...]
        a_c, r_c, v_c = at_b[items], rt_b[items], vp_b[items]
        ars = _bmm_nt(jnp.concatenate([a_c, r_c], axis=1), s)
        rhs = ars[:, :C] + akv_scr[items]
        u = sel(_bmm(t_scr[items], jnp.concatenate([rhs, rhs], axis=1)))
        y = ars[:, C:] + sel(_bmm(rb_scr[items], jnp.concatenate([u, u], axis=1))) + rkv_scr[items]
        s_new = s * pc_rows[items] + _bmm_tn(jnp.concatenate([u, v_c], axis=1), btk_scr[items])
        state[...] = jnp.where(same_head, s_new, 0.0)
        for p in range(NP):
            y_scr[c * C:(c + 1) * C, p * LANES:(p + 1) * LANES] = y[p]

    y = y_scr[...]
    inv_n = 1.0 / RW_HEAD_DIM
    y_hi = y.astype(BF16).astype(F32)
    mean = (head_sum(y_hi) + head_sum(y - y_hi)) * inv_n
    yc = y - mean
    var = head_sum(yc * yc) * inv_n
    yn = yc * lax.rsqrt(var + RW_GN_EPS) * lnw_ref[...] + lnb_ref[...]
    o_ref[...] = ((yn + bonus) * g).astype(o_ref.dtype)


def _rwkv_mixer(p_rw, v_first, prm, bsz, seq, has_vres):
    m = p_rw.shape[0]
    tb = _pick_tile(seq, (512, 256, 128, 64))
    nt = seq // tb
    nb = (tb // CHUNK) * (RW_HEADS // 2)
    W = RW_WIDTH
    row_spec = lambda width: pl.BlockSpec((tb, width), lambda b, t: (b * nt + t, 0))
    full = lambda a: pl.BlockSpec(a.shape, lambda b, t: (0,) * a.ndim)
    names = (["mu", "w0", "a0", "wwa", "gup", "kk", "ka", "rk", "lnw", "lnb", "hones"]
             + (["v0", "vup"] if has_vres else []))
    params = [prm[n] for n in names]
    ins = [p_rw] + ([v_first] if has_vres else []) + params
    in_specs = [row_spec(RW_GROUP)] + ([row_spec(W)] if has_vres else []) + [full(a) for a in params]
    if has_vres:
        out_shape = jax.ShapeDtypeStruct((m, W), BF16)
        out_specs = row_spec(W)
    else:
        out_shape = (jax.ShapeDtypeStruct((m, W), BF16), jax.ShapeDtypeStruct((m, W), F32))
        out_specs = (row_spec(W), row_spec(W))
    scratch = [
        pltpu.VMEM((PACKED_ROWS, RW_GROUP), F32),
        pltpu.VMEM((RW_HEADS // 2, LANES, LANES), F32),
        pltpu.VMEM((tb, RW_GROUP), F32),
    ] + [pltpu.VMEM((tb, W), F32) for _ in range(6)] + [
        pltpu.VMEM((nb, 2 * CHUNK, 2 * CHUNK), F32),
        pltpu.VMEM((nb, 2 * CHUNK, 2 * CHUNK), F32),
        pltpu.VMEM((nb, CHUNK, LANES), F32),
        pltpu.VMEM((nb, CHUNK, LANES), F32),
        pltpu.VMEM((nb, 2 * CHUNK, LANES), F32),
    ]
    res = pl.pallas_call(
        functools.partial(_rwkv_kernel, has_vres, tb),
        out_shape=out_shape,
        grid=(bsz, nt),
        in_specs=in_specs,
        out_specs=out_specs,
        scratch_shapes=scratch,
        compiler_params=_cparams(("parallel", "arbitrary")),
        name="rwkv7_vres" if has_vres else "rwkv7",
    )(*ins)
    if has_vres:
        return res, None
    return res


def _gdn_kernel(tb, pg_ref, slab_ref, cw_ref, alog_ref, dtb_ref, onorm_ref, eb_ref, ea_ref, o_ref,
                carry, state, ext, q_scr, k_scr, v_scr, grow_scr, o_scr,
                gcf_scr, kb_scr, vb_scr, kbeg_scr, qeg_scr, kt2_scr,
                intra_scr, u_scr, wq_scr, kt_scr):
    H, Dh, Wd = GDN_HEADS, GDN_HEAD_DIM, GDN_WIDTH
    QKV = 3 * Wd
    tstep = pl.program_id(1)

    @pl.when(tstep == 0)
    def _():
        carry[...] = jnp.zeros_like(carry)
        state[...] = jnp.zeros_like(state)

    ext[0:PACKED_ROWS, :] = carry[...]
    ext[PACKED_ROWS:, :] = pg_ref[:, 0:QKV].astype(F32)
    carry[...] = pg_ref[tb - PACKED_ROWS:tb, 0:QKV].astype(F32)
    for j in range(QKV // Dh):
        ln = slice(j * Dh, (j + 1) * Dh)
        acc = _causal_taps(ext[pl.ds(PACKED_ROWS - SUBLANES, tb + SUBLANES), ln], cw_ref[:, ln])
        act = acc * _sigmoid(acc)
        which, h = divmod(j, H)
        hl = slice(h * Dh, (h + 1) * Dh)
        if which == 0:
            nrm = lax.rsqrt(jnp.sum(act * act, axis=-1, keepdims=True) + 1e-6)
            q_scr[:, hl] = act * nrm * (Dh ** -0.5)
        elif which == 1:
            nrm = lax.rsqrt(jnp.sum(act * act, axis=-1, keepdims=True) + 1e-6)
            k_scr[:, hl] = act * nrm
        else:
            v_scr[:, hl] = act

    slab = slab_ref[...].astype(F32)
    gsl =-jnp.exp(alog_ref[...]) * jax.nn.softplus(slab + dtb_ref[...])
    lane = _iota((tb, LANES), 1)
    gsl = jnp.where((lane >= SLAB_ALPHA0) & (lane < SLAB_ALPHA0 + H), gsl, 0.0)

    C = CHUNK
    NC = tb // C
    trow, tcol = _iota((tb, tb), 0), _iota((tb, tb), 1)
    same_chunk = (trow // C) == (tcol // C)
    gcol = _mm_exact_lhs((same_chunk & (trow >= tcol)).astype(BF16), gsl)
    grow_scr[...] = _mm_tn_exact_rhs(gsl, (same_chunk & (trow <= tcol)).astype(BF16))

    def spread(x, e_ref):
        return sum(_mm(t, e_ref[...]) for t in _split3(x))

    beta_f = spread(_sigmoid(slab), eb_ref)
    gc_f = spread(gcol, ea_ref)
    eg_f = jnp.exp(gc_f)
    k2 = k_scr[...]
    kb2 = k2 * beta_f
    gc3 = gc_f.reshape(NC, C, Wd)
    gl3 = gc3[:, C - 1:C, :]
    gcf_scr[...] = gc_f
    kb_scr[...] = kb2
    vb_scr[...] = v_scr[...] * beta_f
    kbeg_scr[...] = kb2 * eg_f
    qeg_scr[...] = q_scr[...] * eg_f
    kt2_scr[...] = (k2.reshape(NC, C, Wd) * jnp.exp(gl3 - gc3)).reshape(tb, Wd)
    egl3 = jnp.exp(gl3)

    def items(fn):
        return jnp.concatenate([fn(c, h)[None] for c in range(NC) for h in range(H)], axis=0)

    rows = lambda c: slice(c * C, (c + 1) * C)
    head = lambda h: slice(h * Dh, (h + 1) * Dh)
    tile = lambda ref: items(lambda c, h: ref[rows(c), head(h)])
    gci = items(lambda c, h: gcf_scr[rows(c), h * Dh:h * Dh + C])
    gr = items(lambda c, h: grow_scr[SLAB_ALPHA0 + h:SLAB_ALPHA0 + h + 1, rows(c)])
    eg_last = jnp.concatenate([egl3[c:c + 1, :, head(h)] for c in range(NC) for h in range(H)], axis=0)
    k_b = tile(k_scr)
    row, col = _iota((1, C, C), 1), _iota((1, C, C), 2)
    dec = jnp.exp(jnp.where(row >= col, gci - gr, -jnp.inf))
    prod = _bmm_nt(jnp.concatenate([tile(kb_scr), tile(q_scr)], axis=1), k_b)
    lmat = jnp.where(row > col, prod[:, :C] * dec, 0.0)
    intra_scr[...] = prod[:, C:] * dec
    nb = NC * H
    l2 = (-lmat).reshape(nb // 2, 2 * C, C)
    prow, pcol = _iota((1, 2 * C, 2 * C), 1), _iota((1, 2 * C, 2 * C), 2)
    lp = jnp.where((prow // C) == (pcol // C), jnp.concatenate([l2, l2], axis=2), 0.0)
    tp = _neumann_inverse(lp)
    rhs = jnp.concatenate([tile(vb_scr), tile(kbeg_scr)], axis=2).reshape(nb // 2, 2 * C, 2 * Dh)
    sol = _bmm(tp, rhs).reshape(nb, C, 2 * Dh)
    u_scr[...] = sol[:, :, :Dh]
    wq_scr[...] = jnp.concatenate([sol[:, :, Dh:], tile(qeg_scr)], axis=1)
    kt_scr[...] = tile(kt2_scr)

    for c in range(NC):
        it = slice(c * H, (c + 1) * H)
        s = state[...]
        ws = _bmm(wq_scr[it], s)
        v_new = u_scr[it] - ws[:, :C]
        o = ws[:, C:] + _bmm(intra_scr[it], v_new)
        state[...] = s * eg_last[it] + _bmm_tn(kt_scr[it], v_new)
        for h in range(H):
            o_scr[rows(c), head(h)] = o[h]

    for h in range(H):
        hl = slice(h * Dh, (h + 1) * Dh)
        o = o_scr[:, hl]
        o = o * lax.rsqrt(jnp.mean(o * o, axis=-1, keepdims=True) + NORM_EPS) * onorm_ref[...]
        z = pg_ref[:, QKV + h * Dh:QKV + (h + 1) * Dh].astype(F32)
        o_ref[:, hl] = (o * (z * _sigmoid(z))).astype(o_ref.dtype)


def _gdn_mixer(p_gdn, p_rw, prm, bsz, seq):
    m = p_gdn.shape[0]
    tb = _pick_tile(seq, (256, 128))
    nt = seq // tb
    nb = (tb // CHUNK) * GDN_HEADS
    Wd = GDN_WIDTH
    lane_head = np.arange(Wd)[None, :] // GDN_HEAD_DIM
    spread = lambda off: jnp.asarray(np.arange(LANES)[:, None] - off == lane_head, BF16)
    params = [prm["cw"], prm["alog"], prm["dtb"], prm["onorm"], spread(SLAB_BETA0), spread(SLAB_ALPHA0)]
    full = lambda a: pl.BlockSpec(a.shape, lambda b, t: (0,) * a.ndim)
    slab_blk = RW_COLS // LANES
    return pl.pallas_call(
        functools.partial(_gdn_kernel, tb),
        out_shape=jax.ShapeDtypeStruct((m, Wd), BF16),
        grid=(bsz, nt),
        in_specs=[pl.BlockSpec((tb, 4 * Wd), lambda b, t: (b * nt + t, 0)),
                  pl.BlockSpec((tb, LANES), lambda b, t: (b * nt + t, slab_blk))] + [full(a) for a in params],
        out_specs=pl.BlockSpec((tb, Wd), lambda b, t: (b * nt + t, 0)),
        scratch_shapes=[
            pltpu.VMEM((PACKED_ROWS, 3 * Wd), F32),
            pltpu.VMEM((GDN_HEADS, GDN_HEAD_DIM, GDN_HEAD_DIM), F32),
            pltpu.VMEM((tb + PACKED_ROWS, 3 * Wd), F32),
            pltpu.VMEM((tb, Wd), F32), pltpu.VMEM((tb, Wd), F32), pltpu.VMEM((tb, Wd), F32),
            pltpu.VMEM((LANES, tb), F32),
            pltpu.VMEM((tb, Wd), F32),
        ] + [pltpu.VMEM((tb, Wd), F32) for _ in range(6)] + [
            pltpu.VMEM((nb, CHUNK, CHUNK), F32),
            pltpu.VMEM((nb, CHUNK, GDN_HEAD_DIM), F32),
            pltpu.VMEM((nb, 2 * CHUNK, GDN_HEAD_DIM), F32),
            pltpu.VMEM((nb, CHUNK, GDN_HEAD_DIM), F32),
        ],
        compiler_params=_cparams(("parallel", "arbitrary")),
        name="gdn",
    )(p_gdn, p_rw, *params)


def _swa_kernel(nblk, ps_ref, pos_ref, freq_ref, sgn_lo_ref, sgn_hi_ref, qn_ref, kn_ref, sink_ref, ones_ref, o_ref,
                kprev, vprev):
    Wn = SWA_WINDOW
    hd = SWA_HEAD_DIM
    G = SWA_Q_HEADS // SWA_KV_HEADS
    QW, KW = SWA_WIDTH, SWA_KV_WIDTH
    n = pl.program_id(1)

    @pl.when(n == 0)
    def _():
        kprev[...] = jnp.zeros_like(kprev)
        vprev[...] = jnp.zeros_like(vprev)

    half = ROPE_DIM // 2
    ang = pos_ref[...] * freq_ref[...]
    cs1, sn1 = jnp.cos(ang), jnp.sin(ang)
    lo1, hi1 = sn1 * sgn_lo_ref[...], sn1 * sgn_hi_ref[...]

    def norm_rope(x, gain_row, width):
        rep = lambda t: jnp.concatenate([t] * (width // LANES), axis=1)
        ms = _mm(x * x, ones_ref[0:width, 0:width]) * (1.0 / hd)
        y = x * lax.rsqrt(ms + NORM_EPS) * gain_row
        up = pltpu.roll(y, width - half, 1)
        dn = pltpu.roll(y, half, 1)
        return y * rep(cs1) + up * rep(lo1) + dn * rep(hi1)

    q = norm_rope(ps_ref[:, 0:QW].astype(F32), qn_ref[...], QW) * (hd ** -0.5)
    k = norm_rope(ps_ref[:, QW:QW + KW].astype(F32), kn_ref[...], KW)
    v = ps_ref[:, QW + KW:QW + 2 * KW].astype(F32)

    qi = _iota((G * Wn, 2 * Wn), 0) % Wn
    kj = _iota((G * Wn, 2 * Wn), 1)
    rel = qi + Wn - kj
    in_window = (rel >= 0) & (rel < SWA_WINDOW)
    grp = _iota((G * Wn, 1), 0) // Wn
    sinks = sink_ref[...]
    first_half = _iota((G * Wn, LANES), 1) < hd
    side_mask = (first_half.astype(F32), 1.0 - first_half.astype(F32))
    ones_kv = jnp.ones((2 * Wn, LANES), BF16)

    for blk in range(nblk):
        rows = slice(blk * Wn, (blk + 1) * Wn)
        before = slice((blk - 1) * Wn, blk * Wn)
        allowed = in_window & ((kj >= Wn) | (n > 0)) if blk == 0 else in_window
        for pair in range(SWA_KV_HEADS // 2):
            kl = slice(pair * LANES, (pair + 1) * LANES)
            k_before = kprev[:, kl] if blk == 0 else k[before, kl]
            v_before = vprev[:, kl] if blk == 0 else v[before, kl]
            kband = jnp.concatenate([k_before, k[rows, kl]], axis=0).astype(BF16)
            vband = jnp.concatenate([v_before, v[rows, kl]], axis=0).astype(BF16)
            qs = jnp.concatenate([q[rows, (pair * G + t) * LANES:(pair * G + t + 1) * LANES] for t in range(G)],
                                 axis=0)
            outs = []
            for side in range(2):
                s = jnp.where(allowed, _mm_nt(qs * side_mask[side], kband), -jnp.inf)
                sink = jnp.zeros((G * Wn, 1), F32)
                for t in range(G):
                    h = SWA_Q_ORDER[(pair * G + t) * 2 + side]
                    sink = jnp.where(grp == t, sinks[:, h:h + 1], sink)
                mx = jnp.maximum(jnp.max(s, axis=-1, keepdims=True), sink)
                p = jnp.exp(s - mx).astype(BF16)
                den = jnp.dot(p, ones_kv, preferred_element_type=F32) + jnp.exp(sink - mx)
                outs.append(jnp.dot(p, vband, preferred_element_type=F32) / den)
            o = jnp.where(first_half, outs[0], outs[1])
            for t in range(G):
                o_ref[rows, (pair * G + t) * LANES:(pair * G + t + 1) * LANES] = (
                    o[t * Wn:(t + 1) * Wn].astype(o_ref.dtype))

    last = slice((nblk - 1) * Wn, nblk * Wn)
    kprev[...] = k[last, :]
    vprev[...] = v[last, :]


def _swa_mixer(p_swa, pos_col, prm, bsz, seq):
    m = p_swa.shape[0]
    Wn = SWA_WINDOW
    nblk = next(c for c in (4, 2, 1) if seq % (c * Wn) == 0)
    rows = nblk * Wn
    nb = seq // rows
    head_ones = _head_ones(SWA_WIDTH, SWA_HEAD_DIM)
    params = [prm["freq"], prm["sgn_lo"], prm["sgn_hi"], prm["qn"], prm["kn"], prm["sinks"], head_ones]
    full = lambda a: pl.BlockSpec(a.shape, lambda b, t: (0,) * a.ndim)
    return pl.pallas_call(
        functools.partial(_swa_kernel, nblk),
        out_shape=jax.ShapeDtypeStruct((m, SWA_WIDTH), BF16),
        grid=(bsz, nb),
        in_specs=[pl.BlockSpec((rows, SWA_COLS), lambda b, t: (b * nb + t, 0)),
                  pl.BlockSpec((rows, 1), lambda b, t: (b * nb + t, 0))] + [full(a) for a in params],
        out_specs=pl.BlockSpec((rows, SWA_WIDTH), lambda b, t: (b * nb + t, 0)),
        scratch_shapes=[pltpu.VMEM((Wn, SWA_KV_WIDTH), F32), pltpu.VMEM((Wn, SWA_KV_WIDTH), F32)],
        compiler_params=_cparams(("parallel", "arbitrary")),
        name="swa",
    )(p_swa, pos_col, *params)


def _merge_kernel(x_ref, gain_ref, orw_ref, ogdn_ref, oswa_ref, wg0_ref, wg1_ref, wg2_ref,
                  wb0_ref, wb1_ref, wb2_ref, wout_ref, o_ref, h_scr, acc):
    j = pl.program_id(1)

    @pl.when(j == 0)
    def _():
        x = x_ref[...]
        y = x * lax.rsqrt(jnp.mean(x * x, axis=-1, keepdims=True) + NORM_EPS) * gain_ref[...]
        h_scr[...] = y.astype(BF16)
        acc[...] = jnp.zeros_like(acc)

    h = h_scr[...]
    branch = lambda wg_ref, a_ref, wb_ref: (
        _sigmoid(jnp.dot(h, wg_ref[...], preferred_element_type=F32))
        * jnp.dot(a_ref[...], wb_ref[...], preferred_element_type=F32))
    merged = (branch(wg0_ref, orw_ref, wb0_ref) + branch(wg1_ref, ogdn_ref, wb1_ref)
              + branch(wg2_ref, oswa_ref, wb2_ref))
    acc[...] += jnp.dot(merged.astype(BF16), wout_ref[...], preferred_element_type=F32)

    @pl.when(j == pl.num_programs(1) - 1)
    def _():
        o_ref[...] = x_ref[...] + acc[...]


def _merge(x2, gain, o_rw, o_gdn, o_swa, w_gate, wb_rw, wb_gdn, wb_swa, w_out, layer):
    m, d = x2.shape
    tm = _pick_tile(m, (512, 256, 128, 64, 32, 16, 8))
    tn = 512
    nj = d // tn
    rows = lambda width: pl.BlockSpec((tm, width), lambda i, j: (i, 0))
    gate = lambda b: pl.BlockSpec((d, tn), lambda i, j, b=b: (0, b * nj + j))
    wcol = lambda kdim: pl.BlockSpec((kdim, tn), lambda i, j: (0, j))
    return pl.pallas_call(
        _merge_kernel,
        out_shape=jax.ShapeDtypeStruct((m, d), F32),
        grid=(m // tm, nj),
        in_specs=[rows(d), pl.BlockSpec((1, d), lambda i, j: (0, 0)),
                  rows(RW_WIDTH), rows(GDN_WIDTH), rows(SWA_WIDTH), gate(0), gate(1), gate(2),
                  wcol(RW_WIDTH), wcol(GDN_WIDTH), wcol(SWA_WIDTH),
                  pl.BlockSpec((None, tn, d), lambda i, j: (layer, j, 0))],
        out_specs=rows(d),
        scratch_shapes=[pltpu.VMEM((tm, d), BF16), pltpu.VMEM((tm, d), F32)],
        compiler_params=_cparams(("parallel", "arbitrary")),
        name="merge",
    )(x2, gain.reshape(1, d), o_rw, o_gdn, o_swa, w_gate, w_gate, w_gate, wb_rw, wb_gdn, wb_swa, w_out)


def _ffn_up_kernel(tm, seq, x_ref, halo_ref, gain_ref, wg_ref, wu_ref, cg_ref, cu_ref, act_ref, h_scr, u_scr):
    i = pl.program_id(0)
    j = pl.program_id(1)
    tn = wg_ref.shape[1]

    @pl.when(j == 0)
    def _():
        def norm(x):
            return (x * lax.rsqrt(jnp.mean(x * x, axis=-1, keepdims=True) + NORM_EPS) * gain_ref[...]).astype(BF16)
        seq_start = (i * tm) % seq == 0
        halo = jnp.where(seq_start, 0.0, halo_ref[...])
        h_scr[0:2 * SUBLANES, :] = norm(jnp.concatenate([jnp.zeros_like(halo), halo], axis=0))
        h_scr[2 * SUBLANES:, :] = norm(x_ref[...])

    h = h_scr[...]
    ts = FFN_SUBTILE
    for s in range(tn // ts):
        cols = slice(s * ts, (s + 1) * ts)

        def conv(w_ref, cw_ref, base):
            u_scr[:, base:base + ts] = jnp.dot(h, w_ref[:, cols], preferred_element_type=F32)
            return _causal_taps(u_scr[pl.ds(SUBLANES, tm + SUBLANES), base:base + ts], cw_ref[:, cols])

        gate = conv(wg_ref, cg_ref, 2 * s * ts)
        up = conv(wu_ref, cu_ref, (2 * s + 1) * ts)
        act_ref[:, cols] = (gate * _sigmoid(gate) * up).astype(BF16)


def _ffn_down_kernel(act_ref, wd_ref, x_ref, o_ref):
    o_ref[...] = x_ref[...] + jnp.dot(act_ref[...], wd_ref[...], preferred_element_type=F32)


def _conv_ffn(x2, gain, w_up2, c_up2, w_down, layer, seq):
    m, d = x2.shape
    dff = w_down.shape[1]
    tm = _pick_tile(seq, (1024, 512, 256, 128, 64, 32, 16))
    tn = _pick_tile(dff, (512, 256, 128))
    nj = dff // tn
    hb = tm // SUBLANES
    act = pl.pallas_call(
        functools.partial(_ffn_up_kernel, tm, seq),
        out_shape=jax.ShapeDtypeStruct((m, dff), BF16),
        grid=(m // tm, dff // tn),
        in_specs=[
            pl.BlockSpec((tm, d), lambda i, j: (i, 0)),
            pl.BlockSpec((SUBLANES, d), lambda i, j: (jnp.maximum(i * hb - 1, 0), 0)),
            pl.BlockSpec((1, d), lambda i, j: (0, 0)),
            pl.BlockSpec((None, d, tn), lambda i, j: (layer, 0, j)),
            pl.BlockSpec((None, d, tn), lambda i, j: (layer, 0, nj + j)),
            pl.BlockSpec((FFN_CONV, tn), lambda i, j: (0, j)),
            pl.BlockSpec((FFN_CONV, tn), lambda i, j: (0, nj + j)),
        ],
        out_specs=pl.BlockSpec((tm, tn), lambda i, j: (i, j)),
        scratch_shapes=[pltpu.VMEM((tm + 2 * SUBLANES, d), BF16),
                        pltpu.VMEM((tm + 2 * SUBLANES, 2 * tn), F32)],
        compiler_params=_cparams(("parallel", "arbitrary")),
        name="ffn_up",
    )(x2, x2, gain.reshape(1, d), w_up2, w_up2, c_up2, c_up2)
    tm2 = _pick_tile(m, (1024, 512, 256, 128, 64, 32, 16, 8))
    tn2 = _pick_tile(d, (512, 256, 128))
    return pl.pallas_call(
        _ffn_down_kernel,
        out_shape=jax.ShapeDtypeStruct((m, d), F32),
        grid=(m // tm2, d // tn2),
        in_specs=[
            pl.BlockSpec((tm2, dff), lambda i, j: (i, 0)),
            pl.BlockSpec((None, dff, tn2), lambda i, j: (layer, 0, j)),
            pl.BlockSpec((tm2, tn2), lambda i, j: (i, j)),
        ],
        out_specs=pl.BlockSpec((tm2, tn2), lambda i, j: (i, j)),
        compiler_params=_cparams(("parallel", "arbitrary")),
        name="ffn_down",
    )(act, w_down, x2)


def _row(a):
    return a.reshape(1, -1).astype(F32)


def _head_ones(width, head_dim):
    idx = np.arange(width) // head_dim
    return jnp.asarray(idx[:, None] == idx[None, :], BF16)


def _swa_head_order(a, axis):
    hd = SWA_HEAD_DIM
    take = lambda h: lax.slice_in_dim(a, h * hd, (h + 1) * hd, axis=axis)
    return jnp.concatenate([take(h) for h in SWA_Q_ORDER], axis=axis)


def _pad_cols(a, width):
    return jnp.pad(a, ((0, 0), (0, width - a.shape[1])))


def _layer_params(l, w_in, rw_mu, rw_w0, rw_w_up, rw_a0, rw_a_up, rw_g_up, rw_k_k, rw_k_a, rw_r_k, rw_ln_w,
                  rw_ln_b, vres_down, vres_mu, vres_v0, vres_v_up, gdn_conv, gdn_a_log, gdn_dt_bias, gdn_o_norm,
                  swa_q_norm, swa_k_norm, swa_sinks):
    d = w_in.shape[1]
    c_gdn = RW_COLS
    c_ba = c_gdn + 4 * GDN_WIDTH
    c_swa = c_ba + 2 * GDN_HEADS
    c_gate = c_swa + SWA_COLS
    has_vres = l > 0
    vres_w = vres_down[l - 1] if has_vres else jnp.zeros((d, RW_VRES_RANK), F32)
    assert c_ba % LANES == 0
    ba_w = _pack_cols(w_in, l, c_ba, LANES)[:, :2 * GDN_HEADS]
    slab_w = _pad_cols(jnp.concatenate([vres_w.astype(BF16), ba_w], axis=1), LANES)
    w_rw = jnp.concatenate([_pack_cols(w_in, l, 0, RW_COLS), slab_w], axis=1)
    w_gdn = _pack_cols(w_in, l, c_gdn, 4 * GDN_WIDTH)
    w_swa = _pack_cols(w_in, l, c_swa, SWA_COLS)
    w_swa = jnp.concatenate([_swa_head_order(w_swa[:, :SWA_WIDTH], axis=1), w_swa[:, SWA_WIDTH:]], axis=1)
    w_gate = _pack_cols(w_in, l, c_gate, w_in.shape[2] - c_gate)

    W = RW_WIDTH
    mu = jnp.concatenate([rw_mu[l], vres_mu[l - 1] if has_vres else jnp.zeros((RW_VRES_RANK,), F32),
                          jnp.zeros((LANES - RW_VRES_RANK,), F32)])
    wwa = jnp.zeros((LANES, 2 * W), F32)
    wwa = wwa.at[:RW_DECAY_RANK, :W].set(rw_w_up[l]).at[RW_DECAY_RANK:, W:].set(rw_a_up[l])
    rw = dict(mu=_row(mu), w0=_row(rw_w0[l]), a0=_row(rw_a0[l]), wwa=wwa.astype(BF16),
              gup=rw_g_up[l].astype(BF16), kk=_row(rw_k_k[l]), ka=_row(rw_k_a[l]), rk=_row(rw_r_k[l]),
              lnw=_row(rw_ln_w[l]), lnb=_row(rw_ln_b[l]), hones=_head_ones(W, RW_HEAD_DIM))
    if has_vres:
        vup = jnp.zeros((LANES, W), F32).at[:RW_VRES_RANK].set(vres_v_up[l - 1])
        rw.update(v0=_row(vres_v0[l - 1]), vup=vup.astype(BF16))

    lanes_pad = lambda a, off: jnp.zeros((1, LANES), F32).at[0, off:off + a.shape[0]].set(a)
    gdn = dict(cw=gdn_conv[l].astype(F32), alog=lanes_pad(gdn_a_log[l], SLAB_ALPHA0),
               dtb=lanes_pad(gdn_dt_bias[l], SLAB_ALPHA0), onorm=_row(gdn_o_norm[l]))

    half = ROPE_DIM // 2
    lane = np.arange(LANES) % SWA_HEAD_DIM
    inv_freq = ROPE_THETA ** (-(lane % half).astype(np.float32) * 2.0 / ROPE_DIM)
    freq = np.where(lane < ROPE_DIM, inv_freq, 0.0).astype(np.float32)
    sgn_lo = np.where(lane < half, -1.0, 0.0).astype(np.float32)
    sgn_hi = np.where((lane >= half) & (lane < ROPE_DIM), 1.0, 0.0).astype(np.float32)
    swa = dict(freq=jnp.asarray(freq).reshape(1, -1), sgn_lo=jnp.asarray(sgn_lo).reshape(1, -1),
               sgn_hi=jnp.asarray(sgn_hi).reshape(1, -1),
               qn=_row(jnp.tile(swa_q_norm[l], SWA_Q_HEADS)), kn=_row(jnp.tile(swa_k_norm[l], SWA_KV_HEADS)),
               sinks=lanes_pad(swa_sinks[l], 0))
    groups = (w_rw, w_gdn, w_swa)
    mix_widths = tuple(pl.cdiv(g.shape[1], INPROJ_TN) * INPROJ_TN for g in groups)
    w_mix = jnp.concatenate([_pad_cols(g, wd) for g, wd in zip(groups, mix_widths)], axis=1)
    return dict(w_mix=w_mix, mix_widths=mix_widths, w_gate=w_gate, rw=rw, gdn=gdn, swa=swa, has_vres=has_vres)


def kernel(x, positions, norm_mix, w_in, rw_mu, rw_w0, rw_w_up, rw_a0, rw_a_up, rw_g_up, rw_k_k, rw_k_a, rw_r_k, rw_ln_w, rw_ln_b, vres_down, vres_mu, vres_v0, vres_v_up, gdn_conv, gdn_a_log, gdn_dt_bias, gdn_o_norm, swa_q_norm, swa_k_norm, swa_sinks, w_branch, w_out, norm_ffn, ffn_up, ffn_conv, ffn_down):
    bsz, seq, d = x.shape
    depth = w_in.shape[0]
    x2 = x.reshape(bsz * seq, d)
    pos_col = positions.astype(F32).reshape(bsz * seq, 1)
    v_first = None
    ffn_up_bf16, ffn_down_bf16, w_out_bf16 = ffn_up.astype(BF16), ffn_down.astype(BF16), w_out.astype(BF16)
    for l in range(depth):
        lp = _layer_params(l, w_in, rw_mu, rw_w0, rw_w_up, rw_a0, rw_a_up, rw_g_up, rw_k_k, rw_k_a, rw_r_k,
                           rw_ln_w, rw_ln_b, vres_down, vres_mu, vres_v0, vres_v_up, gdn_conv, gdn_a_log,
                           gdn_dt_bias, gdn_o_norm, swa_q_norm, swa_k_norm, swa_sinks)
        p_rw, p_gdn, p_swa = _inproj(x2, norm_mix[l], lp["w_mix"], lp["mix_widths"])
        if lp["has_vres"]:
            o_rw, _ = _rwkv_mixer(p_rw, v_first, lp["rw"], bsz, seq, True)
        else:
            o_rw, v_first = _rwkv_mixer(p_rw, None, lp["rw"], bsz, seq, False)
        o_gdn = _gdn_mixer(p_gdn, p_rw, lp["gdn"], bsz, seq)
        o_swa = _swa_mixer(p_swa, pos_col, lp["swa"], bsz, seq)
        wb = w_branch[l].astype(BF16)
        x2 = _merge(x2, norm_mix[l], o_rw, o_gdn, o_swa, lp["w_gate"], wb[:RW_WIDTH], wb[RW_WIDTH:RW_WIDTH + GDN_WIDTH],
                    _swa_head_order(wb[RW_WIDTH + GDN_WIDTH:], axis=0), w_out_bf16, l)
        x2 = _conv_ffn(x2, norm_ffn[l], ffn_up_bf16, ffn_conv[l], ffn_down_bf16, l, seq)
    return x2.reshape(bsz, seq, d)
```

```python
import functools

import jax
import jax.numpy as jnp
import numpy as np
from jax import lax
from jax.experimental import pallas as pl
from jax.experimental.pallas import tpu as pltpu

F32 = jnp.float32
BF16 = jnp.bfloat16

RW_HEADS, RW_HEAD_DIM = 8, 64
RW_WIDTH = RW_HEADS * RW_HEAD_DIM
RW_DECAY_RANK, RW_ICLR_RANK, RW_GATE_RANK, RW_VRES_RANK = 64, 64, 128, 32
RW_GN_EPS = 64e-5
RW_COLS = 3 * RW_WIDTH + RW_DECAY_RANK + RW_ICLR_RANK + RW_GATE_RANK
GDN_HEADS, GDN_HEAD_DIM = 6, 128
GDN_WIDTH = GDN_HEADS * GDN_HEAD_DIM
SWA_Q_HEADS, SWA_KV_HEADS, SWA_HEAD_DIM = 12, 4, 64
SWA_WIDTH = SWA_Q_HEADS * SWA_HEAD_DIM
SWA_KV_WIDTH = SWA_KV_HEADS * SWA_HEAD_DIM
SWA_WINDOW = 128
SWA_COLS = SWA_WIDTH + 2 * SWA_KV_WIDTH
ROPE_DIM = SWA_HEAD_DIM // 4
ROPE_THETA = 500000.0
FFN_CONV = 3
NORM_EPS = 1e-6

LANES = 128
SUBLANES = 8
PACKED_ROWS = 16
VMEM_LIMIT_BYTES = 56 * 1024 * 1024

CHUNK = 64
INV_BLOCK = 16
FFN_SUBTILE = 512
SLAB_BETA0 = RW_VRES_RANK
SLAB_ALPHA0 = RW_VRES_RANK + GDN_HEADS
RW_GROUP = RW_COLS + LANES
_G = SWA_Q_HEADS // SWA_KV_HEADS
SWA_Q_ORDER = tuple((2 * p + side) * _G + t for p in range(SWA_KV_HEADS // 2) for t in range(_G) for side in range(2))

def _cparams(sem):
    return pltpu.CompilerParams(dimension_semantics=sem, vmem_limit_bytes=VMEM_LIMIT_BYTES)


def _dg(a, b, dims):
    return lax.dot_general(a.astype(BF16), b.astype(BF16), dims, preferred_element_type=F32)


def _mm(a, b):
    return _dg(a, b, (((1,), (0,)), ((), ())))


def _mm_nt(a, b):
    return _dg(a, b, (((1,), (1,)), ((), ())))


def _mm_tn(a, b):
    return _dg(a, b, (((0,), (0,)), ((), ())))


def _bmm(a, b):
    return _dg(a, b, (((2,), (1,)), ((0,), (0,))))


def _bmm_nt(a, b):
    return _dg(a, b, (((2,), (2,)), ((0,), (0,))))


def _bmm_tn(a, b):
    return _dg(a, b, (((1,), (1,)), ((0,), (0,))))


def _split3(a):
    hi = a.astype(BF16)
    r1 = a - hi.astype(F32)
    mid = r1.astype(BF16)
    lo = (r1 - mid.astype(F32)).astype(BF16)
    return hi, mid, lo


def _mm_exact_lhs(a_bf16, b):
    hi, mid, lo = _split3(b)
    d = lambda t: jnp.dot(a_bf16, t, preferred_element_type=F32)
    return d(hi) + d(mid) + d(lo)


def _mm_tn_exact_rhs(a, b_bf16):
    hi, mid, lo = _split3(a)
    d = lambda t: _mm_tn(t, b_bf16)
    return d(hi) + d(mid) + d(lo)


def _iota(shape, dim):
    return lax.broadcasted_iota(jnp.int32, shape, dim)


def _sigmoid(x):
    return 0.5 * jnp.tanh(0.5 * x) + 0.5


def _causal_taps(x, taps):
    k = taps.shape[0]
    acc = x * taps[k - 1:k, :]
    for s in range(1, k):
        acc = acc + pltpu.roll(x, s, 0) * taps[k - 1 - s:k - s, :]
    return acc[SUBLANES:, :]


def _neumann_inverse(lmat):
    assert CHUNK // INV_BLOCK == 4
    n = lmat.shape[-1]
    row, col = _iota((1, n, n), 1), _iota((1, n, n), 2)
    eye = (row == col).astype(F32)
    in_blk = (row // INV_BLOCK) == (col // INV_BLOCK)
    lb = jnp.where(in_blk, lmat, 0.0)
    e = lmat - lb
    inv = eye + lb
    p = _bmm(lb, lb)
    steps = INV_BLOCK.bit_length() - 2
    for s in range(steps):
        if s + 1 < steps:
            both = _bmm(jnp.concatenate([inv, p], axis=1), p)
            inv, p = inv + both[:, :n], both[:, n:]
        else:
            inv = inv + _bmm(inv, p)
    nmat = _bmm(inv, e)
    n2 = _bmm(nmat, nmat)
    m = eye + nmat + n2 + _bmm(nmat, n2)
    return _bmm(m, inv)


def _pack_kernel(shift, *refs):
    if shift == 0:
        a_ref, o_ref = refs
        o_ref[...] = a_ref[...].astype(BF16)
    else:
        a_ref, b_ref, o_ref = refs
        lane = _iota(a_ref.shape, 1)
        moved_a = pltpu.roll(a_ref[...], LANES - shift, 1)
        moved_b = pltpu.roll(b_ref[...], LANES - shift, 1)
        o_ref[...] = jnp.where(lane < LANES - shift, moved_a, moved_b).astype(BF16)


def _pack_cols(w3, layer, start, width):
    _, d, n = w3.shape
    q, shift = divmod(start, LANES)
    last_blk = pl.cdiv(n, LANES) - 1
    src = lambda off: pl.BlockSpec((None, d, LANES), lambda t: (layer, 0, jnp.minimum(q + t + off, last_blk)))
    ins = [w3] if shift == 0 else [w3, w3]
    return pl.pallas_call(
        functools.partial(_pack_kernel, shift),
        out_shape=jax.ShapeDtypeStruct((d, width), BF16),
        grid=(width // LANES,),
        in_specs=[src(0)] if shift == 0 else [src(0), src(1)],
        out_specs=pl.BlockSpec((d, LANES), lambda t: (0, t)),
        compiler_params=_cparams(("arbitrary",)),
        name="pack_cols",
    )(*ins)


INPROJ_TN = 512


def _inproj_kernel(starts, x_ref, g_ref, w_ref, *rest):
    outs, h_scr = rest[:-1], rest[-1]
    j = pl.program_id(1)

    @pl.when(j == 0)
    def _():
        x = x_ref[...]
        y = x * lax.rsqrt(jnp.mean(x * x, axis=-1, keepdims=True) + NORM_EPS) * g_ref[...]
        h_scr[...] = y.astype(BF16)

    for k, o_ref in enumerate(outs):
        @pl.when((j >= starts[k]) & (j < starts[k + 1]))
        def _(o_ref=o_ref):
            o_ref[...] = jnp.dot(h_scr[...], w_ref[...], preferred_element_type=F32).astype(o_ref.dtype)


def _pick_tile(n, prefs):
    for t in prefs:
        if n % t == 0:
            return t
    return n


def _inproj(x2, gain, w_bf16, widths):
    m, d = x2.shape
    tn = INPROJ_TN
    assert sum(widths) == w_bf16.shape[1] and all(w % tn == 0 for w in widths)
    tm = _pick_tile(m, (1024, 512, 256, 128, 64, 32, 16, 8))
    starts = tuple(int(s) for s in np.cumsum((0,) + tuple(w // tn for w in widths)))
    out_spec = lambda k: pl.BlockSpec(
        (tm, tn), lambda i, j: (i, jnp.clip(j - starts[k], 0, starts[k + 1] - starts[k] - 1)))
    return pl.pallas_call(
        functools.partial(_inproj_kernel, starts),
        out_shape=tuple(jax.ShapeDtypeStruct((m, w), BF16) for w in widths),
        grid=(m // tm, starts[-1]),
        in_specs=[
            pl.BlockSpec((tm, d), lambda i, j: (i, 0)),
            pl.BlockSpec((1, d), lambda i, j: (0, 0)),
            pl.BlockSpec((d, tn), lambda i, j: (0, j)),
        ],
        out_specs=tuple(out_spec(k) for k in range(len(widths))),
        scratch_shapes=[pltpu.VMEM((tm, d), BF16)],
        compiler_params=_cparams(("parallel", "arbitrary")),
        name="inproj",
    )(x2, gain.reshape(1, d), w_bf16)


def _rwkv_kernel(has_vres, tb, *refs):
    scr = refs[-14:]
    (carry, state, pl_scr, at_scr, rt_scr, bt_scr, kt_scr, v_scr, y_scr,
     t_scr, rb_scr, akv_scr, rkv_scr, btk_scr) = scr
    if has_vres:
        (p_ref, vf_ref, mu_ref, w0_ref, a0_ref, wwa_ref, gup_ref, kk_ref, ka_ref, rk_ref, lnw_ref, lnb_ref,
         hones_ref, v0_ref, vup_ref, o_ref) = refs[:-14]
    else:
        (p_ref, mu_ref, w0_ref, a0_ref, wwa_ref, gup_ref, kk_ref, ka_ref, rk_ref, lnw_ref, lnb_ref,
         hones_ref, o_ref, vout_ref) = refs[:-14]
    W = RW_WIDTH
    tstep = pl.program_id(1)

    @pl.when(tstep == 0)
    def _():
        carry[...] = jnp.zeros_like(carry)
        state[...] = jnp.zeros_like(state)

    row0 = _iota((tb, LANES), 0) == 0
    for c0 in range(0, RW_GROUP, LANES):
        x = p_ref[:, c0:c0 + LANES].astype(F32)
        prev = pltpu.roll(x, 1, 0)
        prev = jnp.where(row0, carry[PACKED_ROWS - 1:PACKED_ROWS, c0:c0 + LANES], prev)
        pl_scr[:, c0:c0 + LANES] = x + (prev - x) * mu_ref[:, c0:c0 + LANES]
    carry[...] = p_ref[tb - PACKED_ROWS:tb, :].astype(F32)

    head_sum = lambda t: _mm(t, hones_ref[...])

    c_wd = 3 * W
    x128 = pl_scr[:, c_wd:c_wd + LANES]
    lane = _iota((tb, LANES), 1)
    xin = jnp.where(lane < RW_DECAY_RANK, jnp.tanh(x128), x128)
    wa = jnp.dot(xin.astype(BF16), wwa_ref[...], preferred_element_type=F32)
    w_log = -jax.nn.softplus(-(w0_ref[...] + wa[:, :W])) - 0.5
    lw = -jnp.exp(w_log)
    asig = _sigmoid(a0_ref[...] + wa[:, W:])
    gd = pl_scr[:, c_wd + LANES:c_wd + 2 * LANES]
    g = jnp.dot(_sigmoid(gd).astype(BF16), gup_ref[...], preferred_element_type=F32)

    v = pl_scr[:, 2 * W:3 * W]
    if has_vres:
        vd = pl_scr[:, RW_COLS:RW_COLS + LANES]
        vl = jnp.dot(vd.astype(BF16), vup_ref[...], preferred_element_type=F32)
        v = v + (vf_ref[...] - v) * _sigmoid(v0_ref[...] + vl)
    else:
        vout_ref[...] = v
    v_scr[...] = v

    k = pl_scr[:, W:2 * W]
    kkraw = k * kk_ref[...]
    ssq = head_sum(kkraw * kkraw)
    kk = kkraw * lax.rsqrt(ssq + 1e-12)
    kfin = k * (1.0 + (asig - 1.0) * ka_ref[...])
    r = pl_scr[:, 0:W]
    bonus = head_sum(r * kfin * rk_ref[...]) * v

    C = CHUNK
    P2 = 2 * C
    NC, NP = tb // C, RW_HEADS // 2
    tril_b = (_iota((NC, C, C), 1) >= _iota((NC, C, C), 2)).astype(BF16)
    cum = sum(_bmm(tril_b, t.reshape(NC, C, W)) for t in _split3(lw)).reshape(tb, W)
    e_pos = jnp.exp(cum)
    e_neg = jnp.exp(-cum)
    rt_scr[...] = r * e_pos
    kt_scr[...] = kfin * e_neg
    bt_scr[...] = (kk * asig) * e_neg
    at_scr[...] = -kk * jnp.exp(cum - lw)

    def to_b(ref):
        return jnp.concatenate([ref[c * C:(c + 1) * C, p * LANES:(p + 1) * LANES][None]
                                for c in range(NC) for p in range(NP)], axis=0)

    row, col = _iota((1, P2, P2), 1), _iota((1, P2, P2), 2)
    same_head = (row // C) == (col // C)
    strict = same_head & ((row % C) > (col % C))
    incl = same_head & ((row % C) >= (col % C))
    left = _iota((1, C, LANES), 2) < RW_HEAD_DIM
    leftf = left.astype(F32)
    rightf = 1.0 - leftf

    def sel(x3):
        return jnp.where(left, x3[:, :C], x3[:, C:])

    at_b, rt_b, bt_b, kt_b, vp_b = to_b(at_scr), to_b(rt_scr), to_b(bt_scr), to_b(kt_scr), to_b(v_scr)
    lhs = jnp.concatenate([at_b * leftf, rt_b * leftf, at_b * rightf, rt_b * rightf], axis=1)
    out_ab = _bmm_nt(lhs, jnp.concatenate([bt_b, kt_b, kt_b, bt_b], axis=1))
    out_a, out_b = out_ab[:, :, :P2], out_ab[:, :, P2:]
    ab = jnp.where(strict, jnp.concatenate([out_a[:, 0:C], out_b[:, 2 * C:3 * C]], axis=1), 0.0)
    ak = jnp.where(strict, jnp.concatenate([out_b[:, 0:C], out_a[:, 2 * C:3 * C]], axis=1), 0.0)
    rb = jnp.where(incl, jnp.concatenate([out_a[:, C:2 * C], out_b[:, 3 * C:4 * C]], axis=1), 0.0)
    rk = jnp.where(incl, jnp.concatenate([out_b[:, C:2 * C], out_a[:, 3 * C:4 * C]], axis=1), 0.0)
    t_scr[...] = _neumann_inverse(ab)
    rb_scr[...] = rb
    vv = jnp.concatenate([vp_b, vp_b], axis=1)
    akv_scr[...] = sel(_bmm(ak, vv))
    rkv_scr[...] = sel(_bmm(rk, vv))
    pc_rows = jnp.concatenate([e_pos[(c + 1) * C - 1:(c + 1) * C, p * LANES:(p + 1) * LANES][None]
                               for c in range(NC) for p in range(NP)], axis=0)
    btk_scr[...] = jnp.concatenate([bt_b * pc_rows, kt_b * pc_rows], axis=1)

    for c in range(NC):
        items = slice(c * NP, (c + 1) * NP)
        s = state[...]
        a_c, r_c, v_c = at_b[items], rt_b[items], vp_b[items]
        ars = _bmm_nt(jnp.concatenate([a_c, r_c], axis=1), s)
        rhs = ars[:, :C] + akv_scr[items]
        u = sel(_bmm(t_scr[items], jnp.concatenate([rhs, rhs], axis=1)))
        y = ars[:, C:] + sel(_bmm(rb_scr[items], jnp.concatenate([u, u], axis=1))) + rkv_scr[items]
        s_new = s * pc_rows[items] + _bmm_tn(jnp.concatenate([u, v_c], axis=1), btk_scr[items])
        state[...] = jnp.where(same_head, s_new, 0.0)
        for p in range(NP):
            y_scr[c * C:(c + 1) * C, p * LANES:(p + 1) * LANES] = y[p]

    y = y_scr[...]
    inv_n = 1.0 / RW_HEAD_DIM
    y_hi = y.astype(BF16).astype(F32)
    mean = (head_sum(y_hi) + head_sum(y - y_hi)) * inv_n
    yc = y - mean
    var = head_sum(yc * yc) * inv_n
    yn = yc * lax.rsqrt(var + RW_GN_EPS) * lnw_ref[...] + lnb_ref[...]
    o_ref[...] = ((yn + bonus) * g).astype(o_ref.dtype)


def _rwkv_mixer(p_rw, v_first, prm, bsz, seq, has_vres):
    m = p_rw.shape[0]
    tb = _pick_tile(seq, (512, 256, 128, 64))
    nt = seq // tb
    nb = (tb // CHUNK) * (RW_HEADS // 2)
    W = RW_WIDTH
    row_spec = lambda width: pl.BlockSpec((tb, width), lambda b, t: (b * nt + t, 0))
    full = lambda a: pl.BlockSpec(a.shape, lambda b, t: (0,) * a.ndim)
    names = (["mu", "w0", "a0", "wwa", "gup", "kk", "ka", "rk", "lnw", "lnb", "hones"]
             + (["v0", "vup"] if has_vres else []))
    params = [prm[n] for n in names]
    ins = [p_rw] + ([v_first] if has_vres else []) + params
    in_specs = [row_spec(RW_GROUP)] + ([row_spec(W)] if has_vres else []) + [full(a) for a in params]
    if has_vres:
        out_shape = jax.ShapeDtypeStruct((m, W), BF16)
        out_specs = row_spec(W)
    else:
        out_shape = (jax.ShapeDtypeStruct((m, W), BF16), jax.ShapeDtypeStruct((m, W), F32))
        out_specs = (row_spec(W), row_spec(W))
    scratch = [
        pltpu.VMEM((PACKED_ROWS, RW_GROUP), F32),
        pltpu.VMEM((RW_HEADS // 2, LANES, LANES), F32),
        pltpu.VMEM((tb, RW_GROUP), F32),
    ] + [pltpu.VMEM((tb, W), F32) for _ in range(6)] + [
        pltpu.VMEM((nb, 2 * CHUNK, 2 * CHUNK), F32),
        pltpu.VMEM((nb, 2 * CHUNK, 2 * CHUNK), F32),
        pltpu.VMEM((nb, CHUNK, LANES), F32),
        pltpu.VMEM((nb, CHUNK, LANES), F32),
        pltpu.VMEM((nb, 2 * CHUNK, LANES), F32),
    ]
    res = pl.pallas_call(
        functools.partial(_rwkv_kernel, has_vres, tb),
        out_shape=out_shape,
        grid=(bsz, nt),
        in_specs=in_specs,
        out_specs=out_specs,
        scratch_shapes=scratch,
        compiler_params=_cparams(("parallel", "arbitrary")),
        name="rwkv7_vres" if has_vres else "rwkv7",
    )(*ins)
    if has_vres:
        return res, None
    return res


def _gdn_kernel(tb, pg_ref, slab_ref, cw_ref, alog_ref, dtb_ref, onorm_ref, eb_ref, ea_ref, o_ref,
                carry, state, ext, q_scr, k_scr, v_scr, grow_scr, o_scr,
                gcf_scr, kb_scr, vb_scr, kbeg_scr, qeg_scr, kt2_scr,
                intra_scr, u_scr, wq_scr, kt_scr):
    H, Dh, Wd = GDN_HEADS, GDN_HEAD_DIM, GDN_WIDTH
    QKV = 3 * Wd
    tstep = pl.program_id(1)

    @pl.when(tstep == 0)
    def _():
        carry[...] = jnp.zeros_like(carry)
        state[...] = jnp.zeros_like(state)

    ext[0:PACKED_ROWS, :] = carry[...]
    ext[PACKED_ROWS:, :] = pg_ref[:, 0:QKV].astype(F32)
    carry[...] = pg_ref[tb - PACKED_ROWS:tb, 0:QKV].astype(F32)
    for j in range(QKV // Dh):
        ln = slice(j * Dh, (j + 1) * Dh)
        acc = _causal_taps(ext[pl.ds(PACKED_ROWS - SUBLANES, tb + SUBLANES), ln], cw_ref[:, ln])
        act = acc * _sigmoid(acc)
        which, h = divmod(j, H)
        hl = slice(h * Dh, (h + 1) * Dh)
        if which == 0:
            nrm = lax.rsqrt(jnp.sum(act * act, axis=-1, keepdims=True) + 1e-6)
            q_scr[:, hl] = act * nrm * (Dh ** -0.5)
        elif which == 1:
            nrm = lax.rsqrt(jnp.sum(act * act, axis=-1, keepdims=True) + 1e-6)
            k_scr[:, hl] = act * nrm
        else:
            v_scr[:, hl] = act

    slab = slab_ref[...].astype(F32)
    gsl =-jnp.exp(alog_ref[...]) * jax.nn.softplus(slab + dtb_ref[...])
    lane = _iota((tb, LANES), 1)
    gsl = jnp.where((lane >= SLAB_ALPHA0) & (lane < SLAB_ALPHA0 + H), gsl, 0.0)

    C = CHUNK
    NC = tb // C
    trow, tcol = _iota((tb, tb), 0), _iota((tb, tb), 1)
    same_chunk = (trow // C) == (tcol // C)
    gcol = _mm_exact_lhs((same_chunk & (trow >= tcol)).astype(BF16), gsl)
    grow_scr[...] = _mm_tn_exact_rhs(gsl, (same_chunk & (trow <= tcol)).astype(BF16))

    def spread(x, e_ref):
        return sum(_mm(t, e_ref[...]) for t in _split3(x))

    beta_f = spread(_sigmoid(slab), eb_ref)
    gc_f = spread(gcol, ea_ref)
    eg_f = jnp.exp(gc_f)
    k2 = k_scr[...]
    kb2 = k2 * beta_f
    gc3 = gc_f.reshape(NC, C, Wd)
    gl3 = gc3[:, C - 1:C, :]
    gcf_scr[...] = gc_f
    kb_scr[...] = kb2
    vb_scr[...] = v_scr[...] * beta_f
    kbeg_scr[...] = kb2 * eg_f
    qeg_scr[...] = q_scr[...] * eg_f
    kt2_scr[...] = (k2.reshape(NC, C, Wd) * jnp.exp(gl3 - gc3)).reshape(tb, Wd)
    egl3 = jnp.exp(gl3)

    def items(fn):
        return jnp.concatenate([fn(c, h)[None] for c in range(NC) for h in range(H)], axis=0)

    rows = lambda c: slice(c * C, (c + 1) * C)
    head = lambda h: slice(h * Dh, (h + 1) * Dh)
    tile = lambda ref: items(lambda c, h: ref[rows(c), head(h)])
    gci = items(lambda c, h: gcf_scr[rows(c), h * Dh:h * Dh + C])
    gr = items(lambda c, h: grow_scr[SLAB_ALPHA0 + h:SLAB_ALPHA0 + h + 1, rows(c)])
    eg_last = jnp.concatenate([egl3[c:c + 1, :, head(h)] for c in range(NC) for h in range(H)], axis=0)
    k_b = tile(k_scr)
    row, col = _iota((1, C, C), 1), _iota((1, C, C), 2)
    dec = jnp.exp(jnp.where(row >= col, gci - gr, -jnp.inf))
    prod = _bmm_nt(jnp.concatenate([tile(kb_scr), tile(q_scr)], axis=1), k_b)
    lmat = jnp.where(row > col, prod[:, :C] * dec, 0.0)
    intra_scr[...] = prod[:, C:] * dec
    nb = NC * H
    l2 = (-lmat).reshape(nb // 2, 2 * C, C)
    prow, pcol = _iota((1, 2 * C, 2 * C), 1), _iota((1, 2 * C, 2 * C), 2)
    lp = jnp.where((prow // C) == (pcol // C), jnp.concatenate([l2, l2], axis=2), 0.0)
    tp = _neumann_inverse(lp)
    rhs = jnp.concatenate([tile(vb_scr), tile(kbeg_scr)], axis=2).reshape(nb // 2, 2 * C, 2 * Dh)
    sol = _bmm(tp, rhs).reshape(nb, C, 2 * Dh)
    u_scr[...] = sol[:, :, :Dh]
    wq_scr[...] = jnp.concatenate([sol[:, :, Dh:], tile(qeg_scr)], axis=1)
    kt_scr[...] = tile(kt2_scr)

    for c in range(NC):
        it = slice(c * H, (c + 1) * H)
        s = state[...]
        ws = _bmm(wq_scr[it], s)
        v_new = u_scr[it] - ws[:, :C]
        o = ws[:, C:] + _bmm(intra_scr[it], v_new)
        state[...] = s * eg_last[it] + _bmm_tn(kt_scr[it], v_new)
        for h in range(H):
            o_scr[rows(c), head(h)] = o[h]

    for h in range(H):
        hl = slice(h * Dh, (h + 1) * Dh)
        o = o_scr[:, hl]
        o = o * lax.rsqrt(jnp.mean(o * o, axis=-1, keepdims=True) + NORM_EPS) * onorm_ref[...]
        z = pg_ref[:, QKV + h * Dh:QKV + (h + 1) * Dh].astype(F32)
        o_ref[:, hl] = (o * (z * _sigmoid(z))).astype(o_ref.dtype)


def _gdn_mixer(p_gdn, p_rw, prm, bsz, seq):
    m = p_gdn.shape[0]
    tb = _pick_tile(seq, (256, 128))
    nt = seq // tb
    nb = (tb // CHUNK) * GDN_HEADS
    Wd = GDN_WIDTH
    lane_head = np.arange(Wd)[None, :] // GDN_HEAD_DIM
    spread = lambda off: jnp.asarray(np.arange(LANES)[:, None] - off == lane_head, BF16)
    params = [prm["cw"], prm["alog"], prm["dtb"], prm["onorm"], spread(SLAB_BETA0), spread(SLAB_ALPHA0)]
    full = lambda a: pl.BlockSpec(a.shape, lambda b, t: (0,) * a.ndim)
    slab_blk = RW_COLS // LANES
    return pl.pallas_call(
        functools.partial(_gdn_kernel, tb),
        out_shape=jax.ShapeDtypeStruct((m, Wd), BF16),
        grid=(bsz, nt),
        in_specs=[pl.BlockSpec((tb, 4 * Wd), lambda b, t: (b * nt + t, 0)),
                  pl.BlockSpec((tb, LANES), lambda b, t: (b * nt + t, slab_blk))] + [full(a) for a in params],
        out_specs=pl.BlockSpec((tb, Wd), lambda b, t: (b * nt + t, 0)),
        scratch_shapes=[
            pltpu.VMEM((PACKED_ROWS, 3 * Wd), F32),
            pltpu.VMEM((GDN_HEADS, GDN_HEAD_DIM, GDN_HEAD_DIM), F32),
            pltpu.VMEM((tb + PACKED_ROWS, 3 * Wd), F32),
            pltpu.VMEM((tb, Wd), F32), pltpu.VMEM((tb, Wd), F32), pltpu.VMEM((tb, Wd), F32),
            pltpu.VMEM((LANES, tb), F32),
            pltpu.VMEM((tb, Wd), F32),
        ] + [pltpu.VMEM((tb, Wd), F32) for _ in range(6)] + [
            pltpu.VMEM((nb, CHUNK, CHUNK), F32),
            pltpu.VMEM((nb, CHUNK, GDN_HEAD_DIM), F32),
            pltpu.VMEM((nb, 2 * CHUNK, GDN_HEAD_DIM), F32),
            pltpu.VMEM((nb, CHUNK, GDN_HEAD_DIM), F32),
        ],
        compiler_params=_cparams(("parallel", "arbitrary")),
        name="gdn",
    )(p_gdn, p_rw, *params)


def _swa_kernel(nblk, ps_ref, pos_ref, freq_ref, sgn_lo_ref, sgn_hi_ref, qn_ref, kn_ref, sink_ref, ones_ref, o_ref,
                kprev, vprev):
    Wn = SWA_WINDOW
    hd = SWA_HEAD_DIM
    G = SWA_Q_HEADS // SWA_KV_HEADS
    QW, KW = SWA_WIDTH, SWA_KV_WIDTH
    n = pl.program_id(1)

    @pl.when(n == 0)
    def _():
        kprev[...] = jnp.zeros_like(kprev)
        vprev[...] = jnp.zeros_like(vprev)

    half = ROPE_DIM // 2
    ang = pos_ref[...] * freq_ref[...]
    cs1, sn1 = jnp.cos(ang), jnp.sin(ang)
    lo1, hi1 = sn1 * sgn_lo_ref[...], sn1 * sgn_hi_ref[...]

    def norm_rope(x, gain_row, width):
        rep = lambda t: jnp.concatenate([t] * (width // LANES), axis=1)
        ms = _mm(x * x, ones_ref[0:width, 0:width]) * (1.0 / hd)
        y = x * lax.rsqrt(ms + NORM_EPS) * gain_row
        up = pltpu.roll(y, width - half, 1)
        dn = pltpu.roll(y, half, 1)
        return y * rep(cs1) + up * rep(lo1) + dn * rep(hi1)

    q = norm_rope(ps_ref[:, 0:QW].astype(F32), qn_ref[...], QW) * (hd ** -0.5)
    k = norm_rope(ps_ref[:, QW:QW + KW].astype(F32), kn_ref[...], KW)
    v = ps_ref[:, QW + KW:QW + 2 * KW].astype(F32)

    qi = _iota((G * Wn, 2 * Wn), 0) % Wn
    kj = _iota((G * Wn, 2 * Wn), 1)
    rel = qi + Wn - kj
    in_window = (rel >= 0) & (rel < SWA_WINDOW)
    grp = _iota((G * Wn, 1), 0) // Wn
    sinks = sink_ref[...]
    first_half = _iota((G * Wn, LANES), 1) < hd
    side_mask = (first_half.astype(F32), 1.0 - first_half.astype(F32))
    ones_kv = jnp.ones((2 * Wn, LANES), BF16)

    for blk in range(nblk):
        rows = slice(blk * Wn, (blk + 1) * Wn)
        before = slice((blk - 1) * Wn, blk * Wn)
        allowed = in_window & ((kj >= Wn) | (n > 0)) if blk == 0 else in_window
        for pair in range(SWA_KV_HEADS // 2):
            kl = slice(pair * LANES, (pair + 1) * LANES)
            k_before = kprev[:, kl] if blk == 0 else k[before, kl]
            v_before = vprev[:, kl] if blk == 0 else v[before, kl]
            kband = jnp.concatenate([k_before, k[rows, kl]], axis=0).astype(BF16)
            vband = jnp.concatenate([v_before, v[rows, kl]], axis=0).astype(BF16)
            qs = jnp.concatenate([q[rows, (pair * G + t) * LANES:(pair * G + t + 1) * LANES] for t in range(G)],
                                 axis=0)
            outs = []
            for side in range(2):
                s = jnp.where(allowed, _mm_nt(qs * side_mask[side], kband), -jnp.inf)
                sink = jnp.zeros((G * Wn, 1), F32)
                for t in range(G):
                    h = SWA_Q_ORDER[(pair * G + t) * 2 + side]
                    sink = jnp.where(grp == t, sinks[:, h:h + 1], sink)
                mx = jnp.maximum(jnp.max(s, axis=-1, keepdims=True), sink)
                p = jnp.exp(s - mx).astype(BF16)
                den = jnp.dot(p, ones_kv, preferred_element_type=F32) + jnp.exp(sink - mx)
                outs.append(jnp.dot(p, vband, preferred_element_type=F32) / den)
            o = jnp.where(first_half, outs[0], outs[1])
            for t in range(G):
                o_ref[rows, (pair * G + t) * LANES:(pair * G + t + 1) * LANES] = (
                    o[t * Wn:(t + 1) * Wn].astype(o_ref.dtype))

    last = slice((nblk - 1) * Wn, nblk * Wn)
    kprev[...] = k[last, :]
    vprev[...] = v[last, :]


def _swa_mixer(p_swa, pos_col, prm, bsz, seq):
    m = p_swa.shape[0]
    Wn = SWA_WINDOW
    nblk = next(c for c in (4, 2, 1) if seq % (c * Wn) == 0)
    rows = nblk * Wn
    nb = seq // rows
    head_ones = _head_ones(SWA_WIDTH, SWA_HEAD_DIM)
    params = [prm["freq"], prm["sgn_lo"], prm["sgn_hi"], prm["qn"], prm["kn"], prm["sinks"], head_ones]
    full = lambda a: pl.BlockSpec(a.shape, lambda b, t: (0,) * a.ndim)
    return pl.pallas_call(
        functools.partial(_swa_kernel, nblk),
        out_shape=jax.ShapeDtypeStruct((m, SWA_WIDTH), BF16),
        grid=(bsz, nb),
        in_specs=[pl.BlockSpec((rows, SWA_COLS), lambda b, t: (b * nb + t, 0)),
                  pl.BlockSpec((rows, 1), lambda b, t: (b * nb + t, 0))] + [full(a) for a in params],
        out_specs=pl.BlockSpec((rows, SWA_WIDTH), lambda b, t: (b * nb + t, 0)),
        scratch_shapes=[pltpu.VMEM((Wn, SWA_KV_WIDTH), F32), pltpu.VMEM((Wn, SWA_KV_WIDTH), F32)],
        compiler_params=_cparams(("parallel", "arbitrary")),
        name="swa",
    )(p_swa, pos_col, *params)


def _gated_kernel(x_ref, gain_ref, orw_ref, ogdn_ref, oswa_ref, wg0_ref, wg1_ref, wg2_ref,
                  wb0_ref, wb1_ref, wb2_ref, o_ref, h_scr):
    @pl.when(pl.program_id(1) == 0)
    def _():
        x = x_ref[...]
        y = x * lax.rsqrt(jnp.mean(x * x, axis=-1, keepdims=True) + NORM_EPS) * gain_ref[...]
        h_scr[...] = y.astype(BF16)

    h = h_scr[...]
    branch = lambda wg_ref, a_ref, wb_ref: (
        _sigmoid(jnp.dot(h, wg_ref[...], preferred_element_type=F32))
        * jnp.dot(a_ref[...], wb_ref[...], preferred_element_type=F32))
    merged = (branch(wg0_ref, orw_ref, wb0_ref) + branch(wg1_ref, ogdn_ref, wb1_ref)
              + branch(wg2_ref, oswa_ref, wb2_ref))
    o_ref[...] = merged.astype(BF16)


def _residual_matmul_kernel(a_ref, w_ref, x_ref, o_ref):
    o_ref[...] = x_ref[...] + jnp.dot(a_ref[...], w_ref[...], preferred_element_type=F32)


def _residual_matmul(a_bf16, w3_bf16, layer, x2, name):
    m, k = a_bf16.shape
    d = x2.shape[1]
    tm = _pick_tile(m, (1024, 512, 256, 128, 64, 32, 16, 8))
    tn = _pick_tile(d, (512, 256, 128))
    return pl.pallas_call(
        _residual_matmul_kernel,
        out_shape=jax.ShapeDtypeStruct((m, d), F32),
        grid=(m // tm, d // tn),
        in_specs=[
            pl.BlockSpec((tm, k), lambda i, j: (i, 0)),
            pl.BlockSpec((None, k, tn), lambda i, j: (layer, 0, j)),
            pl.BlockSpec((tm, tn), lambda i, j: (i, j)),
        ],
        out_specs=pl.BlockSpec((tm, tn), lambda i, j: (i, j)),
        compiler_params=_cparams(("parallel", "arbitrary")),
        name=name,
    )(a_bf16, w3_bf16, x2)


def _merge(x2, gain, o_rw, o_gdn, o_swa, w_gate, wb_rw, wb_gdn, wb_swa, w_out, layer):
    m, d = x2.shape
    tm = _pick_tile(m, (1024, 512, 256, 128, 64, 32, 16, 8))
    tn = 512
    nj = d // tn
    rows = lambda width: pl.BlockSpec((tm, width), lambda i, j: (i, 0))
    gate = lambda b: pl.BlockSpec((d, tn), lambda i, j, b=b: (0, b * nj + j))
    wcol = lambda kdim: pl.BlockSpec((kdim, tn), lambda i, j: (0, j))
    merged = pl.pallas_call(
        _gated_kernel,
        out_shape=jax.ShapeDtypeStruct((m, d), BF16),
        grid=(m // tm, nj),
        in_specs=[rows(d), pl.BlockSpec((1, d), lambda i, j: (0, 0)),
                  rows(RW_WIDTH), rows(GDN_WIDTH), rows(SWA_WIDTH), gate(0), gate(1), gate(2),
                  wcol(RW_WIDTH), wcol(GDN_WIDTH), wcol(SWA_WIDTH)],
        out_specs=pl.BlockSpec((tm, tn), lambda i, j: (i, j)),
        scratch_shapes=[pltpu.VMEM((tm, d), BF16)],
        compiler_params=_cparams(("parallel", "arbitrary")),
        name="gated_branches",
    )(x2, gain.reshape(1, d), o_rw, o_gdn, o_swa, w_gate, w_gate, w_gate, wb_rw, wb_gdn, wb_swa)
    return _residual_matmul(merged, w_out, layer, x2, "w_out")


def _ffn_up_kernel(tm, seq, x_ref, halo_ref, gain_ref, wg_ref, wu_ref, cg_ref, cu_ref, act_ref, h_scr, u_scr):
    i = pl.program_id(0)
    j = pl.program_id(1)
    tn = wg_ref.shape[1]

    @pl.when(j == 0)
    def _():
        def norm(x):
            return (x * lax.rsqrt(jnp.mean(x * x, axis=-1, keepdims=True) + NORM_EPS) * gain_ref[...]).astype(BF16)
        seq_start = (i * tm) % seq == 0
        halo = jnp.where(seq_start, 0.0, halo_ref[...])
        h_scr[0:2 * SUBLANES, :] = norm(jnp.concatenate([jnp.zeros_like(halo), halo], axis=0))
        h_scr[2 * SUBLANES:, :] = norm(x_ref[...])

    h = h_scr[...]
    ts = FFN_SUBTILE
    for s in range(tn // ts):
        cols = slice(s * ts, (s + 1) * ts)

        def conv(w_ref, cw_ref, base):
            u_scr[:, base:base + ts] = jnp.dot(h, w_ref[:, cols], preferred_element_type=F32)
            return _causal_taps(u_scr[pl.ds(SUBLANES, tm + SUBLANES), base:base + ts], cw_ref[:, cols])

        gate = conv(wg_ref, cg_ref, 2 * s * ts)
        up = conv(wu_ref, cu_ref, (2 * s + 1) * ts)
        act_ref[:, cols] = (gate * _sigmoid(gate) * up).astype(BF16)


def _conv_ffn(x2, gain, w_up2, c_up2, w_down, layer, seq):
    m, d = x2.shape
    dff = w_down.shape[1]
    tm = _pick_tile(seq, (1024, 512, 256, 128, 64, 32, 16))
    tn = _pick_tile(dff, (512, 256, 128))
    nj = dff // tn
    hb = tm // SUBLANES
    act = pl.pallas_call(
        functools.partial(_ffn_up_kernel, tm, seq),
        out_shape=jax.ShapeDtypeStruct((m, dff), BF16),
        grid=(m // tm, dff // tn),
        in_specs=[
            pl.BlockSpec((tm, d), lambda i, j: (i, 0)),
            pl.BlockSpec((SUBLANES, d), lambda i, j: (jnp.maximum(i * hb - 1, 0), 0)),
            pl.BlockSpec((1, d), lambda i, j: (0, 0)),
            pl.BlockSpec((None, d, tn), lambda i, j: (layer, 0, j)),
            pl.BlockSpec((None, d, tn), lambda i, j: (layer, 0, nj + j)),
            pl.BlockSpec((FFN_CONV, tn), lambda i, j: (0, j)),
            pl.BlockSpec((FFN_CONV, tn), lambda i, j: (0, nj + j)),
        ],
        out_specs=pl.BlockSpec((tm, tn), lambda i, j: (i, j)),
        scratch_shapes=[pltpu.VMEM((tm + 2 * SUBLANES, d), BF16),
                        pltpu.VMEM((tm + 2 * SUBLANES, 2 * tn), F32)],
        compiler_params=_cparams(("parallel", "arbitrary")),
        name="ffn_up",
    )(x2, x2, gain.reshape(1, d), w_up2, w_up2, c_up2, c_up2)
    return _residual_matmul(act, w_down, layer, x2, "ffn_down")


def _row(a):
    return a.reshape(1, -1).astype(F32)


def _head_ones(width, head_dim):
    idx = np.arange(width) // head_dim
    return jnp.asarray(idx[:, None] == idx[None, :], BF16)


def _swa_head_order(a, axis):
    hd = SWA_HEAD_DIM
    take = lambda h: lax.slice_in_dim(a, h * hd, (h + 1) * hd, axis=axis)
    return jnp.concatenate([take(h) for h in SWA_Q_ORDER], axis=axis)


def _pad_cols(a, width):
    return jnp.pad(a, ((0, 0), (0, width - a.shape[1])))


def _layer_params(l, w_in, rw_mu, rw_w0, rw_w_up, rw_a0, rw_a_up, rw_g_up, rw_k_k, rw_k_a, rw_r_k, rw_ln_w,
                  rw_ln_b, vres_down, vres_mu, vres_v0, vres_v_up, gdn_conv, gdn_a_log, gdn_dt_bias, gdn_o_norm,
                  swa_q_norm, swa_k_norm, swa_sinks):
    d = w_in.shape[1]
    c_gdn = RW_COLS
    c_ba = c_gdn + 4 * GDN_WIDTH
    c_swa = c_ba + 2 * GDN_HEADS
    c_gate = c_swa + SWA_COLS
    has_vres = l > 0
    vres_w = vres_down[l - 1] if has_vres else jnp.zeros((d, RW_VRES_RANK), F32)
    assert c_ba % LANES == 0
    ba_w = _pack_cols(w_in, l, c_ba, LANES)[:, :2 * GDN_HEADS]
    slab_w = _pad_cols(jnp.concatenate([vres_w.astype(BF16), ba_w], axis=1), LANES)
    w_rw = jnp.concatenate([_pack_cols(w_in, l, 0, RW_COLS), slab_w], axis=1)
    w_gdn = _pack_cols(w_in, l, c_gdn, 4 * GDN_WIDTH)
    w_swa = _pack_cols(w_in, l, c_swa, SWA_COLS)
    w_swa = jnp.concatenate([_swa_head_order(w_swa[:, :SWA_WIDTH], axis=1), w_swa[:, SWA_WIDTH:]], axis=1)
    w_gate = _pack_cols(w_in, l, c_gate, w_in.shape[2] - c_gate)

    W = RW_WIDTH
    mu = jnp.concatenate([rw_mu[l], vres_mu[l - 1] if has_vres else jnp.zeros((RW_VRES_RANK,), F32),
                          jnp.zeros((LANES - RW_VRES_RANK,), F32)])
    wwa = jnp.zeros((LANES, 2 * W), F32)
    wwa = wwa.at[:RW_DECAY_RANK, :W].set(rw_w_up[l]).at[RW_DECAY_RANK:, W:].set(rw_a_up[l])
    rw = dict(mu=_row(mu), w0=_row(rw_w0[l]), a0=_row(rw_a0[l]), wwa=wwa.astype(BF16),
              gup=rw_g_up[l].astype(BF16), kk=_row(rw_k_k[l]), ka=_row(rw_k_a[l]), rk=_row(rw_r_k[l]),
              lnw=_row(rw_ln_w[l]), lnb=_row(rw_ln_b[l]), hones=_head_ones(W, RW_HEAD_DIM))
    if has_vres:
        vup = jnp.zeros((LANES, W), F32).at[:RW_VRES_RANK].set(vres_v_up[l - 1])
        rw.update(v0=_row(vres_v0[l - 1]), vup=vup.astype(BF16))

    lanes_pad = lambda a, off: jnp.zeros((1, LANES), F32).at[0, off:off + a.shape[0]].set(a)
    gdn = dict(cw=gdn_conv[l].astype(F32), alog=lanes_pad(gdn_a_log[l], SLAB_ALPHA0),
               dtb=lanes_pad(gdn_dt_bias[l], SLAB_ALPHA0), onorm=_row(gdn_o_norm[l]))

    half = ROPE_DIM // 2
    lane = np.arange(LANES) % SWA_HEAD_DIM
    inv_freq = ROPE_THETA ** (-(lane % half).astype(np.float32) * 2.0 / ROPE_DIM)
    freq = np.where(lane < ROPE_DIM, inv_freq, 0.0).astype(np.float32)
    sgn_lo = np.where(lane < half, -1.0, 0.0).astype(np.float32)
    sgn_hi = np.where((lane >= half) & (lane < ROPE_DIM), 1.0, 0.0).astype(np.float32)
    swa = dict(freq=jnp.asarray(freq).reshape(1, -1), sgn_lo=jnp.asarray(sgn_lo).reshape(1, -1),
               sgn_hi=jnp.asarray(sgn_hi).reshape(1, -1),
               qn=_row(jnp.tile(swa_q_norm[l], SWA_Q_HEADS)), kn=_row(jnp.tile(swa_k_norm[l], SWA_KV_HEADS)),
               sinks=lanes_pad(swa_sinks[l], 0))
    groups = (w_rw, w_gdn, w_swa)
    mix_widths = tuple(pl.cdiv(g.shape[1], INPROJ_TN) * INPROJ_TN for g in groups)
    w_mix = jnp.concatenate([_pad_cols(g, wd) for g, wd in zip(groups, mix_widths)], axis=1)
    return dict(w_mix=w_mix, mix_widths=mix_widths, w_gate=w_gate, rw=rw, gdn=gdn, swa=swa, has_vres=has_vres)


def kernel(x, positions, norm_mix, w_in, rw_mu, rw_w0, rw_w_up, rw_a0, rw_a_up, rw_g_up, rw_k_k, rw_k_a, rw_r_k, rw_ln_w, rw_ln_b, vres_down, vres_mu, vres_v0, vres_v_up, gdn_conv, gdn_a_log, gdn_dt_bias, gdn_o_norm, swa_q_norm, swa_k_norm, swa_sinks, w_branch, w_out, norm_ffn, ffn_up, ffn_conv, ffn_down):
    bsz, seq, d = x.shape
    depth = w_in.shape[0]
    x2 = x.reshape(bsz * seq, d)
    pos_col = positions.astype(F32).reshape(bsz * seq, 1)
    v_first = None
    ffn_up_bf16, ffn_down_bf16, w_out_bf16 = ffn_up.astype(BF16), ffn_down.astype(BF16), w_out.astype(BF16)
    for l in range(depth):
        lp = _layer_params(l, w_in, rw_mu, rw_w0, rw_w_up, rw_a0, rw_a_up, rw_g_up, rw_k_k, rw_k_a, rw_r_k,
                           rw_ln_w, rw_ln_b, vres_down, vres_mu, vres_v0, vres_v_up, gdn_conv, gdn_a_log,
                           gdn_dt_bias, gdn_o_norm, swa_q_norm, swa_k_norm, swa_sinks)
        p_rw, p_gdn, p_swa = _inproj(x2, norm_mix[l], lp["w_mix"], lp["mix_widths"])
        if lp["has_vres"]:
            o_rw, _ = _rwkv_mixer(p_rw, v_first, lp["rw"], bsz, seq, True)
        else:
            o_rw, v_first = _rwkv_mixer(p_rw, None, lp["rw"], bsz, seq, False)
        o_gdn = _gdn_mixer(p_gdn, p_rw, lp["gdn"], bsz, seq)
        o_swa = _swa_mixer(p_swa, pos_col, lp["swa"], bsz, seq)
        wb = w_branch[l].astype(BF16)
        x2 = _merge(x2, norm_mix[l], o_rw, o_gdn, o_swa, lp["w_gate"], wb[:RW_WIDTH], wb[RW_WIDTH:RW_WIDTH + GDN_WIDTH],
                    _swa_head_order(wb[RW_WIDTH + GDN_WIDTH:], axis=0), w_out_bf16, l)
        x2 = _conv_ffn(x2, norm_ffn[l], ffn_up_bf16, ffn_conv[l], ffn_down_bf16, l, seq)
    return x2.reshape(bsz, seq, d)
```

```python
import functools

import jax
import jax.numpy as jnp
import numpy as np
from jax import lax
from jax.experimental import pallas as pl
from jax.experimental.pallas import tpu as pltpu

F32 = jnp.float32
BF16 = jnp.bfloat16

D_MODEL = 2048
RW_HEADS, RW_HEAD_DIM = 8, 64
RW_WIDTH = RW_HEADS * RW_HEAD_DIM
RW_DECAY_RANK, RW_ICLR_RANK, RW_GATE_RANK, RW_VRES_RANK = 64, 64, 128, 32
RW_GN_EPS = 64e-5
RW_COLS = 3 * RW_WIDTH + RW_DECAY_RANK + RW_ICLR_RANK + RW_GATE_RANK
GDN_HEADS, GDN_HEAD_DIM = 6, 128
GDN_WIDTH = GDN_HEADS * GDN_HEAD_DIM
GDN_CONV = 4
SWA_Q_HEADS, SWA_KV_HEADS, SWA_HEAD_DIM = 12, 4, 64
SWA_WIDTH = SWA_Q_HEADS * SWA_HEAD_DIM
SWA_KV_WIDTH = SWA_KV_HEADS * SWA_HEAD_DIM
SWA_WINDOW = 128
SWA_COLS = SWA_WIDTH + 2 * SWA_KV_WIDTH
ROPE_DIM = SWA_HEAD_DIM // 4
ROPE_THETA = 500000.0
D_FF = 5632
FFN_CONV = 3
NORM_EPS = 1e-6

LANES = 128
SUBLANES = 8
PACKED_ROWS = 16
VMEM_LIMIT_BYTES = 56 * 1024 * 1024

CHUNK = 64
INV_BLOCK = 16
FFN_SUBTILE = 512
SLAB_VRES0 = 0
SLAB_BETA0 = RW_VRES_RANK
SLAB_ALPHA0 = RW_VRES_RANK + GDN_HEADS
RW_GROUP = RW_COLS + LANES
_G = SWA_Q_HEADS // SWA_KV_HEADS
SWA_Q_ORDER = tuple((2 * p + side) * _G + t for p in range(SWA_KV_HEADS // 2) for t in range(_G) for side in range(2))

def _cparams(sem):
    return pltpu.CompilerParams(dimension_semantics=sem, vmem_limit_bytes=VMEM_LIMIT_BYTES)


def _dg(a, b, dims):
    return lax.dot_general(a.astype(BF16), b.astype(BF16), dims, preferred_element_type=F32)


def _mm(a, b):
    return _dg(a, b, (((1,), (0,)), ((), ())))


def _mm_nt(a, b):
    return _dg(a, b, (((1,), (1,)), ((), ())))


def _mm_tn(a, b):
    return _dg(a, b, (((0,), (0,)), ((), ())))


def _bmm(a, b):
    return _dg(a, b, (((2,), (1,)), ((0,), (0,))))


def _bmm_nt(a, b):
    return _dg(a, b, (((2,), (2,)), ((0,), (0,))))


def _bmm_tn(a, b):
    return _dg(a, b, (((1,), (1,)), ((0,), (0,))))


def _split3(a):
    hi = a.astype(BF16)
    r1 = a - hi.astype(F32)
    mid = r1.astype(BF16)
    lo = (r1 - mid.astype(F32)).astype(BF16)
    return hi, mid, lo


def _mm_exact_lhs(a_bf16, b):
    hi, mid, lo = _split3(b)
    d = lambda t: jnp.dot(a_bf16, t, preferred_element_type=F32)
    return d(hi) + d(mid) + d(lo)


def _mm_tn_exact_rhs(a, b_bf16):
    hi, mid, lo = _split3(a)
    d = lambda t: _mm_tn(t, b_bf16)
    return d(hi) + d(mid) + d(lo)


def _iota(shape, dim):
    return lax.broadcasted_iota(jnp.int32, shape, dim)


def _sigmoid(x):
    return 0.5 * jnp.tanh(0.5 * x) + 0.5


def _causal_taps(x, taps):
    k = taps.shape[0]
    acc = x * taps[k - 1:k, :]
    for s in range(1, k):
        acc = acc + pltpu.roll(x, s, 0) * taps[k - 1 - s:k - s, :]
    return acc[SUBLANES:, :]


def _neumann_inverse(lmat):
    assert CHUNK // INV_BLOCK == 4
    n = lmat.shape[-1]
    row, col = _iota((1, n, n), 1), _iota((1, n, n), 2)
    eye = (row == col).astype(F32)
    in_blk = (row // INV_BLOCK) == (col // INV_BLOCK)
    lb = jnp.where(in_blk, lmat, 0.0)
    e = lmat - lb
    inv = eye + lb
    p = _bmm(lb, lb)
    steps = INV_BLOCK.bit_length() - 2
    for s in range(steps):
        if s + 1 < steps:
            both = _bmm(jnp.concatenate([inv, p], axis=1), p)
            inv, p = inv + both[:, :n], both[:, n:]
        else:
            inv = inv + _bmm(inv, p)
    nmat = _bmm(inv, e)
    n2 = _bmm(nmat, nmat)
    m = eye + nmat + n2 + _bmm(nmat, n2)
    return _bmm(m, inv)


def _pack_kernel(shift, *refs):
    if shift == 0:
        a_ref, o_ref = refs
        o_ref[...] = a_ref[...].astype(BF16)
    else:
        a_ref, b_ref, o_ref = refs
        lane = _iota(a_ref.shape, 1)
        moved_a = pltpu.roll(a_ref[...], LANES - shift, 1)
        moved_b = pltpu.roll(b_ref[...], LANES - shift, 1)
        o_ref[...] = jnp.where(lane < LANES - shift, moved_a, moved_b).astype(BF16)


def _pack_cols(w3, layer, start, width):
    _, d, n = w3.shape
    q, shift = divmod(start, LANES)
    last_blk = pl.cdiv(n, LANES) - 1
    src = lambda off: pl.BlockSpec((None, d, LANES), lambda t: (layer, 0, jnp.minimum(q + t + off, last_blk)))
    ins = [w3] if shift == 0 else [w3, w3]
    return pl.pallas_call(
        functools.partial(_pack_kernel, shift),
        out_shape=jax.ShapeDtypeStruct((d, width), BF16),
        grid=(width // LANES,),
        in_specs=[src(0)] if shift == 0 else [src(0), src(1)],
        out_specs=pl.BlockSpec((d, LANES), lambda t: (0, t)),
        compiler_params=_cparams(("arbitrary",)),
        name="pack_cols",
    )(*ins)


INPROJ_TN = 512


def _inproj_kernel(starts, x_ref, g_ref, w_ref, *rest):
    outs, h_ref, h_scr = rest[:-2], rest[-2], rest[-1]
    j = pl.program_id(1)

    @pl.when(j == 0)
    def _():
        x = x_ref[...]
        y = x * lax.rsqrt(jnp.mean(x * x, axis=-1, keepdims=True) + NORM_EPS) * g_ref[...]
        h_scr[...] = y.astype(BF16)
        h_ref[...] = y.astype(BF16)

    for k, o_ref in enumerate(outs):
        @pl.when((j >= starts[k]) & (j < starts[k + 1]))
        def _(o_ref=o_ref):
            o_ref[...] = jnp.dot(h_scr[...], w_ref[...], preferred_element_type=F32).astype(o_ref.dtype)


def _pick_tile(n, prefs):
    for t in prefs:
        if n % t == 0:
            return t
    return n


def _inproj(x2, gain, w_bf16, widths):
    m, d = x2.shape
    tn = INPROJ_TN
    assert sum(widths) == w_bf16.shape[1] and all(w % tn == 0 for w in widths)
    tm = _pick_tile(m, (1024, 512, 256, 128, 64, 32, 16, 8))
    starts = tuple(int(s) for s in np.cumsum((0,) + tuple(w // tn for w in widths)))
    out_spec = lambda k: pl.BlockSpec(
        (tm, tn), lambda i, j: (i, jnp.clip(j - starts[k], 0, starts[k + 1] - starts[k] - 1)))
    return pl.pallas_call(
        functools.partial(_inproj_kernel, starts),
        out_shape=tuple(jax.ShapeDtypeStruct((m, w), BF16) for w in widths) + (jax.ShapeDtypeStruct((m, d), BF16),),
        grid=(m // tm, starts[-1]),
        in_specs=[
            pl.BlockSpec((tm, d), lambda i, j: (i, 0)),
            pl.BlockSpec((1, d), lambda i, j: (0, 0)),
            pl.BlockSpec((d, tn), lambda i, j: (0, j)),
        ],
        out_specs=tuple(out_spec(k) for k in range(len(widths))) + (pl.BlockSpec((tm, d), lambda i, j: (i, 0)),),
        scratch_shapes=[pltpu.VMEM((tm, d), BF16)],
        compiler_params=_cparams(("parallel", "arbitrary")),
        name="inproj",
    )(x2, gain.reshape(1, d), w_bf16)


def _rwkv_kernel(has_vres, tb, *refs):
    scr = refs[-14:]
    (carry, state, pl_scr, at_scr, rt_scr, bt_scr, kt_scr, v_scr, y_scr,
     t_scr, rb_scr, akv_scr, rkv_scr, btk_scr) = scr
    if has_vres:
        (p_ref, vf_ref, mu_ref, w0_ref, a0_ref, wwa_ref, gup_ref, kk_ref, ka_ref, rk_ref, lnw_ref, lnb_ref,
         hones_ref, v0_ref, vup_ref, o_ref) = refs[:-14]
    else:
        (p_ref, mu_ref, w0_ref, a0_ref, wwa_ref, gup_ref, kk_ref, ka_ref, rk_ref, lnw_ref, lnb_ref,
         hones_ref, o_ref, vout_ref) = refs[:-14]
    W = RW_WIDTH
    tstep = pl.program_id(1)

    @pl.when(tstep == 0)
    def _():
        carry[...] = jnp.zeros_like(carry)
        state[...] = jnp.zeros_like(state)

    row0 = _iota((tb, LANES), 0) == 0
    for c0 in range(0, RW_GROUP, LANES):
        x = p_ref[:, c0:c0 + LANES].astype(F32)
        prev = pltpu.roll(x, 1, 0)
        prev = jnp.where(row0, carry[PACKED_ROWS - 1:PACKED_ROWS, c0:c0 + LANES], prev)
        pl_scr[:, c0:c0 + LANES] = x + (prev - x) * mu_ref[:, c0:c0 + LANES]
    carry[...] = p_ref[tb - PACKED_ROWS:tb, :].astype(F32)

    head_sum = lambda t: _mm(t, hones_ref[...])

    c_wd = 3 * W
    x128 = pl_scr[:, c_wd:c_wd + LANES]
    lane = _iota((tb, LANES), 1)
    xin = jnp.where(lane < RW_DECAY_RANK, jnp.tanh(x128), x128)
    wa = jnp.dot(xin.astype(BF16), wwa_ref[...], preferred_element_type=F32)
    w_log = -jax.nn.softplus(-(w0_ref[...] + wa[:, :W])) - 0.5
    lw = -jnp.exp(w_log)
    asig = _sigmoid(a0_ref[...] + wa[:, W:])
    gd = pl_scr[:, c_wd + LANES:c_wd + 2 * LANES]
    g = jnp.dot(_sigmoid(gd).astype(BF16), gup_ref[...], preferred_element_type=F32)

    v = pl_scr[:, 2 * W:3 * W]
    if has_vres:
        vd = pl_scr[:, RW_COLS:RW_COLS + LANES]
        vl = jnp.dot(vd.astype(BF16), vup_ref[...], preferred_element_type=F32)
        v = v + (vf_ref[...] - v) * _sigmoid(v0_ref[...] + vl)
    else:
        vout_ref[...] = v
    v_scr[...] = v

    k = pl_scr[:, W:2 * W]
    kkraw = k * kk_ref[...]
    ssq = head_sum(kkraw * kkraw)
    kk = kkraw * lax.rsqrt(ssq + 1e-12)
    kfin = k * (1.0 + (asig - 1.0) * ka_ref[...])
    r = pl_scr[:, 0:W]
    bonus = head_sum(r * kfin * rk_ref[...]) * v

    C = CHUNK
    P2 = 2 * C
    NC, NP = tb // C, RW_HEADS // 2
    tril_b = (_iota((NC, C, C), 1) >= _iota((NC, C, C), 2)).astype(BF16)
    cum = sum(_bmm(tril_b, t.reshape(NC, C, W)) for t in _split3(lw)).reshape(tb, W)
    e_pos = jnp.exp(cum)
    e_neg = jnp.exp(-cum)
    rt_scr[...] = r * e_pos
    kt_scr[...] = kfin * e_neg
    bt_scr[...] = (kk * asig) * e_neg
    at_scr[...] = -kk * jnp.exp(cum - lw)

    def to_b(ref):
        return jnp.concatenate([ref[c * C:(c + 1) * C, p * LANES:(p + 1) * LANES][None]
                                for c in range(NC) for p in range(NP)], axis=0)

    row, col = _iota((1, P2, P2), 1), _iota((1, P2, P2), 2)
    same_head = (row // C) == (col // C)
    strict = same_head & ((row % C) > (col % C))
    incl = same_head & ((row % C) >= (col % C))
    left = _iota((1, C, LANES), 2) < RW_HEAD_DIM
    leftf = left.astype(F32)
    rightf = 1.0 - leftf

    def sel(x3):
        return jnp.where(left, x3[:, :C], x3[:, C:])

    at_b, rt_b, bt_b, kt_b, vp_b = to_b(at_scr), to_b(rt_scr), to_b(bt_scr), to_b(kt_scr), to_b(v_scr)
    lhs = jnp.concatenate([at_b * leftf, rt_b * leftf, at_b * rightf, rt_b * rightf], axis=1)
    out_ab = _bmm_nt(lhs, jnp.concatenate([bt_b, kt_b, kt_b, bt_b], axis=1))
    out_a, out_b = out_ab[:, :, :P2], out_ab[:, :, P2:]
    ab = jnp.where(strict, jnp.concatenate([out_a[:, 0:C], out_b[:, 2 * C:3 * C]], axis=1), 0.0)
    ak = jnp.where(strict, jnp.concatenate([out_b[:, 0:C], out_a[:, 2 * C:3 * C]], axis=1), 0.0)
    rb = jnp.where(incl, jnp.concatenate([out_a[:, C:2 * C], out_b[:, 3 * C:4 * C]], axis=1), 0.0)
    rk = jnp.where(incl, jnp.concatenate([out_b[:, C:2 * C], out_a[:, 3 * C:4 * C]], axis=1), 0.0)
    t_scr[...] = _neumann_inverse(ab)
    rb_scr[...] = rb
    vv = jnp.concatenate([vp_b, vp_b], axis=1)
    akv_scr[...] = sel(_bmm(ak, vv))
    rkv_scr[...] = sel(_bmm(rk, vv))
    pc_rows = jnp.concatenate([e_pos[(c + 1) * C - 1:(c + 1) * C, p * LANES:(p + 1) * LANES][None]
                               for c in range(NC) for p in range(NP)], axis=0)
    btk_scr[...] = jnp.concatenate([bt_b * pc_rows, kt_b * pc_rows], axis=1)

    for c in range(NC):
        items = slice(c * NP, (c + 1) * NP)
        s = state[...]
        a_c, r_c, v_c = at_b[items], rt_b[items], vp_b[items]
        ars = _bmm_nt(jnp.concatenate([a_c, r_c], axis=1), s)
        rhs = ars[:, :C] + akv_scr[items]
        u = sel(_bmm(t_scr[items], jnp.concatenate([rhs, rhs], axis=1)))
        y = ars[:, C:] + sel(_bmm(rb_scr[items], jnp.concatenate([u, u], axis=1))) + rkv_scr[items]
        s_new = s * pc_rows[items] + _bmm_tn(jnp.concatenate([u, v_c], axis=1), btk_scr[items])
        state[...] = jnp.where(same_head, s_new, 0.0)
        for p in range(NP):
            y_scr[c * C:(c + 1) * C, p * LANES:(p + 1) * LANES] = y[p]

    y = y_scr[...]
    inv_n = 1.0 / RW_HEAD_DIM
    y_hi = y.astype(BF16).astype(F32)
    mean = (head_sum(y_hi) + head_sum(y - y_hi)) * inv_n
    yc = y - mean
    var = head_sum(yc * yc) * inv_n
    yn = yc * lax.rsqrt(var + RW_GN_EPS) * lnw_ref[...] + lnb_ref[...]
    o_ref[...] = ((yn + bonus) * g).astype(o_ref.dtype)


def _rwkv_mixer(p_rw, v_first, prm, bsz, seq, has_vres):
    m = p_rw.shape[0]
    tb = _pick_tile(seq, (512, 256, 128, 64))
    nt = seq // tb
    nb = (tb // CHUNK) * (RW_HEADS // 2)
    W = RW_WIDTH
    row_spec = lambda width: pl.BlockSpec((tb, width), lambda b, t: (b * nt + t, 0))
    full = lambda a: pl.BlockSpec(a.shape, lambda b, t: (0,) * a.ndim)
    names = (["mu", "w0", "a0", "wwa", "gup", "kk", "ka", "rk", "lnw", "lnb", "hones"]
             + (["v0", "vup"] if has_vres else []))
    params = [prm[n] for n in names]
    ins = [p_rw] + ([v_first] if has_vres else []) + params
    in_specs = [row_spec(RW_GROUP)] + ([row_spec(W)] if has_vres else []) + [full(a) for a in params]
    if has_vres:
        out_shape = jax.ShapeDtypeStruct((m, W), BF16)
        out_specs = row_spec(W)
    else:
        out_shape = (jax.ShapeDtypeStruct((m, W), BF16), jax.ShapeDtypeStruct((m, W), F32))
        out_specs = (row_spec(W), row_spec(W))
    scratch = [
        pltpu.VMEM((PACKED_ROWS, RW_GROUP), F32),
        pltpu.VMEM((RW_HEADS // 2, LANES, LANES), F32),
        pltpu.VMEM((tb, RW_GROUP), F32),
    ] + [pltpu.VMEM((tb, W), F32) for _ in range(6)] + [
        pltpu.VMEM((nb, 2 * CHUNK, 2 * CHUNK), F32),
        pltpu.VMEM((nb, 2 * CHUNK, 2 * CHUNK), F32),
        pltpu.VMEM((nb, CHUNK, LANES), F32),
        pltpu.VMEM((nb, CHUNK, LANES), F32),
        pltpu.VMEM((nb, 2 * CHUNK, LANES), F32),
    ]
    res = pl.pallas_call(
        functools.partial(_rwkv_kernel, has_vres, tb),
        out_shape=out_shape,
        grid=(bsz, nt),
        in_specs=in_specs,
        out_specs=out_specs,
        scratch_shapes=scratch,
        compiler_params=_cparams(("parallel", "arbitrary")),
        name="rwkv7_vres" if has_vres else "rwkv7",
    )(*ins)
    if has_vres:
        return res, None
    return res


def _gdn_kernel(tb, pg_ref, slab_ref, cw_ref, alog_ref, dtb_ref, onorm_ref, eb_ref, ea_ref, o_ref,
                carry, state, ext, q_scr, k_scr, v_scr, grow_scr, o_scr,
                gcf_scr, kb_scr, vb_scr, kbeg_scr, qeg_scr, kt2_scr,
                intra_scr, u_scr, wq_scr, kt_scr):
    H, Dh, Wd = GDN_HEADS, GDN_HEAD_DIM, GDN_WIDTH
    QKV = 3 * Wd
    tstep = pl.program_id(1)

    @pl.when(tstep == 0)
    def _():
        carry[...] = jnp.zeros_like(carry)
        state[...] = jnp.zeros_like(state)

    ext[0:PACKED_ROWS, :] = carry[...]
    ext[PACKED_ROWS:, :] = pg_ref[:, 0:QKV].astype(F32)
    carry[...] = pg_ref[tb - PACKED_ROWS:tb, 0:QKV].astype(F32)
    for j in range(QKV // Dh):
        ln = slice(j * Dh, (j + 1) * Dh)
        acc = _causal_taps(ext[pl.ds(PACKED_ROWS - SUBLANES, tb + SUBLANES), ln], cw_ref[:, ln])
        act = acc * _sigmoid(acc)
        which, h = divmod(j, H)
        hl = slice(h * Dh, (h + 1) * Dh)
        if which == 0:
            nrm = lax.rsqrt(jnp.sum(act * act, axis=-1, keepdims=True) + 1e-6)
            q_scr[:, hl] = act * nrm * (Dh ** -0.5)
        elif which == 1:
            nrm = lax.rsqrt(jnp.sum(act * act, axis=-1, keepdims=True) + 1e-6)
            k_scr[:, hl] = act * nrm
        else:
            v_scr[:, hl] = act

    slab = slab_ref[...].astype(F32)
    gsl =-jnp.exp(alog_ref[...]) * jax.nn.softplus(slab + dtb_ref[...])
    lane = _iota((tb, LANES), 1)
    gsl = jnp.where((lane >= SLAB_ALPHA0) & (lane < SLAB_ALPHA0 + H), gsl, 0.0)

    C = CHUNK
    NC = tb // C
    trow, tcol = _iota((tb, tb), 0), _iota((tb, tb), 1)
    same_chunk = (trow // C) == (tcol // C)
    gcol = _mm_exact_lhs((same_chunk & (trow >= tcol)).astype(BF16), gsl)
    grow_scr[...] = _mm_tn_exact_rhs(gsl, (same_chunk & (trow <= tcol)).astype(BF16))

    def spread(x, e_ref):
        return sum(_mm(t, e_ref[...]) for t in _split3(x))

    beta_f = spread(_sigmoid(slab), eb_ref)
    gc_f = spread(gcol, ea_ref)
    eg_f = jnp.exp(gc_f)
    k2 = k_scr[...]
    kb2 = k2 * beta_f
    gc3 = gc_f.reshape(NC, C, Wd)
    gl3 = gc3[:, C - 1:C, :]
    gcf_scr[...] = gc_f
    kb_scr[...] = kb2
    vb_scr[...] = v_scr[...] * beta_f
    kbeg_scr[...] = kb2 * eg_f
    qeg_scr[...] = q_scr[...] * eg_f
    kt2_scr[...] = (k2.reshape(NC, C, Wd) * jnp.exp(gl3 - gc3)).reshape(tb, Wd)
    egl3 = jnp.exp(gl3)

    def items(fn):
        return jnp.concatenate([fn(c, h)[None] for c in range(NC) for h in range(H)], axis=0)

    rows = lambda c: slice(c * C, (c + 1) * C)
    head = lambda h: slice(h * Dh, (h + 1) * Dh)
    tile = lambda ref: items(lambda c, h: ref[rows(c), head(h)])
    gci = items(lambda c, h: gcf_scr[rows(c), h * Dh:h * Dh + C])
    gr = items(lambda c, h: grow_scr[SLAB_ALPHA0 + h:SLAB_ALPHA0 + h + 1, rows(c)])
    eg_last = jnp.concatenate([egl3[c:c + 1, :, head(h)] for c in range(NC) for h in range(H)], axis=0)
    k_b = tile(k_scr)
    row, col = _iota((1, C, C), 1), _iota((1, C, C), 2)
    dec = jnp.exp(jnp.where(row >= col, gci - gr, -jnp.inf))
    prod = _bmm_nt(jnp.concatenate([tile(kb_scr), tile(q_scr)], axis=1), k_b)
    lmat = jnp.where(row > col, prod[:, :C] * dec, 0.0)
    intra_scr[...] = prod[:, C:] * dec
    nb = NC * H
    l2 = (-lmat).reshape(nb // 2, 2 * C, C)
    prow, pcol = _iota((1, 2 * C, 2 * C), 1), _iota((1, 2 * C, 2 * C), 2)
    lp = jnp.where((prow // C) == (pcol // C), jnp.concatenate([l2, l2], axis=2), 0.0)
    tp = _neumann_inverse(lp)
    rhs = jnp.concatenate([tile(vb_scr), tile(kbeg_scr)], axis=2).reshape(nb // 2, 2 * C, 2 * Dh)
    sol = _bmm(tp, rhs).reshape(nb, C, 2 * Dh)
    u_scr[...] = sol[:, :, :Dh]
    wq_scr[...] = jnp.concatenate([sol[:, :, Dh:], tile(qeg_scr)], axis=1)
    kt_scr[...] = tile(kt2_scr)

    for c in range(NC):
        it = slice(c * H, (c + 1) * H)
        s = state[...]
        ws = _bmm(wq_scr[it], s)
        v_new = u_scr[it] - ws[:, :C]
        o = ws[:, C:] + _bmm(intra_scr[it], v_new)
        state[...] = s * eg_last[it] + _bmm_tn(kt_scr[it], v_new)
        for h in range(H):
            o_scr[rows(c), head(h)] = o[h]

    for h in range(H):
        hl = slice(h * Dh, (h + 1) * Dh)
        o = o_scr[:, hl]
        o = o * lax.rsqrt(jnp.mean(o * o, axis=-1, keepdims=True) + NORM_EPS) * onorm_ref[...]
        z = pg_ref[:, QKV + h * Dh:QKV + (h + 1) * Dh].astype(F32)
        o_ref[:, hl] = (o * (z * _sigmoid(z))).astype(o_ref.dtype)


def _gdn_mixer(p_gdn, p_rw, prm, bsz, seq):
    m = p_gdn.shape[0]
    tb = _pick_tile(seq, (256, 128))
    nt = seq // tb
    nb = (tb // CHUNK) * GDN_HEADS
    Wd = GDN_WIDTH
    lane_head = np.arange(Wd)[None, :] // GDN_HEAD_DIM
    spread = lambda off: jnp.asarray(np.arange(LANES)[:, None] - off == lane_head, BF16)
    params = [prm["cw"], prm["alog"], prm["dtb"], prm["onorm"], spread(SLAB_BETA0), spread(SLAB_ALPHA0)]
    full = lambda a: pl.BlockSpec(a.shape, lambda b, t: (0,) * a.ndim)
    slab_blk = RW_COLS // LANES
    return pl.pallas_call(
        functools.partial(_gdn_kernel, tb),
        out_shape=jax.ShapeDtypeStruct((m, Wd), BF16),
        grid=(bsz, nt),
        in_specs=[pl.BlockSpec((tb, 4 * Wd), lambda b, t: (b * nt + t, 0)),
                  pl.BlockSpec((tb, LANES), lambda b, t: (b * nt + t, slab_blk))] + [full(a) for a in params],
        out_specs=pl.BlockSpec((tb, Wd), lambda b, t: (b * nt + t, 0)),
        scratch_shapes=[
            pltpu.VMEM((PACKED_ROWS, 3 * Wd), F32),
            pltpu.VMEM((GDN_HEADS, GDN_HEAD_DIM, GDN_HEAD_DIM), F32),
            pltpu.VMEM((tb + PACKED_ROWS, 3 * Wd), F32),
            pltpu.VMEM((tb, Wd), F32), pltpu.VMEM((tb, Wd), F32), pltpu.VMEM((tb, Wd), F32),
            pltpu.VMEM((LANES, tb), F32),
            pltpu.VMEM((tb, Wd), F32),
        ] + [pltpu.VMEM((tb, Wd), F32) for _ in range(6)] + [
            pltpu.VMEM((nb, CHUNK, CHUNK), F32),
            pltpu.VMEM((nb, CHUNK, GDN_HEAD_DIM), F32),
            pltpu.VMEM((nb, 2 * CHUNK, GDN_HEAD_DIM), F32),
            pltpu.VMEM((nb, CHUNK, GDN_HEAD_DIM), F32),
        ],
        compiler_params=_cparams(("parallel", "arbitrary")),
        name="gdn",
    )(p_gdn, p_rw, *params)


def _swa_kernel(nblk, ps_ref, pos_ref, freq_ref, sgn_lo_ref, sgn_hi_ref, qn_ref, kn_ref, sink_ref, ones_ref, o_ref,
                kprev, vprev):
    Wn = SWA_WINDOW
    hd = SWA_HEAD_DIM
    G = SWA_Q_HEADS // SWA_KV_HEADS
    QW, KW = SWA_WIDTH, SWA_KV_WIDTH
    n = pl.program_id(1)

    @pl.when(n == 0)
    def _():
        kprev[...] = jnp.zeros_like(kprev)
        vprev[...] = jnp.zeros_like(vprev)

    half = ROPE_DIM // 2
    ang = pos_ref[...] * freq_ref[...]
    cs1, sn1 = jnp.cos(ang), jnp.sin(ang)
    lo1, hi1 = sn1 * sgn_lo_ref[...], sn1 * sgn_hi_ref[...]

    def norm_rope(x, gain_row, width):
        rep = lambda t: jnp.concatenate([t] * (width // LANES), axis=1)
        ms = _mm(x * x, ones_ref[0:width, 0:width]) * (1.0 / hd)
        y = x * lax.rsqrt(ms + NORM_EPS) * gain_row
        up = pltpu.roll(y, width - half, 1)
        dn = pltpu.roll(y, half, 1)
        return y * rep(cs1) + up * rep(lo1) + dn * rep(hi1)

    q = norm_rope(ps_ref[:, 0:QW].astype(F32), qn_ref[...], QW) * (hd ** -0.5)
    k = norm_rope(ps_ref[:, QW:QW + KW].astype(F32), kn_ref[...], KW)
    v = ps_ref[:, QW + KW:QW + 2 * KW].astype(F32)

    qi = _iota((G * Wn, 2 * Wn), 0) % Wn
    kj = _iota((G * Wn, 2 * Wn), 1)
    rel = qi + Wn - kj
    in_window = (rel >= 0) & (rel < SWA_WINDOW)
    grp = _iota((G * Wn, 1), 0) // Wn
    sinks = sink_ref[...]
    first_half = _iota((G * Wn, LANES), 1) < hd
    side_mask = (first_half.astype(F32), 1.0 - first_half.astype(F32))
    ones_kv = jnp.ones((2 * Wn, LANES), BF16)

    for blk in range(nblk):
        rows = slice(blk * Wn, (blk + 1) * Wn)
        before = slice((blk - 1) * Wn, blk * Wn)
        allowed = in_window & ((kj >= Wn) | (n > 0)) if blk == 0 else in_window
        for pair in range(SWA_KV_HEADS // 2):
            kl = slice(pair * LANES, (pair + 1) * LANES)
            k_before = kprev[:, kl] if blk == 0 else k[before, kl]
            v_before = vprev[:, kl] if blk == 0 else v[before, kl]
            kband = jnp.concatenate([k_before, k[rows, kl]], axis=0).astype(BF16)
            vband = jnp.concatenate([v_before, v[rows, kl]], axis=0).astype(BF16)
            qs = jnp.concatenate([q[rows, (pair * G + t) * LANES:(pair * G + t + 1) * LANES] for t in range(G)],
                                 axis=0)
            outs = []
            for side in range(2):
                s = jnp.where(allowed, _mm_nt(qs * side_mask[side], kband), -jnp.inf)
                sink = jnp.zeros((G * Wn, 1), F32)
                for t in range(G):
                    h = SWA_Q_ORDER[(pair * G + t) * 2 + side]
                    sink = jnp.where(grp == t, sinks[:, h:h + 1], sink)
                mx = jnp.maximum(jnp.max(s, axis=-1, keepdims=True), sink)
                p = jnp.exp(s - mx).astype(BF16)
                den = jnp.dot(p, ones_kv, preferred_element_type=F32) + jnp.exp(sink - mx)
                outs.append(jnp.dot(p, vband, preferred_element_type=F32) / den)
            o = jnp.where(first_half, outs[0], outs[1])
            for t in range(G):
                o_ref[rows, (pair * G + t) * LANES:(pair * G + t + 1) * LANES] = (
                    o[t * Wn:(t + 1) * Wn].astype(o_ref.dtype))

    last = slice((nblk - 1) * Wn, nblk * Wn)
    kprev[...] = k[last, :]
    vprev[...] = v[last, :]


def _swa_mixer(p_swa, pos_col, prm, bsz, seq):
    m = p_swa.shape[0]
    Wn = SWA_WINDOW
    nblk = next(c for c in (4, 2, 1) if seq % (c * Wn) == 0)
    rows = nblk * Wn
    nb = seq // rows
    head_ones = _head_ones(SWA_WIDTH, SWA_HEAD_DIM)
    params = [prm["freq"], prm["sgn_lo"], prm["sgn_hi"], prm["qn"], prm["kn"], prm["sinks"], head_ones]
    full = lambda a: pl.BlockSpec(a.shape, lambda b, t: (0,) * a.ndim)
    return pl.pallas_call(
        functools.partial(_swa_kernel, nblk),
        out_shape=jax.ShapeDtypeStruct((m, SWA_WIDTH), BF16),
        grid=(bsz, nb),
        in_specs=[pl.BlockSpec((rows, SWA_COLS), lambda b, t: (b * nb + t, 0)),
                  pl.BlockSpec((rows, 1), lambda b, t: (b * nb + t, 0))] + [full(a) for a in params],
        out_specs=pl.BlockSpec((rows, SWA_WIDTH), lambda b, t: (b * nb + t, 0)),
        scratch_shapes=[pltpu.VMEM((Wn, SWA_KV_WIDTH), F32), pltpu.VMEM((Wn, SWA_KV_WIDTH), F32)],
        compiler_params=_cparams(("parallel", "arbitrary")),
        name="swa",
    )(p_swa, pos_col, *params)


def _merge_kernel(x_ref, h_ref, orw_ref, ogdn_ref, oswa_ref, wg0_ref, wg1_ref, wg2_ref,
                  wb0_ref, wb1_ref, wb2_ref, wout_ref, o_ref, acc):
    j = pl.program_id(1)

    @pl.when(j == 0)
    def _():
        acc[...] = jnp.zeros_like(acc)

    h = h_ref[...]
    branch = lambda wg_ref, a_ref, wb_ref: (
        _sigmoid(jnp.dot(h, wg_ref[...], preferred_element_type=F32))
        * jnp.dot(a_ref[...], wb_ref[...], preferred_element_type=F32))
    merged = (branch(wg0_ref, orw_ref, wb0_ref) + branch(wg1_ref, ogdn_ref, wb1_ref)
              + branch(wg2_ref, oswa_ref, wb2_ref))
    acc[...] += jnp.dot(merged.astype(BF16), wout_ref[...], preferred_element_type=F32)

    @pl.when(j == pl.num_programs(1) - 1)
    def _():
        o_ref[...] = x_ref[...] + acc[...]


def _merge(x2, h_bf16, o_rw, o_gdn, o_swa, w_gate, wb_rw, wb_gdn, wb_swa, w_out, layer):
    m, d = x2.shape
    tm = _pick_tile(m, (512, 256, 128, 64, 32, 16, 8))
    tn = 512
    nj = d // tn
    rows = lambda width: pl.BlockSpec((tm, width), lambda i, j: (i, 0))
    gate = lambda b: pl.BlockSpec((d, tn), lambda i, j, b=b: (0, b * nj + j))
    wcol = lambda kdim: pl.BlockSpec((kdim, tn), lambda i, j: (0, j))
    return pl.pallas_call(
        _merge_kernel,
        out_shape=jax.ShapeDtypeStruct((m, d), F32),
        grid=(m // tm, nj),
        in_specs=[rows(d), rows(d),
                  rows(RW_WIDTH), rows(GDN_WIDTH), rows(SWA_WIDTH), gate(0), gate(1), gate(2),
                  wcol(RW_WIDTH), wcol(GDN_WIDTH), wcol(SWA_WIDTH),
                  pl.BlockSpec((None, tn, d), lambda i, j: (layer, j, 0))],
        out_specs=rows(d),
        scratch_shapes=[pltpu.VMEM((tm, d), F32)],
        compiler_params=_cparams(("parallel", "arbitrary")),
        name="merge",
    )(x2, h_bf16, o_rw, o_gdn, o_swa, w_gate, w_gate, w_gate, wb_rw, wb_gdn, wb_swa, w_out)


def _ffn_up_kernel(tm, seq, x_ref, halo_ref, gain_ref, wg_ref, wu_ref, cg_ref, cu_ref, act_ref, h_scr, u_scr):
    i = pl.program_id(0)
    j = pl.program_id(1)
    tn = wg_ref.shape[1]

    @pl.when(j == 0)
    def _():
        def norm(x):
            return (x * lax.rsqrt(jnp.mean(x * x, axis=-1, keepdims=True) + NORM_EPS) * gain_ref[...]).astype(BF16)
        seq_start = (i * tm) % seq == 0
        halo = jnp.where(seq_start, 0.0, halo_ref[...])
        h_scr[0:2 * SUBLANES, :] = norm(jnp.concatenate([jnp.zeros_like(halo), halo], axis=0))
        h_scr[2 * SUBLANES:, :] = norm(x_ref[...])

    h = h_scr[...]
    ts = FFN_SUBTILE
    for s in range(tn // ts):
        cols = slice(s * ts, (s + 1) * ts)

        def conv(w_ref, cw_ref, base):
            u_scr[:, base:base + ts] = jnp.dot(h, w_ref[:, cols], preferred_element_type=F32)
            return _causal_taps(u_scr[pl.ds(SUBLANES, tm + SUBLANES), base:base + ts], cw_ref[:, cols])

        gate = conv(wg_ref, cg_ref, 2 * s * ts)
        up = conv(wu_ref, cu_ref, (2 * s + 1) * ts)
        act_ref[:, cols] = (gate * _sigmoid(gate) * up).astype(BF16)


def _ffn_down_kernel(act_ref, wd_ref, x_ref, o_ref):
    o_ref[...] = x_ref[...] + jnp.dot(act_ref[...], wd_ref[...], preferred_element_type=F32)


def _conv_ffn(x2, gain, w_up2, c_up2, w_down, layer, seq):
    m, d = x2.shape
    dff = w_down.shape[1]
    tm = _pick_tile(seq, (1024, 512, 256, 128, 64, 32, 16))
    tn = _pick_tile(dff, (512, 256, 128))
    nj = dff // tn
    hb = tm // SUBLANES
    act = pl.pallas_call(
        functools.partial(_ffn_up_kernel, tm, seq),
        out_shape=jax.ShapeDtypeStruct((m, dff), BF16),
        grid=(m // tm, dff // tn),
        in_specs=[
            pl.BlockSpec((tm, d), lambda i, j: (i, 0)),
            pl.BlockSpec((SUBLANES, d), lambda i, j: (jnp.maximum(i * hb - 1, 0), 0)),
            pl.BlockSpec((1, d), lambda i, j: (0, 0)),
            pl.BlockSpec((None, d, tn), lambda i, j: (layer, 0, j)),
            pl.BlockSpec((None, d, tn), lambda i, j: (layer, 0, nj + j)),
            pl.BlockSpec((FFN_CONV, tn), lambda i, j: (0, j)),
            pl.BlockSpec((FFN_CONV, tn), lambda i, j: (0, nj + j)),
        ],
        out_specs=pl.BlockSpec((tm, tn), lambda i, j: (i, j)),
        scratch_shapes=[pltpu.VMEM((tm + 2 * SUBLANES, d), BF16),
                        pltpu.VMEM((tm + 2 * SUBLANES, 2 * tn), F32)],
        compiler_params=_cparams(("parallel", "arbitrary")),
        name="ffn_up",
    )(x2, x2, gain.reshape(1, d), w_up2, w_up2, c_up2, c_up2)
    tm2 = _pick_tile(m, (1024, 512, 256, 128, 64, 32, 16, 8))
    tn2 = _pick_tile(d, (512, 256, 128))
    return pl.pallas_call(
        _ffn_down_kernel,
        out_shape=jax.ShapeDtypeStruct((m, d), F32),
        grid=(m // tm2, d // tn2),
        in_specs=[
            pl.BlockSpec((tm2, dff), lambda i, j: (i, 0)),
            pl.BlockSpec((None, dff, tn2), lambda i, j: (layer, 0, j)),
            pl.BlockSpec((tm2, tn2), lambda i, j: (i, j)),
        ],
        out_specs=pl.BlockSpec((tm2, tn2), lambda i, j: (i, j)),
        compiler_params=_cparams(("parallel", "arbitrary")),
        name="ffn_down",
    )(act, w_down, x2)


def _row(a):
    return a.reshape(1, -1).astype(F32)


def _head_ones(width, head_dim):
    idx = np.arange(width) // head_dim
    return jnp.asarray(idx[:, None] == idx[None, :], BF16)


def _swa_head_order(a, axis):
    hd = SWA_HEAD_DIM
    take = lambda h: lax.slice_in_dim(a, h * hd, (h + 1) * hd, axis=axis)
    return jnp.concatenate([take(h) for h in SWA_Q_ORDER], axis=axis)


def _pad_cols(a, width):
    return jnp.pad(a, ((0, 0), (0, width - a.shape[1])))


def _layer_params(l, w_in, rw_mu, rw_w0, rw_w_up, rw_a0, rw_a_up, rw_g_up, rw_k_k, rw_k_a, rw_r_k, rw_ln_w,
                  rw_ln_b, vres_down, vres_mu, vres_v0, vres_v_up, gdn_conv, gdn_a_log, gdn_dt_bias, gdn_o_norm,
                  swa_q_norm, swa_k_norm, swa_sinks):
    d = w_in.shape[1]
    c_gdn = RW_COLS
    c_ba = c_gdn + 4 * GDN_WIDTH
    c_swa = c_ba + 2 * GDN_HEADS
    c_gate = c_swa + SWA_COLS
    has_vres = l > 0
    vres_w = vres_down[l - 1] if has_vres else jnp.zeros((d, RW_VRES_RANK), F32)
    assert c_ba % LANES == 0
    ba_w = _pack_cols(w_in, l, c_ba, LANES)[:, :2 * GDN_HEADS]
    slab_w = _pad_cols(jnp.concatenate([vres_w.astype(BF16), ba_w], axis=1), LANES)
    w_rw = jnp.concatenate([_pack_cols(w_in, l, 0, RW_COLS), slab_w], axis=1)
    w_gdn = _pack_cols(w_in, l, c_gdn, 4 * GDN_WIDTH)
    w_swa = _pack_cols(w_in, l, c_swa, SWA_COLS)
    w_swa = jnp.concatenate([_swa_head_order(w_swa[:, :SWA_WIDTH], axis=1), w_swa[:, SWA_WIDTH:]], axis=1)
    w_gate = _pack_cols(w_in, l, c_gate, w_in.shape[2] - c_gate)

    W = RW_WIDTH
    mu = jnp.concatenate([rw_mu[l], vres_mu[l - 1] if has_vres else jnp.zeros((RW_VRES_RANK,), F32),
                          jnp.zeros((LANES - RW_VRES_RANK,), F32)])
    wwa = jnp.zeros((LANES, 2 * W), F32)
    wwa = wwa.at[:RW_DECAY_RANK, :W].set(rw_w_up[l]).at[RW_DECAY_RANK:, W:].set(rw_a_up[l])
    rw = dict(mu=_row(mu), w0=_row(rw_w0[l]), a0=_row(rw_a0[l]), wwa=wwa.astype(BF16),
              gup=rw_g_up[l].astype(BF16), kk=_row(rw_k_k[l]), ka=_row(rw_k_a[l]), rk=_row(rw_r_k[l]),
              lnw=_row(rw_ln_w[l]), lnb=_row(rw_ln_b[l]), hones=_head_ones(W, RW_HEAD_DIM))
    if has_vres:
        vup = jnp.zeros((LANES, W), F32).at[:RW_VRES_RANK].set(vres_v_up[l - 1])
        rw.update(v0=_row(vres_v0[l - 1]), vup=vup.astype(BF16))

    lanes_pad = lambda a, off: jnp.zeros((1, LANES), F32).at[0, off:off + a.shape[0]].set(a)
    gdn = dict(cw=gdn_conv[l].astype(F32), alog=lanes_pad(gdn_a_log[l], SLAB_ALPHA0),
               dtb=lanes_pad(gdn_dt_bias[l], SLAB_ALPHA0), onorm=_row(gdn_o_norm[l]))

    half = ROPE_DIM // 2
    lane = np.arange(LANES) % SWA_HEAD_DIM
    inv_freq = ROPE_THETA ** (-(lane % half).astype(np.float32) * 2.0 / ROPE_DIM)
    freq = np.where(lane < ROPE_DIM, inv_freq, 0.0).astype(np.float32)
    sgn_lo = np.where(lane < half, -1.0, 0.0).astype(np.float32)
    sgn_hi = np.where((lane >= half) & (lane < ROPE_DIM), 1.0, 0.0).astype(np.float32)
    swa = dict(freq=jnp.asarray(freq).reshape(1, -1), sgn_lo=jnp.asarray(sgn_lo).reshape(1, -1),
               sgn_hi=jnp.asarray(sgn_hi).reshape(1, -1),
               qn=_row(jnp.tile(swa_q_norm[l], SWA_Q_HEADS)), kn=_row(jnp.tile(swa_k_norm[l], SWA_KV_HEADS)),
               sinks=lanes_pad(swa_sinks[l], 0))
    groups = (w_rw, w_gdn, w_swa)
    mix_widths = tuple(pl.cdiv(g.shape[1], INPROJ_TN) * INPROJ_TN for g in groups)
    w_mix = jnp.concatenate([_pad_cols(g, wd) for g, wd in zip(groups, mix_widths)], axis=1)
    return dict(w_mix=w_mix, mix_widths=mix_widths, w_gate=w_gate, rw=rw, gdn=gdn, swa=swa, has_vres=has_vres)


def kernel(x, positions, norm_mix, w_in, rw_mu, rw_w0, rw_w_up, rw_a0, rw_a_up, rw_g_up, rw_k_k, rw_k_a, rw_r_k, rw_ln_w, rw_ln_b, vres_down, vres_mu, vres_v0, vres_v_up, gdn_conv, gdn_a_log, gdn_dt_bias, gdn_o_norm, swa_q_norm, swa_k_norm, swa_sinks, w_branch, w_out, norm_ffn, ffn_up, ffn_conv, ffn_down):
    bsz, seq, d = x.shape
    depth = w_in.shape[0]
    x2 = x.reshape(bsz * seq, d)
    pos_col = positions.astype(F32).reshape(bsz * seq, 1)
    v_first = None
    ffn_up_bf16, ffn_down_bf16, w_out_bf16 = ffn_up.astype(BF16), ffn_down.astype(BF16), w_out.astype(BF16)
    for l in range(depth):
        lp = _layer_params(l, w_in, rw_mu, rw_w0, rw_w_up, rw_a0, rw_a_up, rw_g_up, rw_k_k, rw_k_a, rw_r_k,
                           rw_ln_w, rw_ln_b, vres_down, vres_mu, vres_v0, vres_v_up, gdn_conv, gdn_a_log,
                           gdn_dt_bias, gdn_o_norm, swa_q_norm, swa_k_norm, swa_sinks)
        p_rw, p_gdn, p_swa, h_mix = _inproj(x2, norm_mix[l], lp["w_mix"], lp["mix_widths"])
        if lp["has_vres"]:
            o_rw, _ = _rwkv_mixer(p_rw, v_first, lp["rw"], bsz, seq, True)
        else:
            o_rw, v_first = _rwkv_mixer(p_rw, None, lp["rw"], bsz, seq, False)
        o_gdn = _gdn_mixer(p_gdn, p_rw, lp["gdn"], bsz, seq)
        o_swa = _swa_mixer(p_swa, pos_col, lp["swa"], bsz, seq)
        wb = w_branch[l].astype(BF16)
        x2 = _merge(x2, h_mix, o_rw, o_gdn, o_swa, lp["w_gate"], wb[:RW_WIDTH], wb[RW_WIDTH:RW_WIDTH + GDN_WIDTH],
                    _swa_head_order(wb[RW_WIDTH + GDN_WIDTH:], axis=0), w_out_bf16, l)
        x2 = _conv_ffn(x2, norm_ffn[l], ffn_up_bf16, ffn_conv[l], ffn_down_bf16, l, seq)
    return x2.reshape(bsz, seq, d)
```
